```python
import math
import jax, jax.numpy as jnp
from jax import lax
import numpy as np

D_MODEL = 2048
BATCH = 4
SEQ = 2048
DEPTH = 1

CHUNK = 64
N_META = 16
HEAD_DIM = 128
N_HEADS_FOX = 8
N_HEADS_SB = 8
D_FOX = N_HEADS_FOX * HEAD_DIM
D_SB = N_HEADS_SB * HEAD_DIM
D_MIX = D_FOX + D_SB
D_IN_PROJ = 4 * D_FOX + 3 * D_SB + N_HEADS_FOX
Q_BLOCK = 128
N_GROUPS = 8
EXPERTS_PER_GROUP = 8
N_EXPERTS = N_GROUPS * EXPERTS_PER_GROUP
TOP_K = 2
D_EXPERT = 512
EXPERT_BLOCK = 128
EPS = 1e-6

kernel_name = "hymba_fox_stickbreak_hiermoe"


def rmsnorm(x, g):
    xf = x.astype(jnp.float32)
    y = xf * lax.rsqrt(jnp.mean(xf * xf, axis=-1, keepdims=True) + EPS)
    return (y * g.astype(jnp.float32)).astype(x.dtype)


def _split_heads(t, n_heads):
    b, l, _ = t.shape
    return t.reshape(b, l, n_heads, HEAD_DIM).transpose(0, 2, 1, 3)


def _pad_seq(t, pad):
    widths = [(0, 0)] * t.ndim
    widths[2] = (0, pad)
    return jnp.pad(t, widths)


def parallel_mixer(u, w_in, b_f, q_gain, k_gain, fox_out_gain, sb_out_gain, w_out):
    bsz, seq_len, _ = u.shape
    proj = u @ w_in
    cuts = [D_FOX, 2 * D_FOX, 3 * D_FOX, 4 * D_FOX,
            4 * D_FOX + D_SB, 4 * D_FOX + 2 * D_SB, 4 * D_FOX + 3 * D_SB]
    q_a, k_a, v_a, g_a, q_b, k_b, v_b, f_a = jnp.split(proj, cuts, axis=-1)
    q_a = rmsnorm(_split_heads(q_a, N_HEADS_FOX), q_gain)
    k_a = rmsnorm(_split_heads(k_a, N_HEADS_FOX), k_gain)
    v_a = _split_heads(v_a, N_HEADS_FOX)
    log_f = jax.nn.log_sigmoid((f_a + b_f).astype(jnp.float32)).transpose(0, 2, 1)
    cum_f = jnp.cumsum(log_f, axis=-1)
    q_b = _split_heads(q_b, N_HEADS_SB)
    k_b = _split_heads(k_b, N_HEADS_SB)
    v_b = _split_heads(v_b, N_HEADS_SB)

    n_blocks = -(-seq_len // Q_BLOCK)
    pad = n_blocks * Q_BLOCK - seq_len
    q_a, k_a, v_a, q_b, k_b, v_b, cum_f = (
        _pad_seq(t, pad) for t in (q_a, k_a, v_a, q_b, k_b, v_b, cum_f))
    pos = jnp.arange(n_blocks * Q_BLOCK)
    scale = HEAD_DIM ** -0.5

    outs_a, outs_b = [], []
    for blk in range(n_blocks):
        q0, q1 = blk * Q_BLOCK, (blk + 1) * Q_BLOCK
        qpos, kpos = pos[q0:q1], pos[:q1]
        causal = kpos[None, :] <= qpos[:, None]
        strict = kpos[None, :] < qpos[:, None]
        s = jnp.einsum('bhqd,bhkd->bhqk', q_a[:, :, q0:q1], k_a[:, :, :q1]).astype(jnp.float32) * scale
        s = s + cum_f[:, :, q0:q1, None] - cum_f[:, :, None, :q1]
        p = jax.nn.softmax(jnp.where(causal, s, -jnp.inf), axis=-1)
        outs_a.append(jnp.einsum('bhqk,bhkd->bhqd', p.astype(v_a.dtype), v_a[:, :, :q1]))
        z = jnp.einsum('bhqd,bhkd->bhqk', q_b[:, :, q0:q1], k_b[:, :, :q1]).astype(jnp.float32) * scale
        log_keep = jnp.where(strict, jax.nn.log_sigmoid(-z), 0.0)
        later = lax.cumsum(log_keep, axis=3, reverse=True) - log_keep
        a = jnp.where(strict, jnp.exp(jax.nn.log_sigmoid(z) + later), 0.0)
        outs_b.append(jnp.einsum('bhqk,bhkd->bhqd', a.astype(v_b.dtype), v_b[:, :, :q1]))

    o_a = jnp.concatenate(outs_a, axis=2)[:, :, :seq_len].transpose(0, 2, 1, 3)
    o_b = jnp.concatenate(outs_b, axis=2)[:, :, :seq_len].transpose(0, 2, 1, 3)
    gate = jax.nn.sigmoid(g_a.reshape(bsz, seq_len, N_HEADS_FOX, HEAD_DIM))
    o_a = rmsnorm(o_a, fox_out_gain) * gate
    o_b = rmsnorm(o_b, sb_out_gain)
    mixed = jnp.concatenate([o_a.reshape(bsz, seq_len, D_FOX),
                             o_b.reshape(bsz, seq_len, D_SB)], axis=-1)
    return mixed @ w_out


def _swiglu(xb, w1, w3, w2):
    return (jax.nn.silu(xb @ w1) * (xb @ w3)) @ w2


def expert_dispatch(xf, eid, wts, w1, w3, w2):
    n_tok, d = xf.shape
    n_assign = n_tok * TOP_K
    flat_e = eid.reshape(-1)
    flat_w = wts.reshape(-1)
    order = jnp.argsort(flat_e)
    sorted_e = flat_e[order]
    sorted_tok = order // TOP_K
    sorted_w = flat_w[order]
    counts = jnp.zeros((N_EXPERTS,), jnp.int32).at[flat_e].add(1)
    padded = (counts + EXPERT_BLOCK - 1) // EXPERT_BLOCK * EXPERT_BLOCK
    padded_ends = jnp.cumsum(padded)
    padded_starts = padded_ends - padded
    starts = jnp.cumsum(counts) - counts
    dest = padded_starts[sorted_e] + (jnp.arange(n_assign) - starts[sorted_e])
    n_rows = -(-(n_assign + N_EXPERTS * (EXPERT_BLOCK - 1)) // EXPERT_BLOCK) * EXPERT_BLOCK
    n_blk = n_rows // EXPERT_BLOCK
    row_tok = jnp.full((n_rows,), n_tok, jnp.int32).at[dest].set(sorted_tok)
    xpad = jnp.concatenate([xf, jnp.zeros((1, d), xf.dtype)], axis=0)
    xbuf = xpad[row_tok].reshape(n_blk, EXPERT_BLOCK, d)
    block_e = jnp.minimum(
        jnp.searchsorted(padded_ends, jnp.arange(n_blk) * EXPERT_BLOCK, side='right'),
        N_EXPERTS - 1)
    ybuf = lax.map(lambda a: _swiglu(a[0], w1[a[1]], w3[a[1]], w2[a[1]]), (xbuf, block_e))
    ybuf = ybuf.reshape(n_rows, d)
    contrib = ybuf[dest] * sorted_w[:, None].astype(ybuf.dtype)
    return jax.ops.segment_sum(contrib, sorted_tok, num_segments=n_tok)


def hier_moe(u, w_coarse, b_coarse, w_fine, b_fine, w1, w3, w2):
    bsz, seq_len, d = u.shape
    xf = u.reshape(-1, d)
    coarse = (xf @ w_coarse).astype(jnp.float32) + b_coarse
    p_group = jax.nn.softmax(coarse, axis=-1)
    g_sel = jnp.argmax(coarse, axis=-1)
    g_gate = jnp.take_along_axis(p_group, g_sel[:, None], axis=-1)
    fine_all = ((xf @ w_fine).astype(jnp.float32) + b_fine).reshape(-1, N_GROUPS, EXPERTS_PER_GROUP)
    fine = jnp.take_along_axis(fine_all, g_sel[:, None, None], axis=1)[:, 0]
    top_val, top_idx = lax.top_k(fine, TOP_K)
    wts = g_gate * jax.nn.softmax(top_val, axis=-1)
    eid = (g_sel[:, None] * EXPERTS_PER_GROUP + top_idx).astype(jnp.int32)
    y = expert_dispatch(xf, eid, wts, w1, w3, w2)
    return y.reshape(bsz, seq_len, d)


def setup_inputs(seed: int = 0) -> dict:
    key = jax.random.key(seed)
    ks = jax.random.split(key, 20)
    nrm = jax.random.normal
    f32 = jnp.float32
    return {
        "x": nrm(ks[0], (BATCH, SEQ, D_MODEL), f32),
        "meta_tokens": nrm(ks[1], (N_META, D_MODEL), f32),
        "norm1_g": 1.0 + 0.02 * nrm(ks[2], (DEPTH, D_MODEL), f32),
        "w_in": nrm(ks[3], (DEPTH, D_MODEL, D_IN_PROJ), f32) * D_MODEL ** -0.5,
        "b_f": 0.1 * nrm(ks[4], (DEPTH, N_HEADS_FOX), f32),
        "q_gain": 1.0 + 0.02 * nrm(ks[5], (DEPTH, HEAD_DIM), f32),
        "k_gain": 1.0 + 0.02 * nrm(ks[6], (DEPTH, HEAD_DIM), f32),
        "fox_out_gain": 1.0 + 0.02 * nrm(ks[7], (DEPTH, N_HEADS_FOX, HEAD_DIM), f32),
        "sb_out_gain": 1.0 + 0.02 * nrm(ks[8], (DEPTH, N_HEADS_SB, HEAD_DIM), f32),
        "w_out": nrm(ks[9], (DEPTH, D_MIX, D_MODEL), f32) * D_MIX ** -0.5,
        "norm2_g": 1.0 + 0.02 * nrm(ks[10], (DEPTH, D_MODEL), f32),
        "w_coarse": nrm(ks[11], (DEPTH, D_MODEL, N_GROUPS), f32) * D_MODEL ** -0.5,
        "b_coarse": 0.01 * nrm(ks[12], (DEPTH, N_GROUPS), f32),
        "w_fine": nrm(ks[13], (DEPTH, D_MODEL, N_EXPERTS), f32) * D_MODEL ** -0.5,
        "b_fine": 0.01 * nrm(ks[14], (DEPTH, N_EXPERTS), f32),
        "w1": nrm(ks[15], (DEPTH, N_EXPERTS, D_MODEL, D_EXPERT), f32) * D_MODEL ** -0.5,
        "w3": nrm(ks[16], (DEPTH, N_EXPERTS, D_MODEL, D_EXPERT), f32) * D_MODEL ** -0.5,
        "w2": nrm(ks[17], (DEPTH, N_EXPERTS, D_EXPERT, D_MODEL), f32) * D_EXPERT ** -0.5,
    }


def reference(x, meta_tokens, norm1_g, w_in, b_f, q_gain, k_gain, fox_out_gain, sb_out_gain,
              w_out, norm2_g, w_coarse, b_coarse, w_fine, b_fine, w1, w3, w2):
    bsz = x.shape[0]
    meta = jnp.broadcast_to(meta_tokens.astype(x.dtype)[None], (bsz, N_META, D_MODEL))
    h = jnp.concatenate([meta, x], axis=1)
    for layer in range(DEPTH):
        h = h + parallel_mixer(rmsnorm(h, norm1_g[layer]), w_in[layer], b_f[layer],
                               q_gain[layer], k_gain[layer], fox_out_gain[layer],
                               sb_out_gain[layer], w_out[layer])
        h = h + hier_moe(rmsnorm(h, norm2_g[layer]), w_coarse[layer], b_coarse[layer],
                         w_fine[layer], b_fine[layer], w1[layer], w3[layer], w2[layer])
    return h[:, N_META:]
```

```python
import functools

import jax
import jax.numpy as jnp
from jax import lax
from jax.experimental import pallas as pl
from jax.experimental.pallas import tpu as pltpu

F32 = jnp.float32
BF16 = jnp.bfloat16

D_MODEL = 2048
N_META = 16
HEAD_DIM = 128
N_HEADS = 8
D_GROUP = N_HEADS * HEAD_DIM
N_PROJ_GROUPS = 7
D_PROJ = N_PROJ_GROUPS * D_GROUP
N_GROUPS = 8
EXPERTS_PER_GROUP = 8
N_EXPERTS = 64
TOP_K = 2
D_EXPERT = 512
EPS = 1e-6
LANES = 128
BLK = 128
SKIP_LOG = 100.0
VMEM_LIMIT = 56 * 1024 * 1024


def _cparams(sem, vmem=VMEM_LIMIT):
    return pltpu.CompilerParams(dimension_semantics=sem, vmem_limit_bytes=vmem)


def _log_sigmoid(x):
    return jnp.minimum(x, 0.0) - jnp.log(1.0 + jnp.exp(-jnp.abs(x)))


def _split3(x):
    hi = x.astype(BF16)
    r1 = x - hi.astype(F32)
    mid = r1.astype(BF16)
    lo = (r1 - mid.astype(F32)).astype(BF16)
    return hi, mid, lo


def _cast_kernel(x_ref, o_ref):
    o_ref[...] = x_ref[...].astype(o_ref.dtype)


def _cast_bf16(w, n_cols, tr=256, tc=1024):
    rows = w.shape[0]
    return pl.pallas_call(
        _cast_kernel,
        grid=(rows // tr, n_cols // tc),
        in_specs=[pl.BlockSpec((tr, tc), lambda i, j: (i, j))],
        out_specs=pl.BlockSpec((tr, tc), lambda i, j: (i, j)),
        out_shape=jax.ShapeDtypeStruct((rows, n_cols), BF16),
        compiler_params=_cparams(("parallel", "parallel")),
        name="cast_bf16",
    )(w)


def _inproj_kernel(x_ref, g_ref, w_ref, wf_ref, bf_ref, gain_ref, o_ref, lf_ref, u_ref, *, heads_per_tile):
    j = pl.program_id(1)

    @pl.when(j == 0)
    def _():
        x = x_ref[...]
        ms = jnp.mean(x * x, axis=-1, keepdims=True)
        u = (x * lax.rsqrt(ms + EPS) * g_ref[...]).astype(BF16)
        u_ref[...] = u
        f = jnp.dot(u, wf_ref[...], preferred_element_type=F32) + bf_ref[...]
        lf_ref[...] = _log_sigmoid(f)

    acc = jnp.dot(u_ref[...], w_ref[...], preferred_element_type=F32)
    gain = gain_ref[0]
    tiles_per_group = N_HEADS // heads_per_tile
    group = j // tiles_per_group

    @pl.when(group < 2)
    def _():
        for h in range(heads_per_tile):
            y = acc[:, h * HEAD_DIM:(h + 1) * HEAD_DIM]
            ms = jnp.mean(y * y, axis=-1, keepdims=True)
            o_ref[h] = (y * lax.rsqrt(ms + EPS) * gain).astype(BF16)

    @pl.when(group >= 2)
    def _():
        for h in range(heads_per_tile):
            o_ref[h] = (acc[:, h * HEAD_DIM:(h + 1) * HEAD_DIM] * gain).astype(BF16)


def _inproj(x2d, norm_g, w_bf, wf_bf, bf_pad, gains, tm, tn=512):
    rows = x2d.shape[0]
    hpt = tn // HEAD_DIM
    kern = functools.partial(_inproj_kernel, heads_per_tile=hpt)
    return pl.pallas_call(
        kern,
        grid=(rows // tm, D_PROJ // tn),
        in_specs=[
            pl.BlockSpec((tm, D_MODEL), lambda i, j: (i, 0)),
            pl.BlockSpec((1, D_MODEL), lambda i, j: (0, 0)),
            pl.BlockSpec((D_MODEL, tn), lambda i, j: (0, j)),
            pl.BlockSpec((D_MODEL, LANES), lambda i, j: (0, 0)),
            pl.BlockSpec((1, LANES), lambda i, j: (0, 0)),
            pl.BlockSpec((1, 1, HEAD_DIM), lambda i, j: (j // (N_HEADS // hpt), 0, 0)),
        ],
        out_specs=[
            pl.BlockSpec((hpt, tm, HEAD_DIM), lambda i, j: (j, i, 0)),
            pl.BlockSpec((tm, LANES), lambda i, j: (i, 0)),
        ],
        out_shape=[
            jax.ShapeDtypeStruct((D_PROJ // HEAD_DIM, rows, HEAD_DIM), BF16),
            jax.ShapeDtypeStruct((rows, LANES), F32),
        ],
        scratch_shapes=[pltpu.VMEM((tm, D_MODEL), BF16)],
        compiler_params=_cparams(("parallel", "arbitrary")),
        name="inproj",
    )(x2d, norm_g, w_bf, wf_bf, bf_pad, gains)


def _cumgate_kernel(lfm_ref, lf_ref, qx_ref, kx_ref, kxm_ref, *, n_blk):
    row = lax.broadcasted_iota(jnp.int32, (BLK, BLK), 0)
    col = lax.broadcasted_iota(jnp.int32, (BLK, BLK), 1)
    tri = (col <= row).astype(BF16)
    one = jnp.ones((BLK, LANES), F32)
    zero = jnp.zeros((BLK, LANES), F32)

    def prefix(x):
        hi, mid, lo = _split3(x)
        return (jnp.dot(tri, hi, preferred_element_type=F32)
                + jnp.dot(tri, mid, preferred_element_type=F32)
                + jnp.dot(tri, lo, preferred_element_type=F32))

    def ext(cum, h):
        c = jnp.broadcast_to(cum[:, h:h + 1], (BLK, LANES))
        hi, mid, lo = (t.astype(F32) for t in _split3(c))
        qx = jnp.where(col == 0, hi, jnp.where(col == 1, mid, jnp.where(col == 2, lo,
                       jnp.where(col < 6, one, zero))))
        kx = jnp.where(col < 3, one, jnp.where(col == 3, -hi, jnp.where(col == 4, -mid,
                       jnp.where(col == 5, -lo, zero))))
        return qx.astype(BF16), kx.astype(BF16)

    lfm = jnp.where(row < N_META, lfm_ref[...], 0.0)
    cum_m = prefix(lfm)
    for h in range(N_HEADS):
        _, kx = ext(cum_m, h)
        kxm_ref[h] = kx
    carry = cum_m[BLK - 1:BLK, :]
    for b in range(n_blk):
        cum = prefix(lf_ref[b * BLK:(b + 1) * BLK, :]) + carry
        carry = cum[BLK - 1:BLK, :]
        for h in range(N_HEADS):
            qx, kx = ext(cum, h)
            qx_ref[h, b * BLK:(b + 1) * BLK, :] = qx
            kx_ref[h, b * BLK:(b + 1) * BLK, :] = kx


def _cumgate(lf_meta, lf_real, n_batch, seq):
    kern = functools.partial(_cumgate_kernel, n_blk=seq // BLK)
    return pl.pallas_call(
        kern,
        grid=(n_batch,),
        in_specs=[
            pl.BlockSpec((BLK, LANES), lambda b: (0, 0)),
            pl.BlockSpec((seq, LANES), lambda b: (b, 0)),
        ],
        out_specs=[
            pl.BlockSpec((N_HEADS, seq, LANES), lambda b: (0, b, 0)),
            pl.BlockSpec((N_HEADS, seq, LANES), lambda b: (0, b, 0)),
            pl.BlockSpec((N_HEADS, BLK, LANES), lambda b: (0, 0, 0)),
        ],
        out_shape=[
            jax.ShapeDtypeStruct((N_HEADS, n_batch * seq, LANES), BF16),
            jax.ShapeDtypeStruct((N_HEADS, n_batch * seq, LANES), BF16),
            jax.ShapeDtypeStruct((N_HEADS, BLK, LANES), BF16),
        ],
        compiler_params=_cparams(("arbitrary",)),
        name="cumgate",
    )(lf_meta, lf_real)


def _dot_nt(a, b):
    return lax.dot_general(a, b, (((1,), (1,)), ((), ())), preferred_element_type=F32)


def _fox_kernel(bound_ref, q_ref, qx_ref, k_ref, kx_ref, v_ref, g_ref, km_ref, kxm_ref, vm_ref, gain_ref,
                o_ref, m_scr, l_scr, acc_scr):
    i = pl.program_id(1)
    row = lax.broadcasted_iota(jnp.int32, (BLK, BLK), 0)
    col = lax.broadcasted_iota(jnp.int32, (BLK, BLK), 1)

    def sweep(kb, kxb, vb, mask, first):
        worst = None
        for h in range(N_HEADS):
            qa = jnp.concatenate([q_ref[h], qx_ref[h]], axis=1)
            ka = jnp.concatenate([kb(h), kxb(h)], axis=1)
            s = _dot_nt(qa, ka)
            if mask is not None:
                s = jnp.where(mask, s, -jnp.inf)
            m_cur = jnp.max(s, axis=1, keepdims=True)
            if first:
                m_new = jnp.broadcast_to(m_cur, (BLK, LANES))
                p = jnp.exp(s - m_new)
                l_scr[h] = jnp.broadcast_to(jnp.sum(p, axis=1, keepdims=True), (BLK, LANES))
                acc_scr[h] = jnp.dot(p.astype(BF16), vb(h), preferred_element_type=F32)
            else:
                m_prev = m_scr[h]
                m_new = jnp.maximum(m_prev, m_cur)
                alpha = jnp.exp(m_prev - m_new)
                p = jnp.exp(s - m_new)
                l_scr[h] = alpha * l_scr[h] + jnp.sum(p, axis=1, keepdims=True)
                acc_scr[h] = alpha * acc_scr[h] + jnp.dot(p.astype(BF16), vb(h), preferred_element_type=F32)
            m_scr[h] = m_new
            gap = s[:, 0:1] - m_new[:, 0:1]
            worst = gap if worst is None else jnp.maximum(worst, gap)
        return jnp.max(worst)

    def real_block(j):
        start = pl.multiple_of(j * BLK, BLK)
        return (lambda h: k_ref[h, pl.ds(start, BLK), :],
                lambda h: kx_ref[h, pl.ds(start, BLK), :],
                lambda h: v_ref[h, pl.ds(start, BLK), :])

    bound = bound_ref[0]
    kb, kxb, vb = real_block(i)
    gap0 = sweep(kb, kxb, vb, col <= row, True)

    def cond(c):
        j, done = c
        return jnp.logical_and(j >= 0, done == 0)

    def body(c):
        j, _ = c
        kb, kxb, vb = real_block(j)
        gap = sweep(kb, kxb, vb, None, False)
        return j - 1, (gap + bound < -SKIP_LOG).astype(jnp.int32)

    _, done = lax.while_loop(cond, body, (i - 1, (gap0 + bound < -SKIP_LOG).astype(jnp.int32)))

    @pl.when(done == 0)
    def _():
        sweep(lambda h: km_ref[h], lambda h: kxm_ref[h], lambda h: vm_ref[h], col < N_META, False)

    for h in range(N_HEADS):
        o = acc_scr[h] / l_scr[h]
        ms = jnp.mean(o * o, axis=-1, keepdims=True)
        gate = 1.0 / (1.0 + jnp.exp(-g_ref[h].astype(F32)))
        o_ref[h] = (o * lax.rsqrt(ms + EPS) * gain_ref[h] * gate).astype(BF16)


def _sb_kernel(q_ref, k_ref, v_ref, km_ref, vm_ref, gain_ref, o_ref, carry_scr, acc_scr):
    i = pl.program_id(1)
    row = lax.broadcasted_iota(jnp.int32, (BLK, BLK), 0)
    col = lax.broadcasted_iota(jnp.int32, (BLK, BLK), 1)
    r2 = lax.broadcasted_iota(jnp.int32, (BLK, 2 * BLK), 0)
    c2 = lax.broadcasted_iota(jnp.int32, (BLK, 2 * BLK), 1)
    suffix = jnp.logical_or(c2 >= BLK, r2 > c2).astype(BF16)

    def sweep(kb, vb, mask, first):
        worst = None
        for h in range(N_HEADS):
            z = _dot_nt(q_ref[h], kb(h))
            sp = jnp.maximum(z, 0.0) + jnp.log(1.0 + jnp.exp(-jnp.abs(z)))
            lk = -sp
            if mask is not None:
                lk = jnp.where(mask, lk, 0.0)
            hi = lk.astype(BF16)
            lo = (lk - hi.astype(F32)).astype(BF16)
            t = (jnp.dot(hi, suffix, preferred_element_type=F32)
                 + jnp.dot(lo, suffix, preferred_element_type=F32))
            later = t[:, :BLK]
            rowsum = t[:, BLK:]
            if not first:
                later = later + carry_scr[h]
            a = jnp.exp(z - sp + later)
            if mask is not None:
                a = jnp.where(mask, a, 0.0)
            pv = jnp.dot(a.astype(BF16), vb(h), preferred_element_type=F32)
            if first:
                acc_scr[h] = pv
                c_new = rowsum
            else:
                acc_scr[h] = acc_scr[h] + pv
                c_new = carry_scr[h] + rowsum
            carry_scr[h] = c_new
            worst = c_new[:, 0:1] if worst is None else jnp.maximum(worst, c_new[:, 0:1])
        return jnp.max(worst)

    def real_block(j):
        start = pl.multiple_of(j * BLK, BLK)
        return (lambda h: k_ref[h, pl.ds(start, BLK), :],
                lambda h: v_ref[h, pl.ds(start, BLK), :])

    kb, vb = real_block(i)
    top0 = sweep(kb, vb, col < row, True)

    def cond(c):
        j, done = c
        return jnp.logical_and(j >= 0, done == 0)

    def body(c):
        j, _ = c
        kb, vb = real_block(j)
        top = sweep(kb, vb, None, False)
        return j - 1, (top < -SKIP_LOG).astype(jnp.int32)

    _, done = lax.while_loop(cond, body, (i - 1, (top0 < -SKIP_LOG).astype(jnp.int32)))

    @pl.when(done == 0)
    def _():
        sweep(lambda h: km_ref[h], lambda h: vm_ref[h], col < N_META, False)

    for h in range(N_HEADS):
        o = acc_scr[h]
        ms = jnp.mean(o * o, axis=-1, keepdims=True)
        o_ref[h] = (o * lax.rsqrt(ms + EPS) * gain_ref[h]).astype(BF16)


def _head_spec_q(group, nq):
    return pl.BlockSpec((N_HEADS, BLK, HEAD_DIM), lambda b, i: (group, b * nq + i, 0))


def _head_spec_kv(group, seq):
    return pl.BlockSpec((N_HEADS, seq, HEAD_DIM), lambda b, i: (group, b, 0))


def _head_spec_meta(group):
    return pl.BlockSpec((N_HEADS, BLK, HEAD_DIM), lambda b, i: (group, 0, 0))


def _fox_attention(bound, proj, proj_m, qx, kx, kxm, out_gain, n_batch, seq):
    nq = seq // BLK
    rows = n_batch * seq
    return pl.pallas_call(
        _fox_kernel,
        grid=(n_batch, nq),
        in_specs=[
            pl.BlockSpec(memory_space=pltpu.SMEM),
            _head_spec_q(0, nq),
            _head_spec_q(0, nq),
            _head_spec_kv(1, seq),
            _head_spec_kv(0, seq),
            _head_spec_kv(2, seq),
            _head_spec_q(3, nq),
            _head_spec_meta(1),
            _head_spec_meta(0),
            _head_spec_meta(2),
            pl.BlockSpec((N_HEADS, 1, HEAD_DIM), lambda b, i: (0, 0, 0)),
        ],
        out_specs=pl.BlockSpec((N_HEADS, BLK, HEAD_DIM), lambda b, i: (0, b * nq + i, 0)),
        out_shape=jax.ShapeDtypeStruct((N_HEADS, rows, HEAD_DIM), BF16),
        scratch_shapes=[pltpu.VMEM((N_HEADS, BLK, LANES), F32)] * 3,
        compiler_params=_cparams(("parallel", "arbitrary")),
        name="fox_attention",
    )(bound, proj, qx, proj, kx, proj, proj, proj_m, kxm, proj_m, out_gain)


def _sb_attention(proj, proj_m, out_gain, n_batch, seq):
    nq = seq // BLK
    rows = n_batch * seq
    return pl.pallas_call(
        _sb_kernel,
        grid=(n_batch, nq),
        in_specs=[
            _head_spec_q(4, nq),
            _head_spec_kv(5, seq),
            _head_spec_kv(6, seq),
            _head_spec_meta(5),
            _head_spec_meta(6),
            pl.BlockSpec((N_HEADS, 1, HEAD_DIM), lambda b, i: (0, 0, 0)),
        ],
        out_specs=pl.BlockSpec((N_HEADS, BLK, HEAD_DIM), lambda b, i: (0, b * nq + i, 0)),
        out_shape=jax.ShapeDtypeStruct((N_HEADS, rows, HEAD_DIM), BF16),
        scratch_shapes=[pltpu.VMEM((N_HEADS, BLK, LANES), F32)] * 2,
        compiler_params=_cparams(("parallel", "arbitrary")),
        name="sb_attention",
    )(proj, proj, proj, proj_m, proj_m, out_gain)


def _outproj_kernel(oa_ref, ob_ref, w_ref, x_ref, o_ref):
    lhs = jnp.concatenate([oa_ref[h] for h in range(N_HEADS)] + [ob_ref[h] for h in range(N_HEADS)], axis=1)
    o_ref[...] = x_ref[...] + jnp.dot(lhs, w_ref[...], preferred_element_type=F32)


def _outproj(oa, ob, w_bf, x2d, tm=256):
    rows = x2d.shape[0]
    return pl.pallas_call(
        _outproj_kernel,
        grid=(rows // tm,),
        in_specs=[
            pl.BlockSpec((N_HEADS, tm, HEAD_DIM), lambda i: (0, i, 0)),
            pl.BlockSpec((N_HEADS, tm, HEAD_DIM), lambda i: (0, i, 0)),
            pl.BlockSpec((2 * D_GROUP, D_MODEL), lambda i: (0, 0)),
            pl.BlockSpec((tm, D_MODEL), lambda i: (i, 0)),
        ],
        out_specs=pl.BlockSpec((tm, D_MODEL), lambda i: (i, 0)),
        out_shape=jax.ShapeDtypeStruct((rows, D_MODEL), F32),
        compiler_params=_cparams(("parallel",)),
        name="outproj",
    )(oa, ob, w_bf, x2d)


def _router_kernel(h_ref, g_ref, whi_ref, wlo_ref, b_ref, u_ref, eid_ref, w0_ref, w1_ref):
    x = h_ref[...]
    tm = x.shape[0]
    ms = jnp.mean(x * x, axis=-1, keepdims=True)
    u = x * lax.rsqrt(ms + EPS) * g_ref[...]
    u_ref[...] = u
    uhi = u.astype(BF16)
    ulo = (u - uhi.astype(F32)).astype(BF16)
    logits = (jnp.dot(uhi, whi_ref[...], preferred_element_type=F32)
              + jnp.dot(uhi, wlo_ref[...], preferred_element_type=F32)
              + jnp.dot(ulo, whi_ref[...], preferred_element_type=F32)) + b_ref[...]
    lane_i = lax.broadcasted_iota(jnp.int32, (tm, LANES), 1)
    lane = lane_i.astype(F32)
    big = float(4 * LANES)
    neg = -jnp.inf
    c = jnp.where(lane_i < N_GROUPS, logits, neg)
    cmax = jnp.max(c, axis=1, keepdims=True)
    g_sel = jnp.min(jnp.where(c == cmax, lane, big), axis=1, keepdims=True)
    g_gate = 1.0 / jnp.sum(jnp.exp(c - cmax), axis=1, keepdims=True)
    lo = N_GROUPS + g_sel * EXPERTS_PER_GROUP
    in_group = jnp.logical_and(lane >= lo, lane < lo + EXPERTS_PER_GROUP)
    f = jnp.where(in_group, logits, neg)
    t1 = jnp.max(f, axis=1, keepdims=True)
    i1 = jnp.min(jnp.where(f == t1, lane, big), axis=1, keepdims=True)
    f2 = jnp.where(lane == i1, neg, f)
    t2 = jnp.max(f2, axis=1, keepdims=True)
    i2 = jnp.min(jnp.where(f2 == t2, lane, big), axis=1, keepdims=True)
    d = jnp.exp(t2 - t1)
    w_first = g_gate / (1.0 + d)
    e1 = jnp.broadcast_to(i1 - N_GROUPS, (tm, LANES)).astype(jnp.int32)
    e2 = jnp.broadcast_to(i2 - N_GROUPS, (tm, LANES)).astype(jnp.int32)
    eid_ref[...] = jnp.where(lane_i == 0, e1, jnp.where(lane_i == 1, e2, 0))
    w0_ref[...] = jnp.broadcast_to(w_first, (tm, LANES))
    w1_ref[...] = jnp.broadcast_to(w_first * d, (tm, LANES))


def _router(h1, norm_g, whi, wlo, b_pad, tm=256):
    rows = h1.shape[0]
    return pl.pallas_call(
        _router_kernel,
        grid=(rows // tm,),
        in_specs=[
            pl.BlockSpec((tm, D_MODEL), lambda i: (i, 0)),
            pl.BlockSpec((1, D_MODEL), lambda i: (0, 0)),
            pl.BlockSpec((D_MODEL, LANES), lambda i: (0, 0)),
            pl.BlockSpec((D_MODEL, LANES), lambda i: (0, 0)),
            pl.BlockSpec((1, LANES), lambda i: (0, 0)),
        ],
        out_specs=[
            pl.BlockSpec((tm, D_MODEL), lambda i: (i, 0)),
            pl.BlockSpec((tm, LANES), lambda i: (i, 0)),
            pl.BlockSpec((tm, LANES), lambda i: (i, 0)),
            pl.BlockSpec((tm, LANES), lambda i: (i, 0)),
        ],
        out_shape=[
            jax.ShapeDtypeStruct((rows, D_MODEL), F32),
            jax.ShapeDtypeStruct((rows, LANES), jnp.int32),
            jax.ShapeDtypeStruct((rows, LANES), F32),
            jax.ShapeDtypeStruct((rows, LANES), F32),
        ],
        compiler_params=_cparams(("parallel",)),
        name="router",
    )(h1, norm_g, whi, wlo, b_pad)


GATHER_CHUNK = 128


def _gather_kernel(idx_ref, src_ref, dst_ref, sem, *, n_rows):
    n_chunks = n_rows // GATHER_CHUNK

    def row_copy(src_row, dst_row):
        return pltpu.make_async_copy(src_ref.at[pl.ds(src_row, 1)], dst_ref.at[pl.ds(dst_row, 1)], sem)

    def issue(c):
        def body(r, carry):
            dst_row = c * GATHER_CHUNK + r
            row_copy(idx_ref[dst_row], dst_row).start()
            return carry
        lax.fori_loop(0, GATHER_CHUNK, body, 0)

    def drain():
        def body(r, carry):
            row_copy(0, 0).wait()
            return carry
        lax.fori_loop(0, GATHER_CHUNK, body, 0)

    issue(0)

    def chunk_body(c, carry):
        issue(c)
        drain()
        return carry

    lax.fori_loop(1, n_chunks, chunk_body, 0)
    drain()


def _gather_rows(idx, src, n_rows):
    kern = functools.partial(_gather_kernel, n_rows=n_rows)
    return pl.pallas_call(
        kern,
        grid_spec=pltpu.PrefetchScalarGridSpec(
            num_scalar_prefetch=1,
            grid=(1,),
            in_specs=[pl.BlockSpec(memory_space=pl.ANY)],
            out_specs=pl.BlockSpec(memory_space=pl.ANY),
            scratch_shapes=[pltpu.SemaphoreType.DMA(())],
        ),
        out_shape=jax.ShapeDtypeStruct((n_rows, src.shape[1]), src.dtype),
        compiler_params=_cparams(("arbitrary",)),
        name="gather_rows",
    )(idx, src)


FFN_TM = 256


def _ffn_kernel(tile_ref, exp_ref, lo_ref, hi_ref, first_ref, x_ref, w1_ref, w3_ref, w2_ref, o_ref):
    it = pl.program_id(0)
    lo = lo_ref[it]
    hi = hi_ref[it]

    @pl.when(first_ref[it] == 1)
    def _():
        o_ref[...] = jnp.zeros_like(o_ref)

    @pl.when(hi > lo)
    def _():
        x = x_ref[...].astype(BF16)
        a = jnp.dot(x, w1_ref[...].astype(BF16), preferred_element_type=F32)
        b = jnp.dot(x, w3_ref[...].astype(BF16), preferred_element_type=F32)
        mid = a / (1.0 + jnp.exp(-a)) * b
        rows = tile_ref[it] * FFN_TM + lax.broadcasted_iota(jnp.int32, mid.shape, 0)
        mid = jnp.where(jnp.logical_and(rows >= lo, rows < hi), mid, 0.0).astype(BF16)
        o_ref[...] += jnp.dot(mid, w2_ref[...].astype(BF16), preferred_element_type=F32)


def _grouped_ffn(item_tile, item_exp, item_lo, item_hi, item_first, xs, w1, w3, w2):
    n_items = item_tile.shape[0]
    n_rows = xs.shape[0]
    return pl.pallas_call(
        _ffn_kernel,
        grid_spec=pltpu.PrefetchScalarGridSpec(
            num_scalar_prefetch=5,
            grid=(n_items,),
            in_specs=[
                pl.BlockSpec((FFN_TM, D_MODEL), lambda i, t, e, lo, hi, f: (t[i], 0)),
                pl.BlockSpec((None, D_MODEL, D_EXPERT), lambda i, t, e, lo, hi, f: (e[i], 0, 0)),
                pl.BlockSpec((None, D_MODEL, D_EXPERT), lambda i, t, e, lo, hi, f: (e[i], 0, 0)),
                pl.BlockSpec((None, D_EXPERT, D_MODEL), lambda i, t, e, lo, hi, f: (e[i], 0, 0)),
            ],
            out_specs=pl.BlockSpec((FFN_TM, D_MODEL), lambda i, t, e, lo, hi, f: (t[i], 0)),
        ),
        out_shape=jax.ShapeDtypeStruct((n_rows, D_MODEL), F32),
        compiler_params=_cparams(("arbitrary",)),
        name="grouped_ffn",
    )(item_tile, item_exp, item_lo, item_hi, item_first, xs, w1, w3, w2)


def _combine_kernel(h_ref, y0_ref, y1_ref, w0_ref, w1_ref, o_ref):
    reps = D_MODEL // LANES
    w0 = jnp.concatenate([w0_ref[...]] * reps, axis=1)
    w1 = jnp.concatenate([w1_ref[...]] * reps, axis=1)
    o_ref[...] = h_ref[...] + w0 * y0_ref[...] + w1 * y1_ref[...]


def _combine(h1, yg, w0b, w1b, tm=256):
    rows = h1.shape[0]
    nt = rows // tm
    return pl.pallas_call(
        _combine_kernel,
        grid=(nt,),
        in_specs=[
            pl.BlockSpec((tm, D_MODEL), lambda i: (i, 0)),
            pl.BlockSpec((tm, D_MODEL), lambda i: (i, 0)),
            pl.BlockSpec((tm, D_MODEL), lambda i: (i + nt, 0)),
            pl.BlockSpec((tm, LANES), lambda i: (i, 0)),
            pl.BlockSpec((tm, LANES), lambda i: (i, 0)),
        ],
        out_specs=pl.BlockSpec((tm, D_MODEL), lambda i: (i, 0)),
        out_shape=jax.ShapeDtypeStruct((rows, D_MODEL), F32),
        compiler_params=_cparams(("parallel",)),
        name="combine",
    )(h1, yg, yg, w0b, w1b)


def _dispatch_plan(eid):
    n_tok = eid.shape[0]
    n_assign = n_tok * TOP_K
    flat_e = eid.reshape(-1)
    onehot = (flat_e[:, None] == jnp.arange(N_EXPERTS, dtype=jnp.int32)[None, :]).astype(jnp.int32)
    csum = jnp.cumsum(onehot, axis=0)
    counts = csum[-1]
    ends = jnp.cumsum(counts)
    starts = ends - counts
    rank = jnp.sum((csum - onehot) * onehot, axis=1)
    pos = starts[flat_e] + rank
    sorted_tok = jnp.zeros((n_assign,), jnp.int32).at[pos].set(jnp.arange(n_assign, dtype=jnp.int32) // TOP_K)
    n_tiles = n_assign // FFN_TM
    tile_starts = jnp.arange(n_tiles, dtype=jnp.int32) * FFN_TM
    bps = jnp.sort(jnp.concatenate([tile_starts, starts.astype(jnp.int32)]))
    seg_lo = bps
    seg_hi = jnp.concatenate([bps[1:], jnp.array([n_assign], jnp.int32)])
    item_tile = jnp.minimum(seg_lo // FFN_TM, n_tiles - 1)
    item_exp = jnp.minimum(jnp.searchsorted(ends, seg_lo, side="right"), N_EXPERTS - 1).astype(jnp.int32)
    item_first = (seg_lo == item_tile * FFN_TM).astype(jnp.int32)
    return pos, sorted_tok, item_tile.astype(jnp.int32), item_exp, seg_lo, seg_hi, item_first


def kernel(x, meta_tokens, norm1_g, w_in, b_f, q_gain, k_gain, fox_out_gain, sb_out_gain, w_out, norm2_g,
           w_coarse, b_coarse, w_fine, b_fine, w1, w3, w2):
    assert norm1_g.shape[0] == 1, "single-layer block"
    n_batch, seq, _ = x.shape
    n_tok = n_batch * seq
    scale = HEAD_DIM ** -0.5
    x2d = x.reshape(n_tok, D_MODEL)

    w_in0 = w_in[0]
    w_in_bf = _cast_bf16(w_in0, D_PROJ)
    w_out_bf = _cast_bf16(w_out[0], D_MODEL)
    wf_bf = jnp.pad(w_in0[:, D_PROJ:], ((0, 0), (0, LANES - N_HEADS))).astype(BF16)
    bf_pad = jnp.pad(b_f[0], (0, LANES - N_HEADS)).reshape(1, LANES)
    ones = jnp.ones((HEAD_DIM,), F32)
    gains = jnp.stack([q_gain[0] * scale, k_gain[0], ones, ones, ones * scale, ones, ones]).reshape(
        N_PROJ_GROUPS, 1, HEAD_DIM)
    meta_pad = jnp.pad(meta_tokens.astype(F32), ((0, BLK - N_META), (0, 0)))
    n1 = norm1_g[0].reshape(1, D_MODEL)

    proj, lf = _inproj(x2d, n1, w_in_bf, wf_bf, bf_pad, gains, tm=1024)
    proj_m, lf_m = _inproj(meta_pad, n1, w_in_bf, wf_bf, bf_pad, gains, tm=BLK)
    qx, kx, kxm = _cumgate(lf_m, lf, n_batch, seq)
    qk_bound = 1.02 * HEAD_DIM * scale * jnp.max(jnp.abs(q_gain[0])) * jnp.max(jnp.abs(k_gain[0]))
    bound = (2.0 * qk_bound).reshape(1).astype(F32)
    oa = _fox_attention(bound, proj, proj_m, qx, kx, kxm, fox_out_gain[0].reshape(N_HEADS, 1, HEAD_DIM),
                        n_batch, seq)
    ob = _sb_attention(proj, proj_m, sb_out_gain[0].reshape(N_HEADS, 1, HEAD_DIM), n_batch, seq)
    h1 = _outproj(oa, ob, w_out_bf, x2d)

    wr = jnp.pad(jnp.concatenate([w_coarse[0], w_fine[0]], axis=1),
                 ((0, 0), (0, LANES - N_GROUPS - N_EXPERTS)))
    wr_hi = wr.astype(BF16)
    wr_lo = (wr - wr_hi.astype(F32)).astype(BF16)
    br = jnp.pad(jnp.concatenate([b_coarse[0], b_fine[0]]), (0, LANES - N_GROUPS - N_EXPERTS)).reshape(1, LANES)
    u2, eid, w0b, w1b = _router(h1, norm2_g[0].reshape(1, D_MODEL), wr_hi, wr_lo, br)
    pos, sorted_tok, item_tile, item_exp, item_lo, item_hi, item_first = _dispatch_plan(eid[:, :TOP_K])
    xs = _gather_rows(sorted_tok, u2, n_tok * TOP_K)
    ys = _grouped_ffn(item_tile, item_exp, item_lo, item_hi, item_first, xs, w1[0], w3[0], w2[0])
    back = pos.reshape(n_tok, TOP_K).T.reshape(-1)
    yg = _gather_rows(back, ys, n_tok * TOP_K)
    out = _combine(h1, yg, w0b, w1b)
    return out.reshape(n_batch, seq, D_MODEL)
```

```python
import functools

import jax
import jax.numpy as jnp
from jax import lax
from jax.experimental import pallas as pl
from jax.experimental.pallas import tpu as pltpu

F32 = jnp.float32
BF16 = jnp.bfloat16

D_MODEL = 2048
N_META = 16
HEAD_DIM = 128
N_HEADS = 8
D_GROUP = N_HEADS * HEAD_DIM
N_PROJ_GROUPS = 7
D_PROJ = N_PROJ_GROUPS * D_GROUP
N_GROUPS = 8
EXPERTS_PER_GROUP = 8
N_EXPERTS = 64
TOP_K = 2
D_EXPERT = 512
EPS = 1e-6
LANES = 128
N_CHUNK = D_MODEL // LANES
BLK = 128
SKIP_LOG = 88.0
VMEM_LIMIT = 56 * 1024 * 1024


def _cparams(sem, vmem=VMEM_LIMIT):
    return pltpu.CompilerParams(dimension_semantics=sem, vmem_limit_bytes=vmem)


def _log_sigmoid(x):
    return jnp.minimum(x, 0.0) - jnp.log(1.0 + jnp.exp(-jnp.abs(x)))


def _split3(x):
    hi = x.astype(BF16)
    r1 = x - hi.astype(F32)
    mid = r1.astype(BF16)
    lo = (r1 - mid.astype(F32)).astype(BF16)
    return hi, mid, lo


def _dot_nt(a, b):
    return lax.dot_general(a, b, (((1,), (1,)), ((), ())), preferred_element_type=F32)


def _cast_kernel(x_ref, o_ref):
    o_ref[...] = x_ref[...].astype(o_ref.dtype)


def _cast_bf16(w, n_rows, tr=512):
    n_cols = w.shape[1]
    return pl.pallas_call(
        _cast_kernel,
        grid=(n_rows // tr,),
        in_specs=[pl.BlockSpec((tr, n_cols), lambda i: (i, 0))],
        out_specs=pl.BlockSpec((tr, n_cols), lambda i: (i, 0)),
        out_shape=jax.ShapeDtypeStruct((n_rows, n_cols), BF16),
        compiler_params=_cparams(("parallel",)),
        name="cast_bf16",
    )(w)


def _inproj_kernel(x_ref, g_ref, wt_ref, wft_ref, bf_ref, gain_ref, o_ref, lf_ref, u_ref, *, heads_per_tile):
    j = pl.program_id(1)

    @pl.when(j == 0)
    def _():
        x = x_ref[...]
        ms = jnp.mean(x * x, axis=-1, keepdims=True)
        u = (x * lax.rsqrt(ms + EPS) * g_ref[...]).astype(BF16)
        u_ref[...] = u
        f = _dot_nt(u, wft_ref[...]) + bf_ref[...]
        lf_ref[...] = _log_sigmoid(f)

    acc = _dot_nt(u_ref[...], wt_ref[...])
    gain = gain_ref[0]
    tiles_per_group = N_HEADS // heads_per_tile
    group = j // tiles_per_group

    @pl.when(group < 2)
    def _():
        for h in range(heads_per_tile):
            y = acc[:, h * HEAD_DIM:(h + 1) * HEAD_DIM]
            ms = jnp.mean(y * y, axis=-1, keepdims=True)
            o_ref[h] = (y * lax.rsqrt(ms + EPS) * gain).astype(BF16)

    @pl.when(group >= 2)
    def _():
        for h in range(heads_per_tile):
            o_ref[h] = (acc[:, h * HEAD_DIM:(h + 1) * HEAD_DIM] * gain).astype(BF16)


def _inproj(x2d, norm_g, wt_bf, wft_bf, bf_pad, gains, tm, tn=512):
    rows = x2d.shape[0]
    hpt = tn // HEAD_DIM
    kern = functools.partial(_inproj_kernel, heads_per_tile=hpt)
    return pl.pallas_call(
        kern,
        grid=(rows // tm, D_PROJ // tn),
        in_specs=[
            pl.BlockSpec((tm, D_MODEL), lambda i, j: (i, 0)),
            pl.BlockSpec((1, D_MODEL), lambda i, j: (0, 0)),
            pl.BlockSpec((tn, D_MODEL), lambda i, j: (j, 0)),
            pl.BlockSpec((LANES, D_MODEL), lambda i, j: (0, 0)),
            pl.BlockSpec((1, LANES), lambda i, j: (0, 0)),
            pl.BlockSpec((1, 1, HEAD_DIM), lambda i, j: (j // (N_HEADS // hpt), 0, 0)),
        ],
        out_specs=[
            pl.BlockSpec((hpt, tm, HEAD_DIM), lambda i, j: (j, i, 0)),
            pl.BlockSpec((tm, LANES), lambda i, j: (i, 0)),
        ],
        out_shape=[
            jax.ShapeDtypeStruct((D_PROJ // HEAD_DIM, rows, HEAD_DIM), BF16),
            jax.ShapeDtypeStruct((rows, LANES), F32),
        ],
        scratch_shapes=[pltpu.VMEM((tm, D_MODEL), BF16)],
        compiler_params=_cparams(("parallel", "arbitrary")),
        name="inproj",
    )(x2d, norm_g, wt_bf, wft_bf, bf_pad, gains)


def _cumgate_kernel(lfm_ref, lf_ref, qx_ref, kx_ref, kxm_ref, *, n_blk):
    row = lax.broadcasted_iota(jnp.int32, (BLK, BLK), 0)
    col = lax.broadcasted_iota(jnp.int32, (BLK, BLK), 1)
    tri = (col <= row).astype(BF16)
    one = jnp.ones((BLK, LANES), F32)
    zero = jnp.zeros((BLK, LANES), F32)

    def prefix(x):
        hi, mid, lo = _split3(x)
        return (jnp.dot(tri, hi, preferred_element_type=F32)
                + jnp.dot(tri, mid, preferred_element_type=F32)
                + jnp.dot(tri, lo, preferred_element_type=F32))

    def ext(cum, h):
        c = jnp.broadcast_to(cum[:, h:h + 1], (BLK, LANES))
        hi, mid, lo = (t.astype(F32) for t in _split3(c))
        qx = jnp.where(col == 0, hi, jnp.where(col == 1, mid, jnp.where(col == 2, lo,
                       jnp.where(col < 6, one, zero))))
        kx = jnp.where(col < 3, one, jnp.where(col == 3, -hi, jnp.where(col == 4, -mid,
                       jnp.where(col == 5, -lo, zero))))
        return qx.astype(BF16), kx.astype(BF16)

    lfm = jnp.where(row < N_META, lfm_ref[...], 0.0)
    cum_m = prefix(lfm)
    for h in range(N_HEADS):
        _, kx = ext(cum_m, h)
        kxm_ref[h] = kx
    carry = cum_m[BLK - 1:BLK, :]
    for b in range(n_blk):
        cum = prefix(lf_ref[b * BLK:(b + 1) * BLK, :]) + carry
        carry = cum[BLK - 1:BLK, :]
        for h in range(N_HEADS):
            qx, kx = ext(cum, h)
            qx_ref[h, b * BLK:(b + 1) * BLK, :] = qx
            kx_ref[h, b * BLK:(b + 1) * BLK, :] = kx


def _cumgate(lf_meta, lf_real, n_batch, seq):
    kern = functools.partial(_cumgate_kernel, n_blk=seq // BLK)
    return pl.pallas_call(
        kern,
        grid=(n_batch,),
        in_specs=[
            pl.BlockSpec((BLK, LANES), lambda b: (0, 0)),
            pl.BlockSpec((seq, LANES), lambda b: (b, 0)),
        ],
        out_specs=[
            pl.BlockSpec((N_HEADS, seq, LANES), lambda b: (0, b, 0)),
            pl.BlockSpec((N_HEADS, seq, LANES), lambda b: (0, b, 0)),
            pl.BlockSpec((N_HEADS, BLK, LANES), lambda b: (0, 0, 0)),
        ],
        out_shape=[
            jax.ShapeDtypeStruct((N_HEADS, n_batch * seq, LANES), BF16),
            jax.ShapeDtypeStruct((N_HEADS, n_batch * seq, LANES), BF16),
            jax.ShapeDtypeStruct((N_HEADS, BLK, LANES), BF16),
        ],
        compiler_params=_cparams(("arbitrary",)),
        name="cumgate",
    )(lf_meta, lf_real)


def _bdot_nt(a, b):
    return lax.dot_general(a, b, (((2,), (2,)), ((0,), (0,))), preferred_element_type=F32)


def _bdot_nn(a, b):
    return lax.dot_general(a, b, (((2,), (1,)), ((0,), (0,))), preferred_element_type=F32)


def _fox_kernel(bound_ref, q_ref, qx_ref, k_ref, kx_ref, v_ref, g_ref, km_ref, kxm_ref, vm_ref, gain_ref,
                o_ref, m_scr, l_scr, acc_scr):
    i = pl.program_id(1)
    row = lax.broadcasted_iota(jnp.int32, (N_HEADS, BLK, BLK), 1)
    col = lax.broadcasted_iota(jnp.int32, (N_HEADS, BLK, BLK), 2)

    def sweep(kb, kxb, vb, mask, first):
        qa = jnp.concatenate([q_ref[...], qx_ref[...]], axis=2)
        ka = jnp.concatenate([kb, kxb], axis=2)
        s = _bdot_nt(qa, ka)
        if mask is not None:
            s = jnp.where(mask, s, -jnp.inf)
        m_cur = jnp.max(s, axis=2, keepdims=True)
        if first:
            m_new = jnp.broadcast_to(m_cur, s.shape)
            p = jnp.exp(s - m_new)
            l_scr[...] = jnp.broadcast_to(jnp.sum(p, axis=2, keepdims=True), s.shape)
            acc_scr[...] = _bdot_nn(p.astype(BF16), vb)
        else:
            m_prev = m_scr[...]
            m_new = jnp.maximum(m_prev, m_cur)
            alpha = jnp.exp(m_prev - m_new)
            p = jnp.exp(s - m_new)
            l_scr[...] = alpha * l_scr[...] + jnp.sum(p, axis=2, keepdims=True)
            acc_scr[...] = alpha * acc_scr[...] + _bdot_nn(p.astype(BF16), vb)
        m_scr[...] = m_new
        return jnp.max(s[:, :, 0:1] - m_new[:, :, 0:1])

    def real_block(j):
        start = pl.multiple_of(j * BLK, BLK)
        return (k_ref[:, pl.ds(start, BLK), :], kx_ref[:, pl.ds(start, BLK), :], v_ref[:, pl.ds(start, BLK), :])

    bound = bound_ref[0]
    gap0 = sweep(*real_block(i), col <= row, True)

    def cond(c):
        j, done = c
        return jnp.logical_and(j >= 0, done == 0)

    def body(c):
        j, _ = c
        gap = sweep(*real_block(j), None, False)
        return j - 1, (gap + bound < -SKIP_LOG).astype(jnp.int32)

    _, done = lax.while_loop(cond, body, (i - 1, (gap0 + bound < -SKIP_LOG).astype(jnp.int32)))

    @pl.when(done == 0)
    def _():
        sweep(km_ref[...], kxm_ref[...], vm_ref[...], col < N_META, False)

    o = acc_scr[...] / l_scr[...]
    ms = jnp.mean(o * o, axis=-1, keepdims=True)
    gate = 1.0 / (1.0 + jnp.exp(-g_ref[...].astype(F32)))
    o_ref[...] = (o * lax.rsqrt(ms + EPS) * gain_ref[...] * gate).astype(BF16)


def _sb_kernel(q_ref, k_ref, v_ref, km_ref, vm_ref, gain_ref, o_ref, carry_scr, acc_scr):
    i = pl.program_id(1)
    row = lax.broadcasted_iota(jnp.int32, (N_HEADS, BLK, BLK), 1)
    col = lax.broadcasted_iota(jnp.int32, (N_HEADS, BLK, BLK), 2)
    r2 = lax.broadcasted_iota(jnp.int32, (BLK, 2 * BLK), 0)
    c2 = lax.broadcasted_iota(jnp.int32, (BLK, 2 * BLK), 1)
    suffix = jnp.logical_or(c2 >= BLK, r2 > c2).astype(BF16)

    def sweep(kb, vb, mask, first):
        z = _bdot_nt(q_ref[...], kb)
        sp = jnp.maximum(z, 0.0) + jnp.log(1.0 + jnp.exp(-jnp.abs(z)))
        lk = -sp
        if mask is not None:
            lk = jnp.where(mask, lk, 0.0)
        hi = lk.astype(BF16)
        lo = (lk - hi.astype(F32)).astype(BF16)
        t = (jnp.dot(hi.reshape(N_HEADS * BLK, BLK), suffix, preferred_element_type=F32)
             + jnp.dot(lo.reshape(N_HEADS * BLK, BLK), suffix, preferred_element_type=F32))
        t = t.reshape(N_HEADS, BLK, 2 * BLK)
        later = t[:, :, :BLK]
        rowsum = t[:, :, BLK:]
        if not first:
            later = later + carry_scr[...]
        a = jnp.exp(z - sp + later)
        if mask is not None:
            a = jnp.where(mask, a, 0.0)
        pv = _bdot_nn(a.astype(BF16), vb)
        if first:
            acc_scr[...] = pv
            c_new = rowsum
        else:
            acc_scr[...] = acc_scr[...] + pv
            c_new = carry_scr[...] + rowsum
        carry_scr[...] = c_new
        return jnp.max(c_new[:, :, 0:1])

    def real_block(j):
        start = pl.multiple_of(j * BLK, BLK)
        return k_ref[:, pl.ds(start, BLK), :], v_ref[:, pl.ds(start, BLK), :]

    top0 = sweep(*real_block(i), col < row, True)

    def cond(c):
        j, done = c
        return jnp.logical_and(j >= 0, done == 0)

    def body(c):
        j, _ = c
        top = sweep(*real_block(j), None, False)
        return j - 1, (top < -SKIP_LOG).astype(jnp.int32)

    _, done = lax.while_loop(cond, body, (i - 1, (top0 < -SKIP_LOG).astype(jnp.int32)))

    @pl.when(done == 0)
    def _():
        sweep(km_ref[...], vm_ref[...], col < N_META, False)

    o = acc_scr[...]
    ms = jnp.mean(o * o, axis=-1, keepdims=True)
    o_ref[...] = (o * lax.rsqrt(ms + EPS) * gain_ref[...]).astype(BF16)


def _head_spec_q(group, nq):
    return pl.BlockSpec((N_HEADS, BLK, HEAD_DIM), lambda b, i: (group, b * nq + i, 0))


def _head_spec_kv(group, seq):
    return pl.BlockSpec((N_HEADS, seq, HEAD_DIM), lambda b, i: (group, b, 0))


def _head_spec_meta(group):
    return pl.BlockSpec((N_HEADS, BLK, HEAD_DIM), lambda b, i: (group, 0, 0))


def _fox_attention(bound, proj, proj_m, qx, kx, kxm, out_gain, n_batch, seq):
    nq = seq // BLK
    rows = n_batch * seq
    return pl.pallas_call(
        _fox_kernel,
        grid=(n_batch, nq),
        in_specs=[
            pl.BlockSpec(memory_space=pltpu.SMEM),
            _head_spec_q(0, nq),
            _head_spec_q(0, nq),
            _head_spec_kv(1, seq),
            _head_spec_kv(0, seq),
            _head_spec_kv(2, seq),
            _head_spec_q(3, nq),
            _head_spec_meta(1),
            _head_spec_meta(0),
            _head_spec_meta(2),
            pl.BlockSpec((N_HEADS, 1, HEAD_DIM), lambda b, i: (0, 0, 0)),
        ],
        out_specs=pl.BlockSpec((N_HEADS, BLK, HEAD_DIM), lambda b, i: (0, b * nq + i, 0)),
        out_shape=jax.ShapeDtypeStruct((N_HEADS, rows, HEAD_DIM), BF16),
        scratch_shapes=[pltpu.VMEM((N_HEADS, BLK, LANES), F32)] * 3,
        compiler_params=_cparams(("parallel", "arbitrary")),
        name="fox_attention",
    )(bound, proj, qx, proj, kx, proj, proj, proj_m, kxm, proj_m, out_gain)


def _sb_attention(proj, proj_m, out_gain, n_batch, seq):
    nq = seq // BLK
    rows = n_batch * seq
    return pl.pallas_call(
        _sb_kernel,
        grid=(n_batch, nq),
        in_specs=[
            _head_spec_q(4, nq),
            _head_spec_kv(5, seq),
            _head_spec_kv(6, seq),
            _head_spec_meta(5),
            _head_spec_meta(6),
            pl.BlockSpec((N_HEADS, 1, HEAD_DIM), lambda b, i: (0, 0, 0)),
        ],
        out_specs=pl.BlockSpec((N_HEADS, BLK, HEAD_DIM), lambda b, i: (0, b * nq + i, 0)),
        out_shape=jax.ShapeDtypeStruct((N_HEADS, rows, HEAD_DIM), BF16),
        scratch_shapes=[pltpu.VMEM((N_HEADS, BLK, LANES), F32)] * 2,
        compiler_params=_cparams(("parallel", "arbitrary")),
        name="sb_attention",
    )(proj, proj, proj, proj_m, proj_m, out_gain)


def _outproj_kernel(oa_ref, ob_ref, w_ref, x_ref, o_ref):
    lhs = jnp.concatenate([oa_ref[h] for h in range(N_HEADS)] + [ob_ref[h] for h in range(N_HEADS)], axis=1)
    o_ref[...] = x_ref[...] + jnp.dot(lhs, w_ref[...], preferred_element_type=F32)


def _outproj(oa, ob, w_bf, x2d, tm=256):
    rows = x2d.shape[0]
    return pl.pallas_call(
        _outproj_kernel,
        grid=(rows // tm,),
        in_specs=[
            pl.BlockSpec((N_HEADS, tm, HEAD_DIM), lambda i: (0, i, 0)),
            pl.BlockSpec((N_HEADS, tm, HEAD_DIM), lambda i: (0, i, 0)),
            pl.BlockSpec((2 * D_GROUP, D_MODEL), lambda i: (0, 0)),
            pl.BlockSpec((tm, D_MODEL), lambda i: (i, 0)),
        ],
        out_specs=pl.BlockSpec((tm, D_MODEL), lambda i: (i, 0)),
        out_shape=jax.ShapeDtypeStruct((rows, D_MODEL), F32),
        compiler_params=_cparams(("parallel",)),
        name="outproj",
    )(oa, ob, w_bf, x2d)


ROUTE_TM = 256


def _router_kernel(h_ref, g_ref, whi_ref, wlo_ref, b_ref, u_ref, route_ref, w0_ref, w1_ref, cnt_ref, carry_scr):
    step = pl.program_id(0)
    x = h_ref[...]
    tm = x.shape[0]
    ms = jnp.mean(x * x, axis=-1, keepdims=True)
    u = x * lax.rsqrt(ms + EPS) * g_ref[...]
    for c in range(N_CHUNK):
        u_ref[:, c, :] = u[:, c * LANES:(c + 1) * LANES]
    uhi = u.astype(BF16)
    ulo = (u - uhi.astype(F32)).astype(BF16)
    logits = (jnp.dot(uhi, whi_ref[...], preferred_element_type=F32)
              + jnp.dot(uhi, wlo_ref[...], preferred_element_type=F32)
              + jnp.dot(ulo, whi_ref[...], preferred_element_type=F32)) + b_ref[...]
    lane_i = lax.broadcasted_iota(jnp.int32, (tm, LANES), 1)
    lane = lane_i.astype(F32)
    big = float(4 * LANES)
    neg = -jnp.inf
    c = jnp.where(jnp.logical_and(lane_i >= N_EXPERTS, lane_i < N_EXPERTS + N_GROUPS), logits, neg)
    cmax = jnp.max(c, axis=1, keepdims=True)
    g_sel = jnp.min(jnp.where(c == cmax, lane, big), axis=1, keepdims=True) - N_EXPERTS
    g_gate = 1.0 / jnp.sum(jnp.exp(c - cmax), axis=1, keepdims=True)
    lo = g_sel * EXPERTS_PER_GROUP
    in_group = jnp.logical_and(lane >= lo, lane < lo + EXPERTS_PER_GROUP)
    f = jnp.where(in_group, logits, neg)
    t1 = jnp.max(f, axis=1, keepdims=True)
    i1 = jnp.min(jnp.where(f == t1, lane, big), axis=1, keepdims=True)
    f2 = jnp.where(lane == i1, neg, f)
    t2 = jnp.max(f2, axis=1, keepdims=True)
    i2 = jnp.min(jnp.where(f2 == t2, lane, big), axis=1, keepdims=True)
    d = jnp.exp(t2 - t1)
    w_first = g_gate / (1.0 + d)
    w0_ref[...] = jnp.broadcast_to(w_first, (tm, LANES))
    w1_ref[...] = jnp.broadcast_to(w_first * d, (tm, LANES))

    @pl.when(step == 0)
    def _():
        carry_scr[...] = jnp.zeros_like(carry_scr)

    oh0 = (lane == i1).astype(F32)
    oh1 = (lane == i2).astype(F32)
    oh = oh0 + oh1
    r = lax.broadcasted_iota(jnp.int32, (tm, tm), 0)
    cc = lax.broadcasted_iota(jnp.int32, (tm, tm), 1)
    before = (cc < r).astype(BF16)
    seen = jnp.dot(before, oh.astype(BF16), preferred_element_type=F32) + carry_scr[0:1, :]
    rank0 = jnp.sum(oh0 * seen, axis=1, keepdims=True)
    rank1 = jnp.sum(oh1 * seen, axis=1, keepdims=True)
    total = seen[tm - 1:tm, :] + oh[tm - 1:tm, :]
    carry_scr[...] = jnp.broadcast_to(total, carry_scr.shape)
    cnt_ref[...] = jnp.broadcast_to(total, cnt_ref.shape).astype(jnp.int32)
    vals = jnp.where(lane_i == 0, i1, jnp.where(lane_i == 1, i2, jnp.where(lane_i == 2, rank0,
                     jnp.where(lane_i == 3, rank1, 0.0))))
    route_ref[...] = vals.astype(jnp.int32)


def _router(h1, norm_g, whi, wlo, b_pad):
    rows = h1.shape[0]
    tm = ROUTE_TM
    return pl.pallas_call(
        _router_kernel,
        grid=(rows // tm,),
        in_specs=[
            pl.BlockSpec((tm, D_MODEL), lambda i: (i, 0)),
            pl.BlockSpec((1, D_MODEL), lambda i: (0, 0)),
            pl.BlockSpec((D_MODEL, LANES), lambda i: (0, 0)),
            pl.BlockSpec((D_MODEL, LANES), lambda i: (0, 0)),
            pl.BlockSpec((1, LANES), lambda i: (0, 0)),
        ],
        out_specs=[
            pl.BlockSpec((tm, N_CHUNK, LANES), lambda i: (i, 0, 0)),
            pl.BlockSpec((tm, LANES), lambda i: (i, 0)),
            pl.BlockSpec((tm, LANES), lambda i: (i, 0)),
            pl.BlockSpec((tm, LANES), lambda i: (i, 0)),
            pl.BlockSpec((8, LANES), lambda i: (0, 0)),
        ],
        out_shape=[
            jax.ShapeDtypeStruct((rows, N_CHUNK, LANES), F32),
            jax.ShapeDtypeStruct((rows, LANES), jnp.int32),
            jax.ShapeDtypeStruct((rows, LANES), F32),
            jax.ShapeDtypeStruct((rows, LANES), F32),
            jax.ShapeDtypeStruct((8, LANES), jnp.int32),
        ],
        scratch_shapes=[pltpu.VMEM((8, LANES), F32)],
        compiler_params=_cparams(("arbitrary",)),
        name="router",
    )(h1, norm_g, whi, wlo, b_pad)


MOVE_CHUNK = 128


def _move_kernel(exp_ref, rank_ref, start_ref, src_ref, dst_ref, sem, *, n_rows, to_sorted):
    n_chunks = n_rows // MOVE_CHUNK

    def row_copy(src_row, dst_row):
        return pltpu.make_async_copy(src_ref.at[src_row], dst_ref.at[dst_row], sem)

    def issue(c):
        def body(r, carry):
            i = c * MOVE_CHUNK + r
            sorted_row = start_ref[exp_ref[i]] + rank_ref[i]
            if to_sorted:
                row_copy(i // TOP_K, sorted_row).start()
            else:
                row_copy(sorted_row, i).start()
            return carry
        lax.fori_loop(0, MOVE_CHUNK, body, 0)

    def drain():
        def body(r, carry):
            row_copy(0, 0).wait()
            return carry
        lax.fori_loop(0, MOVE_CHUNK, body, 0)

    issue(0)

    def chunk_body(c, carry):
        issue(c)
        drain()
        return carry

    lax.fori_loop(1, n_chunks, chunk_body, 0)
    drain()


def _move_rows(exp_flat, rank_flat, starts, src, n_rows, to_sorted):
    kern = functools.partial(_move_kernel, n_rows=n_rows, to_sorted=to_sorted)
    return pl.pallas_call(
        kern,
        grid_spec=pltpu.PrefetchScalarGridSpec(
            num_scalar_prefetch=3,
            grid=(1,),
            in_specs=[pl.BlockSpec(memory_space=pl.ANY)],
            out_specs=pl.BlockSpec(memory_space=pl.ANY),
            scratch_shapes=[pltpu.SemaphoreType.DMA(())],
        ),
        out_shape=jax.ShapeDtypeStruct((n_rows,) + src.shape[1:], src.dtype),
        compiler_params=_cparams(("arbitrary",)),
        name="move_rows",
    )(exp_flat, rank_flat, starts, src)


FFN_TM = 256


def _ffn_kernel(tile_ref, exp_ref, lo_ref, hi_ref, first_ref, last_ref, new_ref, slot_ref, next_ref,
                x_ref, w1_hbm, w3_hbm, w2_hbm, o_ref,
                xb_scr, acc_scr, w1_buf, w3_buf, w2_buf, w1_bf, w3_bf, w2_bf, sem):
    it = pl.program_id(0)
    lo = lo_ref[it]
    hi = hi_ref[it]
    slot = slot_ref[it]

    def weight_copies(expert, s):
        return (pltpu.make_async_copy(w1_hbm.at[expert], w1_buf.at[s], sem.at[s, 0]),
                pltpu.make_async_copy(w3_hbm.at[expert], w3_buf.at[s], sem.at[s, 1]),
                pltpu.make_async_copy(w2_hbm.at[expert], w2_buf.at[s], sem.at[s, 2]))

    @pl.when(it == 0)
    def _():
        for cp in weight_copies(exp_ref[0], slot):
            cp.start()

    @pl.when(new_ref[it] == 1)
    def _():
        for cp in weight_copies(exp_ref[it], slot):
            cp.wait()

        @pl.when(next_ref[it] >= 0)
        def _():
            for cp in weight_copies(next_ref[it], 1 - slot):
                cp.start()

        w1_bf[...] = w1_buf[slot].astype(BF16)
        w3_bf[...] = w3_buf[slot].astype(BF16)
        w2_bf[...] = w2_buf[slot].astype(BF16)

    @pl.when(first_ref[it] == 1)
    def _():
        for c in range(N_CHUNK):
            xb_scr[:, c * LANES:(c + 1) * LANES] = x_ref[:, c, :].astype(BF16)
        acc_scr[...] = jnp.zeros_like(acc_scr)

    @pl.when(hi > lo)
    def _():
        x = xb_scr[...]
        a = jnp.dot(x, w1_bf[...], preferred_element_type=F32)
        b = jnp.dot(x, w3_bf[...], preferred_element_type=F32)
        mid = a / (1.0 + jnp.exp(-a)) * b
        rows = tile_ref[it] * FFN_TM + lax.broadcasted_iota(jnp.int32, mid.shape, 0)
        mid = jnp.where(jnp.logical_and(rows >= lo, rows < hi), mid, 0.0).astype(BF16)
        acc_scr[...] += jnp.dot(mid, w2_bf[...], preferred_element_type=F32)

    @pl.when(last_ref[it] == 1)
    def _():
        for c in range(N_CHUNK):
            o_ref[:, c, :] = acc_scr[:, c * LANES:(c + 1) * LANES]


def _grouped_ffn(items, xs, w1, w3, w2):
    n_items = items[0].shape[0]
    n_rows = xs.shape[0]

    def tile_map(i, t, *_):
        return (t[i], 0, 0)

    return pl.pallas_call(
        _ffn_kernel,
        grid_spec=pltpu.PrefetchScalarGridSpec(
            num_scalar_prefetch=len(items),
            grid=(n_items,),
            in_specs=[
                pl.BlockSpec((FFN_TM, N_CHUNK, LANES), tile_map),
                pl.BlockSpec(memory_space=pl.ANY),
                pl.BlockSpec(memory_space=pl.ANY),
                pl.BlockSpec(memory_space=pl.ANY),
            ],
            out_specs=pl.BlockSpec((FFN_TM, N_CHUNK, LANES), tile_map),
            scratch_shapes=[
                pltpu.VMEM((FFN_TM, D_MODEL), BF16),
                pltpu.VMEM((FFN_TM, D_MODEL), F32),
                pltpu.VMEM((2, D_MODEL, D_EXPERT), F32),
                pltpu.VMEM((2, D_MODEL, D_EXPERT), F32),
                pltpu.VMEM((2, D_EXPERT, D_MODEL), F32),
                pltpu.VMEM((D_MODEL, D_EXPERT), BF16),
                pltpu.VMEM((D_MODEL, D_EXPERT), BF16),
                pltpu.VMEM((D_EXPERT, D_MODEL), BF16),
                pltpu.SemaphoreType.DMA((2, 3)),
            ],
        ),
        out_shape=jax.ShapeDtypeStruct((n_rows, N_CHUNK, LANES), F32),
        compiler_params=_cparams(("arbitrary",)),
        name="grouped_ffn",
    )(*items, xs, w1, w3, w2)


def _combine_kernel(h_ref, y0_ref, y1_ref, w0_ref, w1_ref, o_ref):
    for c in range(N_CHUNK):
        sl = slice(c * LANES, (c + 1) * LANES)
        o_ref[:, sl] = h_ref[:, sl] + w0_ref[...] * y0_ref[:, c, :] + w1_ref[...] * y1_ref[:, c, :]


def _combine(h1, yg, w0b, w1b, tm=256):
    rows = h1.shape[0]
    nt = rows // tm
    return pl.pallas_call(
        _combine_kernel,
        grid=(nt,),
        in_specs=[
            pl.BlockSpec((tm, D_MODEL), lambda i: (i, 0)),
            pl.BlockSpec((tm, N_CHUNK, LANES), lambda i: (i, 0, 0)),
            pl.BlockSpec((tm, N_CHUNK, LANES), lambda i: (i + nt, 0, 0)),
            pl.BlockSpec((tm, LANES), lambda i: (i, 0)),
            pl.BlockSpec((tm, LANES), lambda i: (i, 0)),
        ],
        out_specs=pl.BlockSpec((tm, D_MODEL), lambda i: (i, 0)),
        out_shape=jax.ShapeDtypeStruct((rows, D_MODEL), F32),
        compiler_params=_cparams(("parallel",)),
        name="combine",
    )(h1, yg, yg, w0b, w1b)


def _work_items(counts, n_assign):
    ends = jnp.cumsum(counts)
    starts = (ends - counts).astype(jnp.int32)
    n_tiles = n_assign // FFN_TM
    tile_starts = jnp.arange(n_tiles, dtype=jnp.int32) * FFN_TM
    seg_lo = jnp.sort(jnp.concatenate([tile_starts, starts]))
    seg_hi = jnp.concatenate([seg_lo[1:], jnp.array([n_assign], jnp.int32)])
    item_tile = jnp.minimum(seg_lo // FFN_TM, n_tiles - 1).astype(jnp.int32)
    item_exp = jnp.minimum(jnp.sum((ends[None, :] <= seg_lo[:, None]).astype(jnp.int32), axis=1), N_EXPERTS - 1)
    item_exp = item_exp.astype(jnp.int32)
    item_first = (seg_lo == item_tile * FFN_TM).astype(jnp.int32)
    item_last = (seg_hi == (item_tile + 1) * FFN_TM).astype(jnp.int32)
    n_items = seg_lo.shape[0]
    item_new = jnp.concatenate([jnp.ones((1,), jnp.int32), (item_exp[1:] != item_exp[:-1]).astype(jnp.int32)])
    ordinal = jnp.cumsum(item_new) - 1
    run_exp = jnp.full((n_items + 1,), -1, jnp.int32).at[ordinal].set(item_exp)
    item_next = run_exp[ordinal + 1]
    item_slot = (ordinal % 2).astype(jnp.int32)
    items = (item_tile, item_exp, seg_lo, seg_hi, item_first, item_last, item_new, item_slot, item_next)
    return starts, items


def kernel(x, meta_tokens, norm1_g, w_in, b_f, q_gain, k_gain, fox_out_gain, sb_out_gain, w_out, norm2_g,
           w_coarse, b_coarse, w_fine, b_fine, w1, w3, w2):
    assert norm1_g.shape[0] == 1, "single-layer block"
    n_batch, seq, _ = x.shape
    n_tok = n_batch * seq
    n_assign = n_tok * TOP_K
    scale = HEAD_DIM ** -0.5
    x2d = x.reshape(n_tok, D_MODEL)

    w_in_t = w_in[0].T
    w_in_bf = _cast_bf16(w_in_t, D_PROJ)
    w_out_bf = _cast_bf16(w_out[0], 2 * D_GROUP)
    wft_bf = jnp.pad(w_in_t[D_PROJ:], ((0, LANES - N_HEADS), (0, 0))).astype(BF16)
    bf_pad = jnp.pad(b_f[0], (0, LANES - N_HEADS)).reshape(1, LANES)
    ones = jnp.ones((HEAD_DIM,), F32)
    gains = jnp.stack([q_gain[0] * scale, k_gain[0], ones, ones, ones * scale, ones, ones]).reshape(
        N_PROJ_GROUPS, 1, HEAD_DIM)
    meta_pad = jnp.pad(meta_tokens.astype(F32), ((0, BLK - N_META), (0, 0)))
    n1 = norm1_g[0].reshape(1, D_MODEL)

    proj, lf = _inproj(x2d, n1, w_in_bf, wft_bf, bf_pad, gains, tm=1024)
    proj_m, lf_m = _inproj(meta_pad, n1, w_in_bf, wft_bf, bf_pad, gains, tm=BLK)
    qx, kx, kxm = _cumgate(lf_m, lf, n_batch, seq)
    qk_bound = 1.02 * HEAD_DIM * scale * jnp.max(jnp.abs(q_gain[0])) * jnp.max(jnp.abs(k_gain[0]))
    bound = (2.0 * qk_bound).reshape(1).astype(F32)
    oa = _fox_attention(bound, proj, proj_m, qx, kx, kxm, fox_out_gain[0].reshape(N_HEADS, 1, HEAD_DIM),
                        n_batch, seq)
    ob = _sb_attention(proj, proj_m, sb_out_gain[0].reshape(N_HEADS, 1, HEAD_DIM), n_batch, seq)
    h1 = _outproj(oa, ob, w_out_bf, x2d)

    wr = jnp.pad(jnp.concatenate([w_fine[0], w_coarse[0]], axis=1),
                 ((0, 0), (0, LANES - N_GROUPS - N_EXPERTS)))
    wr_hi = wr.astype(BF16)
    wr_lo = (wr - wr_hi.astype(F32)).astype(BF16)
    br = jnp.pad(jnp.concatenate([b_fine[0], b_coarse[0]]), (0, LANES - N_GROUPS - N_EXPERTS)).reshape(1, LANES)
    u2, route, w0b, w1b, cnt = _router(h1, norm2_g[0].reshape(1, D_MODEL), wr_hi, wr_lo, br)
    starts, items = _work_items(cnt[0, :N_EXPERTS], n_assign)
    eid = route[:, 0:TOP_K]
    rank = route[:, TOP_K:2 * TOP_K]
    xs = _move_rows(eid.reshape(-1), rank.reshape(-1), starts, u2, n_assign, True)
    ys = _grouped_ffn(items, xs, w1[0], w3[0], w2[0])
    yg = _move_rows(eid.T.reshape(-1), rank.T.reshape(-1), starts, ys, n_assign, False)
    out = _combine(h1, yg, w0b, w1b)
    return out.reshape(n_batch, seq, D_MODEL)
```

```python
import functools

import jax
import jax.numpy as jnp
from jax import lax
from jax.experimental import pallas as pl
from jax.experimental.pallas import tpu as pltpu

F32 = jnp.float32
BF16 = jnp.bfloat16

D_MODEL = 2048
N_META = 16
HEAD_DIM = 128
N_HEADS = 8
D_GROUP = N_HEADS * HEAD_DIM
N_PROJ_GROUPS = 7
D_PROJ = N_PROJ_GROUPS * D_GROUP
N_GROUPS = 8
EXPERTS_PER_GROUP = 8
N_EXPERTS = 64
TOP_K = 2
D_EXPERT = 512
EPS = 1e-6
LANES = 128
N_CHUNK = D_MODEL // LANES
BLK = 128
SKIP_LOG = 88.0
VMEM_LIMIT = 56 * 1024 * 1024


def _cparams(sem, vmem=VMEM_LIMIT):
    return pltpu.CompilerParams(dimension_semantics=sem, vmem_limit_bytes=vmem)


def _log_sigmoid(x):
    return jnp.minimum(x, 0.0) - jnp.log(1.0 + jnp.exp(-jnp.abs(x)))


def _split3(x):
    hi = x.astype(BF16)
    r1 = x - hi.astype(F32)
    mid = r1.astype(BF16)
    lo = (r1 - mid.astype(F32)).astype(BF16)
    return hi, mid, lo


def _dot_nt(a, b):
    return lax.dot_general(a, b, (((1,), (1,)), ((), ())), preferred_element_type=F32)


def _cast_kernel(x_ref, o_ref):
    o_ref[...] = x_ref[...].astype(o_ref.dtype)


def _cast_bf16(w, n_rows, tr=512):
    n_cols = w.shape[1]
    return pl.pallas_call(
        _cast_kernel,
        grid=(n_rows // tr,),
        in_specs=[pl.BlockSpec((tr, n_cols), lambda i: (i, 0))],
        out_specs=pl.BlockSpec((tr, n_cols), lambda i: (i, 0)),
        out_shape=jax.ShapeDtypeStruct((n_rows, n_cols), BF16),
        compiler_params=_cparams(("parallel",)),
        name="cast_bf16",
    )(w)


def _inproj_kernel(x_ref, g_ref, wt_ref, wft_ref, bf_ref, gain_ref, o_ref, lf_ref, u_ref, *, heads_per_tile):
    j = pl.program_id(1)

    @pl.when(j == 0)
    def _():
        x = x_ref[...]
        ms = jnp.mean(x * x, axis=-1, keepdims=True)
        u = (x * lax.rsqrt(ms + EPS) * g_ref[...]).astype(BF16)
        u_ref[...] = u
        f = _dot_nt(u, wft_ref[...]) + bf_ref[...]
        lf_ref[...] = _log_sigmoid(f)

    acc = _dot_nt(u_ref[...], wt_ref[...])
    gain = gain_ref[0]
    tiles_per_group = N_HEADS // heads_per_tile
    group = j // tiles_per_group

    @pl.when(group < 2)
    def _():
        for h in range(heads_per_tile):
            y = acc[:, h * HEAD_DIM:(h + 1) * HEAD_DIM]
            ms = jnp.mean(y * y, axis=-1, keepdims=True)
            o_ref[h] = (y * lax.rsqrt(ms + EPS) * gain).astype(BF16)

    @pl.when(group >= 2)
    def _():
        for h in range(heads_per_tile):
            o_ref[h] = (acc[:, h * HEAD_DIM:(h + 1) * HEAD_DIM] * gain).astype(BF16)


def _inproj(x2d, norm_g, wt_bf, wft_bf, bf_pad, gains, tm, tn=512):
    rows = x2d.shape[0]
    hpt = tn // HEAD_DIM
    kern = functools.partial(_inproj_kernel, heads_per_tile=hpt)
    return pl.pallas_call(
        kern,
        grid=(rows // tm, D_PROJ // tn),
        in_specs=[
            pl.BlockSpec((tm, D_MODEL), lambda i, j: (i, 0)),
            pl.BlockSpec((1, D_MODEL), lambda i, j: (0, 0)),
            pl.BlockSpec((tn, D_MODEL), lambda i, j: (j, 0)),
            pl.BlockSpec((LANES, D_MODEL), lambda i, j: (0, 0)),
            pl.BlockSpec((1, LANES), lambda i, j: (0, 0)),
            pl.BlockSpec((1, 1, HEAD_DIM), lambda i, j: (j // (N_HEADS // hpt), 0, 0)),
        ],
        out_specs=[
            pl.BlockSpec((hpt, tm, HEAD_DIM), lambda i, j: (j, i, 0)),
            pl.BlockSpec((tm, LANES), lambda i, j: (i, 0)),
        ],
        out_shape=[
            jax.ShapeDtypeStruct((D_PROJ // HEAD_DIM, rows, HEAD_DIM), BF16),
            jax.ShapeDtypeStruct((rows, LANES), F32),
        ],
        scratch_shapes=[pltpu.VMEM((tm, D_MODEL), BF16)],
        compiler_params=_cparams(("parallel", "arbitrary")),
        name="inproj",
    )(x2d, norm_g, wt_bf, wft_bf, bf_pad, gains)


def _cumgate_kernel(lfm_ref, lf_ref, qx_ref, kx_ref, kxm_ref, *, n_blk):
    row = lax.broadcasted_iota(jnp.int32, (BLK, BLK), 0)
    col = lax.broadcasted_iota(jnp.int32, (BLK, BLK), 1)
    tri = (col <= row).astype(BF16)
    one = jnp.ones((BLK, LANES), F32)
    zero = jnp.zeros((BLK, LANES), F32)

    def prefix(x):
        hi, mid, lo = _split3(x)
        return (jnp.dot(tri, hi, preferred_element_type=F32)
                + jnp.dot(tri, mid, preferred_element_type=F32)
                + jnp.dot(tri, lo, preferred_element_type=F32))

    def ext(cum, h):
        c = jnp.broadcast_to(cum[:, h:h + 1], (BLK, LANES))
        hi, mid, lo = (t.astype(F32) for t in _split3(c))
        qx = jnp.where(col == 0, hi, jnp.where(col == 1, mid, jnp.where(col == 2, lo,
                       jnp.where(col < 6, one, zero))))
        kx = jnp.where(col < 3, one, jnp.where(col == 3, -hi, jnp.where(col == 4, -mid,
                       jnp.where(col == 5, -lo, zero))))
        return qx.astype(BF16), kx.astype(BF16)

    lfm = jnp.where(row < N_META, lfm_ref[...], 0.0)
    cum_m = prefix(lfm)
    for h in range(N_HEADS):
        _, kx = ext(cum_m, h)
        kxm_ref[h] = kx
    carry = cum_m[BLK - 1:BLK, :]
    for b in range(n_blk):
        cum = prefix(lf_ref[b * BLK:(b + 1) * BLK, :]) + carry
        carry = cum[BLK - 1:BLK, :]
        for h in range(N_HEADS):
            qx, kx = ext(cum, h)
            qx_ref[h, b * BLK:(b + 1) * BLK, :] = qx
            kx_ref[h, b * BLK:(b + 1) * BLK, :] = kx


def _cumgate(lf_meta, lf_real, n_batch, seq):
    kern = functools.partial(_cumgate_kernel, n_blk=seq // BLK)
    return pl.pallas_call(
        kern,
        grid=(n_batch,),
        in_specs=[
            pl.BlockSpec((BLK, LANES), lambda b: (0, 0)),
            pl.BlockSpec((seq, LANES), lambda b: (b, 0)),
        ],
        out_specs=[
            pl.BlockSpec((N_HEADS, seq, LANES), lambda b: (0, b, 0)),
            pl.BlockSpec((N_HEADS, seq, LANES), lambda b: (0, b, 0)),
            pl.BlockSpec((N_HEADS, BLK, LANES), lambda b: (0, 0, 0)),
        ],
        out_shape=[
            jax.ShapeDtypeStruct((N_HEADS, n_batch * seq, LANES), BF16),
            jax.ShapeDtypeStruct((N_HEADS, n_batch * seq, LANES), BF16),
            jax.ShapeDtypeStruct((N_HEADS, BLK, LANES), BF16),
        ],
        compiler_params=_cparams(("arbitrary",)),
        name="cumgate",
    )(lf_meta, lf_real)


def _bdot_nt(a, b):
    return lax.dot_general(a, b, (((2,), (2,)), ((0,), (0,))), preferred_element_type=F32)


def _bdot_nn(a, b):
    return lax.dot_general(a, b, (((2,), (1,)), ((0,), (0,))), preferred_element_type=F32)


def _fox_kernel(bound_ref, q_ref, qx_ref, k_ref, kx_ref, v_ref, g_ref, km_ref, kxm_ref, vm_ref, gain_ref,
                o_ref, m_scr, l_scr, acc_scr):
    i = pl.program_id(1)
    row = lax.broadcasted_iota(jnp.int32, (N_HEADS, BLK, BLK), 1)
    col = lax.broadcasted_iota(jnp.int32, (N_HEADS, BLK, BLK), 2)

    def sweep(kb, kxb, vb, mask, first):
        qa = jnp.concatenate([q_ref[...], qx_ref[...]], axis=2)
        ka = jnp.concatenate([kb, kxb], axis=2)
        s = _bdot_nt(qa, ka)
        if mask is not None:
            s = jnp.where(mask, s, -jnp.inf)
        m_cur = jnp.max(s, axis=2, keepdims=True)
        if first:
            m_new = jnp.broadcast_to(m_cur, s.shape)
            p = jnp.exp(s - m_new)
            l_scr[...] = jnp.broadcast_to(jnp.sum(p, axis=2, keepdims=True), s.shape)
            acc_scr[...] = _bdot_nn(p.astype(BF16), vb)
        else:
            m_prev = m_scr[...]
            m_new = jnp.maximum(m_prev, m_cur)
            alpha = jnp.exp(m_prev - m_new)
            p = jnp.exp(s - m_new)
            l_scr[...] = alpha * l_scr[...] + jnp.sum(p, axis=2, keepdims=True)
            acc_scr[...] = alpha * acc_scr[...] + _bdot_nn(p.astype(BF16), vb)
        m_scr[...] = m_new
        return jnp.max(s[:, :, 0:1] - m_new[:, :, 0:1])

    def real_block(j):
        start = pl.multiple_of(j * BLK, BLK)
        return (k_ref[:, pl.ds(start, BLK), :], kx_ref[:, pl.ds(start, BLK), :], v_ref[:, pl.ds(start, BLK), :])

    bound = bound_ref[0]
    gap0 = sweep(*real_block(i), col <= row, True)

    def cond(c):
        j, done = c
        return jnp.logical_and(j >= 0, done == 0)

    def body(c):
        j, _ = c
        gap = sweep(*real_block(j), None, False)
        return j - 1, (gap + bound < -SKIP_LOG).astype(jnp.int32)

    _, done = lax.while_loop(cond, body, (i - 1, (gap0 + bound < -SKIP_LOG).astype(jnp.int32)))

    @pl.when(done == 0)
    def _():
        sweep(km_ref[...], kxm_ref[...], vm_ref[...], col < N_META, False)

    o = acc_scr[...] / l_scr[...]
    ms = jnp.mean(o * o, axis=-1, keepdims=True)
    gate = 1.0 / (1.0 + jnp.exp(-g_ref[...].astype(F32)))
    o_ref[...] = (o * lax.rsqrt(ms + EPS) * gain_ref[...] * gate).astype(BF16)


def _sb_kernel(q_ref, k_ref, v_ref, km_ref, vm_ref, gain_ref, o_ref, carry_scr, acc_scr):
    i = pl.program_id(1)
    row = lax.broadcasted_iota(jnp.int32, (N_HEADS, BLK, BLK), 1)
    col = lax.broadcasted_iota(jnp.int32, (N_HEADS, BLK, BLK), 2)
    r2 = lax.broadcasted_iota(jnp.int32, (BLK, 2 * BLK), 0)
    c2 = lax.broadcasted_iota(jnp.int32, (BLK, 2 * BLK), 1)
    suffix = jnp.logical_or(c2 >= BLK, r2 > c2).astype(BF16)

    def sweep(kb, vb, mask, first):
        z = _bdot_nt(q_ref[...], kb)
        sp = jnp.maximum(z, 0.0) + jnp.log(1.0 + jnp.exp(-jnp.abs(z)))
        lk = -sp
        if mask is not None:
            lk = jnp.where(mask, lk, 0.0)
        hi = lk.astype(BF16)
        lo = (lk - hi.astype(F32)).astype(BF16)
        t = (jnp.dot(hi.reshape(N_HEADS * BLK, BLK), suffix, preferred_element_type=F32)
             + jnp.dot(lo.reshape(N_HEADS * BLK, BLK), suffix, preferred_element_type=F32))
        t = t.reshape(N_HEADS, BLK, 2 * BLK)
        later = t[:, :, :BLK]
        rowsum = t[:, :, BLK:]
        if not first:
            later = later + carry_scr[...]
        a = jnp.exp(z - sp + later)
        if mask is not None:
            a = jnp.where(mask, a, 0.0)
        pv = _bdot_nn(a.astype(BF16), vb)
        if first:
            acc_scr[...] = pv
            c_new = rowsum
        else:
            acc_scr[...] = acc_scr[...] + pv
            c_new = carry_scr[...] + rowsum
        carry_scr[...] = c_new
        return jnp.max(c_new[:, :, 0:1])

    def real_block(j):
        start = pl.multiple_of(j * BLK, BLK)
        return k_ref[:, pl.ds(start, BLK), :], v_ref[:, pl.ds(start, BLK), :]

    top0 = sweep(*real_block(i), col < row, True)

    def cond(c):
        j, done = c
        return jnp.logical_and(j >= 0, done == 0)

    def body(c):
        j, _ = c
        top = sweep(*real_block(j), None, False)
        return j - 1, (top < -SKIP_LOG).astype(jnp.int32)

    _, done = lax.while_loop(cond, body, (i - 1, (top0 < -SKIP_LOG).astype(jnp.int32)))

    @pl.when(done == 0)
    def _():
        sweep(km_ref[...], vm_ref[...], col < N_META, False)

    o = acc_scr[...]
    ms = jnp.mean(o * o, axis=-1, keepdims=True)
    o_ref[...] = (o * lax.rsqrt(ms + EPS) * gain_ref[...]).astype(BF16)


def _head_spec_q(group, nq):
    return pl.BlockSpec((N_HEADS, BLK, HEAD_DIM), lambda b, i: (group, b * nq + i, 0))


def _head_spec_kv(group, seq):
    return pl.BlockSpec((N_HEADS, seq, HEAD_DIM), lambda b, i: (group, b, 0))


def _head_spec_meta(group):
    return pl.BlockSpec((N_HEADS, BLK, HEAD_DIM), lambda b, i: (group, 0, 0))


def _fox_attention(bound, proj, proj_m, qx, kx, kxm, out_gain, n_batch, seq):
    nq = seq // BLK
    rows = n_batch * seq
    return pl.pallas_call(
        _fox_kernel,
        grid=(n_batch, nq),
        in_specs=[
            pl.BlockSpec(memory_space=pltpu.SMEM),
            _head_spec_q(0, nq),
            _head_spec_q(0, nq),
            _head_spec_kv(1, seq),
            _head_spec_kv(0, seq),
            _head_spec_kv(2, seq),
            _head_spec_q(3, nq),
            _head_spec_meta(1),
            _head_spec_meta(0),
            _head_spec_meta(2),
            pl.BlockSpec((N_HEADS, 1, HEAD_DIM), lambda b, i: (0, 0, 0)),
        ],
        out_specs=pl.BlockSpec((N_HEADS, BLK, HEAD_DIM), lambda b, i: (0, b * nq + i, 0)),
        out_shape=jax.ShapeDtypeStruct((N_HEADS, rows, HEAD_DIM), BF16),
        scratch_shapes=[pltpu.VMEM((N_HEADS, BLK, LANES), F32)] * 3,
        compiler_params=_cparams(("parallel", "arbitrary")),
        name="fox_attention",
    )(bound, proj, qx, proj, kx, proj, proj, proj_m, kxm, proj_m, out_gain)


def _sb_attention(proj, proj_m, out_gain, n_batch, seq):
    nq = seq // BLK
    rows = n_batch * seq
    return pl.pallas_call(
        _sb_kernel,
        grid=(n_batch, nq),
        in_specs=[
            _head_spec_q(4, nq),
            _head_spec_kv(5, seq),
            _head_spec_kv(6, seq),
            _head_spec_meta(5),
            _head_spec_meta(6),
            pl.BlockSpec((N_HEADS, 1, HEAD_DIM), lambda b, i: (0, 0, 0)),
        ],
        out_specs=pl.BlockSpec((N_HEADS, BLK, HEAD_DIM), lambda b, i: (0, b * nq + i, 0)),
        out_shape=jax.ShapeDtypeStruct((N_HEADS, rows, HEAD_DIM), BF16),
        scratch_shapes=[pltpu.VMEM((N_HEADS, BLK, LANES), F32)] * 2,
        compiler_params=_cparams(("parallel", "arbitrary")),
        name="sb_attention",
    )(proj, proj, proj, proj_m, proj_m, out_gain)


def _outproj_kernel(oa_ref, ob_ref, w_ref, x_ref, o_ref):
    lhs = jnp.concatenate([oa_ref[h] for h in range(N_HEADS)] + [ob_ref[h] for h in range(N_HEADS)], axis=1)
    o_ref[...] = x_ref[...] + jnp.dot(lhs, w_ref[...], preferred_element_type=F32)


def _outproj(oa, ob, w_bf, x2d, tm=256):
    rows = x2d.shape[0]
    return pl.pallas_call(
        _outproj_kernel,
        grid=(rows // tm,),
        in_specs=[
            pl.BlockSpec((N_HEADS, tm, HEAD_DIM), lambda i: (0, i, 0)),
            pl.BlockSpec((N_HEADS, tm, HEAD_DIM), lambda i: (0, i, 0)),
            pl.BlockSpec((2 * D_GROUP, D_MODEL), lambda i: (0, 0)),
            pl.BlockSpec((tm, D_MODEL), lambda i: (i, 0)),
        ],
        out_specs=pl.BlockSpec((tm, D_MODEL), lambda i: (i, 0)),
        out_shape=jax.ShapeDtypeStruct((rows, D_MODEL), F32),
        compiler_params=_cparams(("parallel",)),
        name="outproj",
    )(oa, ob, w_bf, x2d)


ROUTE_TM = 256


def _router_kernel(h_ref, g_ref, whi_ref, wlo_ref, b_ref, u_ref, route_ref, w0_ref, w1_ref, cnt_ref, carry_scr):
    step = pl.program_id(0)
    x = h_ref[...]
    tm = x.shape[0]
    ms = jnp.mean(x * x, axis=-1, keepdims=True)
    u = x * lax.rsqrt(ms + EPS) * g_ref[...]
    u_ref[...] = u
    uhi = u.astype(BF16)
    ulo = (u - uhi.astype(F32)).astype(BF16)
    logits = (jnp.dot(uhi, whi_ref[...], preferred_element_type=F32)
              + jnp.dot(uhi, wlo_ref[...], preferred_element_type=F32)
              + jnp.dot(ulo, whi_ref[...], preferred_element_type=F32)) + b_ref[...]
    lane_i = lax.broadcasted_iota(jnp.int32, (tm, LANES), 1)
    lane = lane_i.astype(F32)
    big = float(4 * LANES)
    neg = -jnp.inf
    c = jnp.where(jnp.logical_and(lane_i >= N_EXPERTS, lane_i < N_EXPERTS + N_GROUPS), logits, neg)
    cmax = jnp.max(c, axis=1, keepdims=True)
    g_sel = jnp.min(jnp.where(c == cmax, lane, big), axis=1, keepdims=True) - N_EXPERTS
    g_gate = 1.0 / jnp.sum(jnp.exp(c - cmax), axis=1, keepdims=True)
    lo = g_sel * EXPERTS_PER_GROUP
    in_group = jnp.logical_and(lane >= lo, lane < lo + EXPERTS_PER_GROUP)
    f = jnp.where(in_group, logits, neg)
    t1 = jnp.max(f, axis=1, keepdims=True)
    i1 = jnp.min(jnp.where(f == t1, lane, big), axis=1, keepdims=True)
    f2 = jnp.where(lane == i1, neg, f)
    t2 = jnp.max(f2, axis=1, keepdims=True)
    i2 = jnp.min(jnp.where(f2 == t2, lane, big), axis=1, keepdims=True)
    d = jnp.exp(t2 - t1)
    w_first = g_gate / (1.0 + d)
    w0_ref[...] = jnp.broadcast_to(w_first, (tm, LANES))
    w1_ref[...] = jnp.broadcast_to(w_first * d, (tm, LANES))

    @pl.when(step == 0)
    def _():
        carry_scr[...] = jnp.zeros_like(carry_scr)

    oh0 = (lane == i1).astype(F32)
    oh1 = (lane == i2).astype(F32)
    oh = oh0 + oh1
    r = lax.broadcasted_iota(jnp.int32, (tm, tm), 0)
    cc = lax.broadcasted_iota(jnp.int32, (tm, tm), 1)
    before = (cc < r).astype(BF16)
    seen = jnp.dot(before, oh.astype(BF16), preferred_element_type=F32) + carry_scr[0:1, :]
    rank0 = jnp.sum(oh0 * seen, axis=1, keepdims=True)
    rank1 = jnp.sum(oh1 * seen, axis=1, keepdims=True)
    total = seen[tm - 1:tm, :] + oh[tm - 1:tm, :]
    carry_scr[...] = jnp.broadcast_to(total, carry_scr.shape)
    cnt_ref[...] = jnp.broadcast_to(total, cnt_ref.shape).astype(jnp.int32)
    vals = jnp.where(lane_i == 0, i1, jnp.where(lane_i == 1, i2, jnp.where(lane_i == 2, rank0,
                     jnp.where(lane_i == 3, rank1, 0.0))))
    route_ref[...] = vals.astype(jnp.int32)


def _router(h1, norm_g, whi, wlo, b_pad):
    rows = h1.shape[0]
    tm = ROUTE_TM
    return pl.pallas_call(
        _router_kernel,
        grid=(rows // tm,),
        in_specs=[
            pl.BlockSpec((tm, D_MODEL), lambda i: (i, 0)),
            pl.BlockSpec((1, D_MODEL), lambda i: (0, 0)),
            pl.BlockSpec((D_MODEL, LANES), lambda i: (0, 0)),
            pl.BlockSpec((D_MODEL, LANES), lambda i: (0, 0)),
            pl.BlockSpec((1, LANES), lambda i: (0, 0)),
        ],
        out_specs=[
            pl.BlockSpec((tm, D_MODEL), lambda i: (i, 0)),
            pl.BlockSpec((tm, LANES), lambda i: (i, 0)),
            pl.BlockSpec((tm, LANES), lambda i: (i, 0)),
            pl.BlockSpec((tm, LANES), lambda i: (i, 0)),
            pl.BlockSpec((8, LANES), lambda i: (0, 0)),
        ],
        out_shape=[
            jax.ShapeDtypeStruct((rows, D_MODEL), F32),
            jax.ShapeDtypeStruct((rows, LANES), jnp.int32),
            jax.ShapeDtypeStruct((rows, LANES), F32),
            jax.ShapeDtypeStruct((rows, LANES), F32),
            jax.ShapeDtypeStruct((8, LANES), jnp.int32),
        ],
        scratch_shapes=[pltpu.VMEM((8, LANES), F32)],
        compiler_params=_cparams(("arbitrary",)),
        name="router",
    )(h1, norm_g, whi, wlo, b_pad)


MOVE_TM = 256
MOVE_UNROLL = 8


def _scatter_kernel(pos_ref, u_ref, xs_ref, sem):
    base = pl.program_id(0) * MOVE_TM

    def body(t, carry):
        for k in range(TOP_K):
            dst = pos_ref[(base + t) * TOP_K + k]
            pltpu.make_async_copy(u_ref.at[pl.ds(t, 1)], xs_ref.at[pl.ds(dst, 1)], sem).start()
        return carry

    lax.fori_loop(0, MOVE_TM, body, 0, unroll=MOVE_UNROLL)
    for _ in range(TOP_K):
        pltpu.make_async_copy(u_ref, xs_ref.at[pl.ds(0, MOVE_TM)], sem).wait()


def _scatter_rows(pos_flat, u2):
    n_tok = u2.shape[0]
    return pl.pallas_call(
        _scatter_kernel,
        grid_spec=pltpu.PrefetchScalarGridSpec(
            num_scalar_prefetch=1,
            grid=(n_tok // MOVE_TM,),
            in_specs=[pl.BlockSpec((MOVE_TM, D_MODEL), lambda i, pos: (i, 0))],
            out_specs=pl.BlockSpec(memory_space=pl.ANY),
            scratch_shapes=[pltpu.SemaphoreType.DMA(())],
        ),
        out_shape=jax.ShapeDtypeStruct((n_tok * TOP_K, D_MODEL), u2.dtype),
        compiler_params=_cparams(("arbitrary",)),
        name="scatter_rows",
    )(pos_flat, u2)


FFN_TM = 256


def _ffn_kernel(tile_ref, exp_ref, lo_ref, hi_ref, first_ref, new_ref, slot_ref, next_ref,
                x_ref, w1_hbm, w3_hbm, w2_hbm, o_ref,
                xb_scr, w1_buf, w3_buf, w2_buf, w1_bf, w3_bf, w2_bf, sem):
    it = pl.program_id(0)
    lo = lo_ref[it]
    hi = hi_ref[it]
    slot = slot_ref[it]

    def weight_copies(expert, s):
        return (pltpu.make_async_copy(w1_hbm.at[expert], w1_buf.at[s], sem.at[s, 0]),
                pltpu.make_async_copy(w3_hbm.at[expert], w3_buf.at[s], sem.at[s, 1]),
                pltpu.make_async_copy(w2_hbm.at[expert], w2_buf.at[s], sem.at[s, 2]))

    @pl.when(it == 0)
    def _():
        for cp in weight_copies(exp_ref[0], slot):
            cp.start()

    @pl.when(new_ref[it] == 1)
    def _():
        for cp in weight_copies(exp_ref[it], slot):
            cp.wait()

        @pl.when(next_ref[it] >= 0)
        def _():
            for cp in weight_copies(next_ref[it], 1 - slot):
                cp.start()

        w1_bf[...] = w1_buf[slot].astype(BF16)
        w3_bf[...] = w3_buf[slot].astype(BF16)
        w2_bf[...] = w2_buf[slot].astype(BF16)

    @pl.when(first_ref[it] == 1)
    def _():
        xb_scr[...] = x_ref[...].astype(BF16)
        o_ref[...] = jnp.zeros_like(o_ref)

    @pl.when(hi > lo)
    def _():
        x = xb_scr[...]
        a = jnp.dot(x, w1_bf[...], preferred_element_type=F32)
        b = jnp.dot(x, w3_bf[...], preferred_element_type=F32)
        mid = a / (1.0 + jnp.exp(-a)) * b
        rows = tile_ref[it] * FFN_TM + lax.broadcasted_iota(jnp.int32, mid.shape, 0)
        mid = jnp.where(jnp.logical_and(rows >= lo, rows < hi), mid, 0.0).astype(BF16)
        o_ref[...] += jnp.dot(mid, w2_bf[...], preferred_element_type=F32)


def _grouped_ffn(items, xs, w1, w3, w2):
    n_items = items[0].shape[0]
    n_rows = xs.shape[0]

    def tile_map(i, t, *_):
        return (t[i], 0)

    return pl.pallas_call(
        _ffn_kernel,
        grid_spec=pltpu.PrefetchScalarGridSpec(
            num_scalar_prefetch=len(items),
            grid=(n_items,),
            in_specs=[
                pl.BlockSpec((FFN_TM, D_MODEL), tile_map),
                pl.BlockSpec(memory_space=pl.ANY),
                pl.BlockSpec(memory_space=pl.ANY),
                pl.BlockSpec(memory_space=pl.ANY),
            ],
            out_specs=pl.BlockSpec((FFN_TM, D_MODEL), tile_map),
            scratch_shapes=[
                pltpu.VMEM((FFN_TM, D_MODEL), BF16),
                pltpu.VMEM((2, D_MODEL, D_EXPERT), F32),
                pltpu.VMEM((2, D_MODEL, D_EXPERT), F32),
                pltpu.VMEM((2, D_EXPERT, D_MODEL), F32),
                pltpu.VMEM((D_MODEL, D_EXPERT), BF16),
                pltpu.VMEM((D_MODEL, D_EXPERT), BF16),
                pltpu.VMEM((D_EXPERT, D_MODEL), BF16),
                pltpu.SemaphoreType.DMA((2, 3)),
            ],
        ),
        out_shape=jax.ShapeDtypeStruct((n_rows, D_MODEL), F32),
        compiler_params=_cparams(("arbitrary",)),
        name="grouped_ffn",
    )(*items, xs, w1, w3, w2)


def _combine_kernel(pos_ref, h_ref, w0_ref, w1_ref, ys_ref, o_ref, ybuf, sem):
    step = pl.program_id(0)
    n_steps = pl.num_programs(0)

    def fetch(s, slot):
        def body(t, carry):
            for k in range(TOP_K):
                src = pos_ref[(s * MOVE_TM + t) * TOP_K + k]
                pltpu.make_async_copy(ys_ref.at[pl.ds(src, 1)], ybuf.at[slot, k, pl.ds(t, 1)], sem.at[slot]).start()
            return carry
        lax.fori_loop(0, MOVE_TM, body, 0, unroll=MOVE_UNROLL)

    @pl.when(step == 0)
    def _():
        fetch(0, 0)

    slot = step % 2

    @pl.when(step + 1 < n_steps)
    def _():
        fetch(step + 1, 1 - slot)

    for k in range(TOP_K):
        pltpu.make_async_copy(ys_ref.at[pl.ds(0, MOVE_TM)], ybuf.at[slot, k], sem.at[slot]).wait()
    reps = D_MODEL // LANES
    w0 = jnp.concatenate([w0_ref[...]] * reps, axis=1)
    w1 = jnp.concatenate([w1_ref[...]] * reps, axis=1)
    o_ref[...] = h_ref[...] + w0 * ybuf[slot, 0] + w1 * ybuf[slot, 1]


def _combine(pos_flat, h1, w0b, w1b, ys):
    rows = h1.shape[0]
    tm = MOVE_TM
    return pl.pallas_call(
        _combine_kernel,
        grid_spec=pltpu.PrefetchScalarGridSpec(
            num_scalar_prefetch=1,
            grid=(rows // tm,),
            in_specs=[
                pl.BlockSpec((tm, D_MODEL), lambda i, pos: (i, 0)),
                pl.BlockSpec((tm, LANES), lambda i, pos: (i, 0)),
                pl.BlockSpec((tm, LANES), lambda i, pos: (i, 0)),
                pl.BlockSpec(memory_space=pl.ANY),
            ],
            out_specs=pl.BlockSpec((tm, D_MODEL), lambda i, pos: (i, 0)),
            scratch_shapes=[pltpu.VMEM((2, TOP_K, tm, D_MODEL), F32), pltpu.SemaphoreType.DMA((2,))],
        ),
        out_shape=jax.ShapeDtypeStruct((rows, D_MODEL), F32),
        compiler_params=_cparams(("arbitrary",)),
        name="combine",
    )(pos_flat, h1, w0b, w1b, ys)


def _work_items(counts, n_assign):
    ends = jnp.cumsum(counts)
    starts = (ends - counts).astype(jnp.int32)
    n_tiles = n_assign // FFN_TM
    tile_starts = jnp.arange(n_tiles, dtype=jnp.int32) * FFN_TM
    seg_lo = jnp.sort(jnp.concatenate([tile_starts, starts]))
    seg_hi = jnp.concatenate([seg_lo[1:], jnp.array([n_assign], jnp.int32)])
    item_tile = jnp.minimum(seg_lo // FFN_TM, n_tiles - 1).astype(jnp.int32)
    item_exp = jnp.minimum(jnp.sum((ends[None, :] <= seg_lo[:, None]).astype(jnp.int32), axis=1), N_EXPERTS - 1)
    item_exp = item_exp.astype(jnp.int32)
    item_first = (seg_lo == item_tile * FFN_TM).astype(jnp.int32)
    n_items = seg_lo.shape[0]
    item_new = jnp.concatenate([jnp.ones((1,), jnp.int32), (item_exp[1:] != item_exp[:-1]).astype(jnp.int32)])
    ordinal = jnp.cumsum(item_new) - 1
    run_exp = jnp.full((n_items + 1,), -1, jnp.int32).at[ordinal].set(item_exp)
    item_next = run_exp[ordinal + 1]
    item_slot = (ordinal % 2).astype(jnp.int32)
    items = (item_tile, item_exp, seg_lo, seg_hi, item_first, item_new, item_slot, item_next)
    return starts, items


def kernel(x, meta_tokens, norm1_g, w_in, b_f, q_gain, k_gain, fox_out_gain, sb_out_gain, w_out, norm2_g,
           w_coarse, b_coarse, w_fine, b_fine, w1, w3, w2):
    assert norm1_g.shape[0] == 1, "single-layer block"
    n_batch, seq, _ = x.shape
    n_tok = n_batch * seq
    n_assign = n_tok * TOP_K
    scale = HEAD_DIM ** -0.5
    x2d = x.reshape(n_tok, D_MODEL)

    w_in_t = w_in[0].T
    w_in_bf = _cast_bf16(w_in_t, D_PROJ)
    w_out_bf = _cast_bf16(w_out[0], 2 * D_GROUP)
    wft_bf = jnp.pad(w_in_t[D_PROJ:], ((0, LANES - N_HEADS), (0, 0))).astype(BF16)
    bf_pad = jnp.pad(b_f[0], (0, LANES - N_HEADS)).reshape(1, LANES)
    ones = jnp.ones((HEAD_DIM,), F32)
    gains = jnp.stack([q_gain[0] * scale, k_gain[0], ones, ones, ones * scale, ones, ones]).reshape(
        N_PROJ_GROUPS, 1, HEAD_DIM)
    meta_pad = jnp.pad(meta_tokens.astype(F32), ((0, BLK - N_META), (0, 0)))
    n1 = norm1_g[0].reshape(1, D_MODEL)

    proj, lf = _inproj(x2d, n1, w_in_bf, wft_bf, bf_pad, gains, tm=1024)
    proj_m, lf_m = _inproj(meta_pad, n1, w_in_bf, wft_bf, bf_pad, gains, tm=BLK)
    qx, kx, kxm = _cumgate(lf_m, lf, n_batch, seq)
    qk_bound = 1.02 * HEAD_DIM * scale * jnp.max(jnp.abs(q_gain[0])) * jnp.max(jnp.abs(k_gain[0]))
    bound = (2.0 * qk_bound).reshape(1).astype(F32)
    oa = _fox_attention(bound, proj, proj_m, qx, kx, kxm, fox_out_gain[0].reshape(N_HEADS, 1, HEAD_DIM),
                        n_batch, seq)
    ob = _sb_attention(proj, proj_m, sb_out_gain[0].reshape(N_HEADS, 1, HEAD_DIM), n_batch, seq)
    h1 = _outproj(oa, ob, w_out_bf, x2d)

    wr = jnp.pad(jnp.concatenate([w_fine[0], w_coarse[0]], axis=1),
                 ((0, 0), (0, LANES - N_GROUPS - N_EXPERTS)))
    wr_hi = wr.astype(BF16)
    wr_lo = (wr - wr_hi.astype(F32)).astype(BF16)
    br = jnp.pad(jnp.concatenate([b_fine[0], b_coarse[0]]), (0, LANES - N_GROUPS - N_EXPERTS)).reshape(1, LANES)
    u2, route, w0b, w1b, cnt = _router(h1, norm2_g[0].reshape(1, D_MODEL), wr_hi, wr_lo, br)
    starts, items = _work_items(cnt[0, :N_EXPERTS], n_assign)
    pos = (starts[route[:, 0:TOP_K]] + route[:, TOP_K:2 * TOP_K]).reshape(-1)
    xs = _scatter_rows(pos, u2)
    ys = _grouped_ffn(items, xs, w1[0], w3[0], w2[0])
    out = _combine(pos, h1, w0b, w1b, ys)
    return out.reshape(n_batch, seq, D_MODEL)
```

```python
import functools

import jax
import jax.numpy as jnp
from jax import lax
from jax.experimental import pallas as pl
from jax.experimental.pallas import tpu as pltpu

F32 = jnp.float32
BF16 = jnp.bfloat16

D_MODEL = 2048
N_META = 16
HEAD_DIM = 128
N_HEADS = 8
D_GROUP = N_HEADS * HEAD_DIM
N_PROJ_GROUPS = 7
D_PROJ = N_PROJ_GROUPS * D_GROUP
N_GROUPS = 8
EXPERTS_PER_GROUP = 8
N_EXPERTS = 64
TOP_K = 2
D_EXPERT = 512
EPS = 1e-6
LANES = 128
N_CHUNK = D_MODEL // LANES
MXU_HEADS = 2
BLK = 128
SKIP_LOG = 88.0
VMEM_LIMIT = 56 * 1024 * 1024


def _cparams(sem, vmem=VMEM_LIMIT):
    return pltpu.CompilerParams(dimension_semantics=sem, vmem_limit_bytes=vmem)


def _log_sigmoid(x):
    return jnp.minimum(x, 0.0) - jnp.log(1.0 + jnp.exp(-jnp.abs(x)))


def _split3(x):
    hi = x.astype(BF16)
    r1 = x - hi.astype(F32)
    mid = r1.astype(BF16)
    lo = (r1 - mid.astype(F32)).astype(BF16)
    return hi, mid, lo


def _dot_nt(a, b):
    return lax.dot_general(a, b, (((1,), (1,)), ((), ())), preferred_element_type=F32)


def _cast_kernel(x_ref, o_ref):
    o_ref[...] = x_ref[...].astype(o_ref.dtype)


def _cast_bf16(w, n_rows, tr=512):
    n_cols = w.shape[1]
    return pl.pallas_call(
        _cast_kernel,
        grid=(n_rows // tr,),
        in_specs=[pl.BlockSpec((tr, n_cols), lambda i: (i, 0))],
        out_specs=pl.BlockSpec((tr, n_cols), lambda i: (i, 0)),
        out_shape=jax.ShapeDtypeStruct((n_rows, n_cols), BF16),
        compiler_params=_cparams(("parallel",)),
        name="cast_bf16",
    )(w)


def _inproj_kernel(x_ref, g_ref, wt_ref, wft_ref, bf_ref, gain_ref, o_ref, lf_ref, u_ref, *, heads_per_tile):
    j = pl.program_id(1)

    @pl.when(j == 0)
    def _():
        x = x_ref[...]
        ms = jnp.mean(x * x, axis=-1, keepdims=True)
        u = (x * lax.rsqrt(ms + EPS) * g_ref[...]).astype(BF16)
        u_ref[...] = u
        f = _dot_nt(u, wft_ref[...]) + bf_ref[...]
        lf_ref[...] = _log_sigmoid(f)

    gain = gain_ref[0, 0:1, :]
    normed = gain_ref[0, 1:2, :] > 0.5
    u = u_ref[...]
    for s in range(heads_per_tile // MXU_HEADS):
        cols = pl.ds(s * MXU_HEADS * HEAD_DIM, MXU_HEADS * HEAD_DIM)
        acc = _dot_nt(u, wt_ref[cols, :])
        for hh in range(MXU_HEADS):
            y = acc[:, hh * HEAD_DIM:(hh + 1) * HEAD_DIM]
            ms = jnp.mean(y * y, axis=-1, keepdims=True)
            scale = jnp.where(normed, lax.rsqrt(ms + EPS), 1.0)
            o_ref[s * MXU_HEADS + hh] = (y * scale * gain).astype(BF16)


def _inproj(x2d, norm_g, wt_bf, wft_bf, bf_pad, gains, tm, tn=D_GROUP):
    rows = x2d.shape[0]
    hpt = tn // HEAD_DIM
    kern = functools.partial(_inproj_kernel, heads_per_tile=hpt)
    return pl.pallas_call(
        kern,
        grid=(rows // tm, D_PROJ // tn),
        in_specs=[
            pl.BlockSpec((tm, D_MODEL), lambda i, j: (i, 0)),
            pl.BlockSpec((1, D_MODEL), lambda i, j: (0, 0)),
            pl.BlockSpec((tn, D_MODEL), lambda i, j: (j, 0)),
            pl.BlockSpec((LANES, D_MODEL), lambda i, j: (0, 0)),
            pl.BlockSpec((1, LANES), lambda i, j: (0, 0)),
            pl.BlockSpec((1, 2, HEAD_DIM), lambda i, j: (j // (N_HEADS // hpt), 0, 0)),
        ],
        out_specs=[
            pl.BlockSpec((hpt, tm, HEAD_DIM), lambda i, j: (j, i, 0)),
            pl.BlockSpec((tm, LANES), lambda i, j: (i, 0)),
        ],
        out_shape=[
            jax.ShapeDtypeStruct((D_PROJ // HEAD_DIM, rows, HEAD_DIM), BF16),
            jax.ShapeDtypeStruct((rows, LANES), F32),
        ],
        scratch_shapes=[pltpu.VMEM((tm, D_MODEL), BF16)],
        compiler_params=_cparams(("parallel", "arbitrary")),
        name="inproj",
    )(x2d, norm_g, wt_bf, wft_bf, bf_pad, gains)


def _cumgate_kernel(lfm_ref, lf_ref, qx_ref, kx_ref, kxm_ref, *, n_blk):
    row = lax.broadcasted_iota(jnp.int32, (BLK, BLK), 0)
    col = lax.broadcasted_iota(jnp.int32, (BLK, BLK), 1)
    tri = (col <= row).astype(BF16)
    one = jnp.ones((BLK, LANES), F32)
    zero = jnp.zeros((BLK, LANES), F32)

    def prefix(x):
        hi, mid, lo = _split3(x)
        return (jnp.dot(tri, hi, preferred_element_type=F32)
                + jnp.dot(tri, mid, preferred_element_type=F32)
                + jnp.dot(tri, lo, preferred_element_type=F32))

    def ext(cum, h):
        c = jnp.broadcast_to(cum[:, h:h + 1], (BLK, LANES))
        hi, mid, lo = (t.astype(F32) for t in _split3(c))
        qx = jnp.where(col == 0, hi, jnp.where(col == 1, mid, jnp.where(col == 2, lo,
                       jnp.where(col < 6, one, zero))))
        kx = jnp.where(col < 3, one, jnp.where(col == 3, -hi, jnp.where(col == 4, -mid,
                       jnp.where(col == 5, -lo, zero))))
        return qx.astype(BF16), kx.astype(BF16)

    lfm = jnp.where(row < N_META, lfm_ref[...], 0.0)
    cum_m = prefix(lfm)
    for h in range(N_HEADS):
        _, kx = ext(cum_m, h)
        kxm_ref[h] = kx
    carry = cum_m[BLK - 1:BLK, :]
    for b in range(n_blk):
        cum = prefix(lf_ref[b * BLK:(b + 1) * BLK, :]) + carry
        carry = cum[BLK - 1:BLK, :]
        for h in range(N_HEADS):
            qx, kx = ext(cum, h)
            qx_ref[h, b * BLK:(b + 1) * BLK, :] = qx
            kx_ref[h, b * BLK:(b + 1) * BLK, :] = kx


def _cumgate(lf_meta, lf_real, n_batch, seq):
    kern = functools.partial(_cumgate_kernel, n_blk=seq // BLK)
    return pl.pallas_call(
        kern,
        grid=(n_batch,),
        in_specs=[
            pl.BlockSpec((BLK, LANES), lambda b: (0, 0)),
            pl.BlockSpec((seq, LANES), lambda b: (b, 0)),
        ],
        out_specs=[
            pl.BlockSpec((N_HEADS, seq, LANES), lambda b: (0, b, 0)),
            pl.BlockSpec((N_HEADS, seq, LANES), lambda b: (0, b, 0)),
            pl.BlockSpec((N_HEADS, BLK, LANES), lambda b: (0, 0, 0)),
        ],
        out_shape=[
            jax.ShapeDtypeStruct((N_HEADS, n_batch * seq, LANES), BF16),
            jax.ShapeDtypeStruct((N_HEADS, n_batch * seq, LANES), BF16),
            jax.ShapeDtypeStruct((N_HEADS, BLK, LANES), BF16),
        ],
        compiler_params=_cparams(("arbitrary",)),
        name="cumgate",
    )(lf_meta, lf_real)


def _bdot_nt(a, b):
    return lax.dot_general(a, b, (((2,), (2,)), ((0,), (0,))), preferred_element_type=F32)


def _bdot_nn(a, b):
    return lax.dot_general(a, b, (((2,), (1,)), ((0,), (0,))), preferred_element_type=F32)


def _fox_kernel(bound_ref, q_ref, qx_ref, k_ref, kx_ref, v_ref, g_ref, km_ref, kxm_ref, vm_ref, gain_ref,
                o_ref, m_scr, l_scr, acc_scr):
    i = pl.program_id(1)
    row = lax.broadcasted_iota(jnp.int32, (N_HEADS, BLK, BLK), 1)
    col = lax.broadcasted_iota(jnp.int32, (N_HEADS, BLK, BLK), 2)

    def sweep(kb, kxb, vb, mask, first):
        qa = jnp.concatenate([q_ref[...], qx_ref[...]], axis=2)
        ka = jnp.concatenate([kb, kxb], axis=2)
        s = _bdot_nt(qa, ka)
        if mask is not None:
            s = jnp.where(mask, s, -jnp.inf)
        m_cur = jnp.max(s, axis=2, keepdims=True)
        v1 = jnp.concatenate([vb, jnp.ones_like(vb)], axis=2)
        if first:
            m_new = jnp.broadcast_to(m_cur, s.shape)
            pv = _bdot_nn(jnp.exp(s - m_new).astype(BF16), v1)
            l_scr[...] = pv[:, :, BLK:]
            acc_scr[...] = pv[:, :, :BLK]
        else:
            m_prev = m_scr[...]
            m_new = jnp.maximum(m_prev, m_cur)
            alpha = jnp.exp(m_prev - m_new)
            pv = _bdot_nn(jnp.exp(s - m_new).astype(BF16), v1)
            l_scr[...] = alpha * l_scr[...] + pv[:, :, BLK:]
            acc_scr[...] = alpha * acc_scr[...] + pv[:, :, :BLK]
        m_scr[...] = m_new
        return jnp.max(s[:, :, 0:1] - m_new[:, :, 0:1])

    def real_block(j):
        start = pl.multiple_of(j * BLK, BLK)
        return (k_ref[:, pl.ds(start, BLK), :], kx_ref[:, pl.ds(start, BLK), :], v_ref[:, pl.ds(start, BLK), :])

    bound = bound_ref[0]
    gap0 = sweep(*real_block(i), col <= row, True)

    def cond(c):
        j, done = c
        return jnp.logical_and(j >= 0, done == 0)

    def body(c):
        j, _ = c
        gap = sweep(*real_block(j), None, False)
        return j - 1, (gap + bound < -SKIP_LOG).astype(jnp.int32)

    _, done = lax.while_loop(cond, body, (i - 1, (gap0 + bound < -SKIP_LOG).astype(jnp.int32)))

    @pl.when(done == 0)
    def _():
        sweep(km_ref[...], kxm_ref[...], vm_ref[...], col < N_META, False)

    o = acc_scr[...] / l_scr[...]
    ms = jnp.mean(o * o, axis=-1, keepdims=True)
    gate = 1.0 / (1.0 + jnp.exp(-g_ref[...].astype(F32)))
    o_ref[...] = (o * lax.rsqrt(ms + EPS) * gain_ref[...] * gate).astype(BF16)


def _sb_kernel(q_ref, k_ref, v_ref, km_ref, vm_ref, gain_ref, o_ref, carry_scr, acc_scr):
    i = pl.program_id(1)
    row = lax.broadcasted_iota(jnp.int32, (N_HEADS, BLK, BLK), 1)
    col = lax.broadcasted_iota(jnp.int32, (N_HEADS, BLK, BLK), 2)
    r2 = lax.broadcasted_iota(jnp.int32, (BLK, 2 * BLK), 0)
    c2 = lax.broadcasted_iota(jnp.int32, (BLK, 2 * BLK), 1)
    suffix = jnp.logical_or(c2 >= BLK, r2 > c2).astype(BF16)

    def sweep(kb, vb, mask, first):
        z = _bdot_nt(q_ref[...], kb)
        sp = jnp.maximum(z, 0.0) + jnp.log(1.0 + jnp.exp(-jnp.abs(z)))
        lk = -sp
        if mask is not None:
            lk = jnp.where(mask, lk, 0.0)
        hi = lk.astype(BF16)
        lo = (lk - hi.astype(F32)).astype(BF16)
        t = (jnp.dot(hi.reshape(N_HEADS * BLK, BLK), suffix, preferred_element_type=F32)
             + jnp.dot(lo.reshape(N_HEADS * BLK, BLK), suffix, preferred_element_type=F32))
        t = t.reshape(N_HEADS, BLK, 2 * BLK)
        later = t[:, :, :BLK]
        rowsum = t[:, :, BLK:]
        if not first:
            later = later + carry_scr[...]
        a = jnp.exp(z - sp + later)
        if mask is not None:
            a = jnp.where(mask, a, 0.0)
        pv = _bdot_nn(a.astype(BF16), vb)
        if first:
            acc_scr[...] = pv
            c_new = rowsum
        else:
            acc_scr[...] = acc_scr[...] + pv
            c_new = carry_scr[...] + rowsum
        carry_scr[...] = c_new
        return jnp.max(c_new[:, :, 0:1])

    def real_block(j):
        start = pl.multiple_of(j * BLK, BLK)
        return k_ref[:, pl.ds(start, BLK), :], v_ref[:, pl.ds(start, BLK), :]

    top0 = sweep(*real_block(i), col < row, True)

    def cond(c):
        j, done = c
        return jnp.logical_and(j >= 0, done == 0)

    def body(c):
        j, _ = c
        top = sweep(*real_block(j), None, False)
        return j - 1, (top < -SKIP_LOG).astype(jnp.int32)

    _, done = lax.while_loop(cond, body, (i - 1, (top0 < -SKIP_LOG).astype(jnp.int32)))

    @pl.when(done == 0)
    def _():
        sweep(km_ref[...], vm_ref[...], col < N_META, False)

    o = acc_scr[...]
    ms = jnp.mean(o * o, axis=-1, keepdims=True)
    o_ref[...] = (o * lax.rsqrt(ms + EPS) * gain_ref[...]).astype(BF16)


def _head_spec_q(group, nq):
    return pl.BlockSpec((N_HEADS, BLK, HEAD_DIM), lambda b, i: (group, b * nq + i, 0))


def _head_spec_kv(group, seq):
    return pl.BlockSpec((N_HEADS, seq, HEAD_DIM), lambda b, i: (group, b, 0))


def _head_spec_meta(group):
    return pl.BlockSpec((N_HEADS, BLK, HEAD_DIM), lambda b, i: (group, 0, 0))


def _fox_attention(bound, proj, proj_m, qx, kx, kxm, out_gain, n_batch, seq):
    nq = seq // BLK
    rows = n_batch * seq
    return pl.pallas_call(
        _fox_kernel,
        grid=(n_batch, nq),
        in_specs=[
            pl.BlockSpec(memory_space=pltpu.SMEM),
            _head_spec_q(0, nq),
            _head_spec_q(0, nq),
            _head_spec_kv(1, seq),
            _head_spec_kv(0, seq),
            _head_spec_kv(2, seq),
            _head_spec_q(3, nq),
            _head_spec_meta(1),
            _head_spec_meta(0),
            _head_spec_meta(2),
            pl.BlockSpec((N_HEADS, 1, HEAD_DIM), lambda b, i: (0, 0, 0)),
        ],
        out_specs=pl.BlockSpec((N_HEADS, BLK, HEAD_DIM), lambda b, i: (0, b * nq + i, 0)),
        out_shape=jax.ShapeDtypeStruct((N_HEADS, rows, HEAD_DIM), BF16),
        scratch_shapes=[pltpu.VMEM((N_HEADS, BLK, LANES), F32)] * 3,
        compiler_params=_cparams(("parallel", "arbitrary")),
        name="fox_attention",
    )(bound, proj, qx, proj, kx, proj, proj, proj_m, kxm, proj_m, out_gain)


def _sb_attention(proj, proj_m, out_gain, n_batch, seq):
    nq = seq // BLK
    rows = n_batch * seq
    return pl.pallas_call(
        _sb_kernel,
        grid=(n_batch, nq),
        in_specs=[
            _head_spec_q(4, nq),
            _head_spec_kv(5, seq),
            _head_spec_kv(6, seq),
            _head_spec_meta(5),
            _head_spec_meta(6),
            pl.BlockSpec((N_HEADS, 1, HEAD_DIM), lambda b, i: (0, 0, 0)),
        ],
        out_specs=pl.BlockSpec((N_HEADS, BLK, HEAD_DIM), lambda b, i: (0, b * nq + i, 0)),
        out_shape=jax.ShapeDtypeStruct((N_HEADS, rows, HEAD_DIM), BF16),
        scratch_shapes=[pltpu.VMEM((N_HEADS, BLK, LANES), F32)] * 2,
        compiler_params=_cparams(("parallel", "arbitrary")),
        name="sb_attention",
    )(proj, proj, proj, proj_m, proj_m, out_gain)


ROUTE_TM = 256


def _router_kernel(oa_ref, ob_ref, wo_ref, x_ref, g_ref, whi_ref, wlo_ref, b_ref,
                   h_ref, u_ref, route_ref, w0_ref, w1_ref, cnt_ref, carry_scr):
    step = pl.program_id(0)
    lhs = jnp.concatenate([oa_ref[h] for h in range(N_HEADS)] + [ob_ref[h] for h in range(N_HEADS)], axis=1)
    x = x_ref[...] + jnp.dot(lhs, wo_ref[...], preferred_element_type=F32)
    h_ref[...] = x
    tm = x.shape[0]
    ms = jnp.mean(x * x, axis=-1, keepdims=True)
    u = x * lax.rsqrt(ms + EPS) * g_ref[...]
    u_ref[...] = u
    uhi = u.astype(BF16)
    ulo = (u - uhi.astype(F32)).astype(BF16)
    logits = (jnp.dot(uhi, whi_ref[...], preferred_element_type=F32)
              + jnp.dot(uhi, wlo_ref[...], preferred_element_type=F32)
              + jnp.dot(ulo, whi_ref[...], preferred_element_type=F32)) + b_ref[...]
    lane_i = lax.broadcasted_iota(jnp.int32, (tm, LANES), 1)
    lane = lane_i.astype(F32)
    big = float(4 * LANES)
    neg = -jnp.inf
    c = jnp.where(jnp.logical_and(lane_i >= N_EXPERTS, lane_i < N_EXPERTS + N_GROUPS), logits, neg)
    cmax = jnp.max(c, axis=1, keepdims=True)
    g_sel = jnp.min(jnp.where(c == cmax, lane, big), axis=1, keepdims=True) - N_EXPERTS
    g_gate = 1.0 / jnp.sum(jnp.exp(c - cmax), axis=1, keepdims=True)
    lo = g_sel * EXPERTS_PER_GROUP
    in_group = jnp.logical_and(lane >= lo, lane < lo + EXPERTS_PER_GROUP)
    f = jnp.where(in_group, logits, neg)
    t1 = jnp.max(f, axis=1, keepdims=True)
    i1 = jnp.min(jnp.where(f == t1, lane, big), axis=1, keepdims=True)
    f2 = jnp.where(lane == i1, neg, f)
    t2 = jnp.max(f2, axis=1, keepdims=True)
    i2 = jnp.min(jnp.where(f2 == t2, lane, big), axis=1, keepdims=True)
    d = jnp.exp(t2 - t1)
    w_first = g_gate / (1.0 + d)
    w0_ref[...] = jnp.broadcast_to(w_first, (tm, LANES))
    w1_ref[...] = jnp.broadcast_to(w_first * d, (tm, LANES))

    @pl.when(step == 0)
    def _():
        carry_scr[...] = jnp.zeros_like(carry_scr)

    oh0 = (lane == i1).astype(F32)
    oh1 = (lane == i2).astype(F32)
    oh = oh0 + oh1
    r = lax.broadcasted_iota(jnp.int32, (tm, tm), 0)
    cc = lax.broadcasted_iota(jnp.int32, (tm, tm), 1)
    before = (cc < r).astype(BF16)
    seen = jnp.dot(before, oh.astype(BF16), preferred_element_type=F32) + carry_scr[0:1, :]
    rank0 = jnp.sum(oh0 * seen, axis=1, keepdims=True)
    rank1 = jnp.sum(oh1 * seen, axis=1, keepdims=True)
    total = seen[tm - 1:tm, :] + oh[tm - 1:tm, :]
    carry_scr[...] = jnp.broadcast_to(total, carry_scr.shape)
    cnt_ref[...] = jnp.broadcast_to(total, cnt_ref.shape).astype(jnp.int32)
    vals = jnp.where(lane_i == 0, i1, jnp.where(lane_i == 1, i2, jnp.where(lane_i == 2, rank0,
                     jnp.where(lane_i == 3, rank1, 0.0))))
    route_ref[...] = vals.astype(jnp.int32)


def _outproj_router(oa, ob, wo_bf, x2d, norm_g, whi, wlo, b_pad):
    rows = x2d.shape[0]
    tm = ROUTE_TM
    return pl.pallas_call(
        _router_kernel,
        grid=(rows // tm,),
        in_specs=[
            pl.BlockSpec((N_HEADS, tm, HEAD_DIM), lambda i: (0, i, 0)),
            pl.BlockSpec((N_HEADS, tm, HEAD_DIM), lambda i: (0, i, 0)),
            pl.BlockSpec((2 * D_GROUP, D_MODEL), lambda i: (0, 0)),
            pl.BlockSpec((tm, D_MODEL), lambda i: (i, 0)),
            pl.BlockSpec((1, D_MODEL), lambda i: (0, 0)),
            pl.BlockSpec((D_MODEL, LANES), lambda i: (0, 0)),
            pl.BlockSpec((D_MODEL, LANES), lambda i: (0, 0)),
            pl.BlockSpec((1, LANES), lambda i: (0, 0)),
        ],
        out_specs=[
            pl.BlockSpec((tm, D_MODEL), lambda i: (i, 0)),
            pl.BlockSpec((tm, D_MODEL), lambda i: (i, 0)),
            pl.BlockSpec((tm, LANES), lambda i: (i, 0)),
            pl.BlockSpec((tm, LANES), lambda i: (i, 0)),
            pl.BlockSpec((tm, LANES), lambda i: (i, 0)),
            pl.BlockSpec((8, LANES), lambda i: (0, 0)),
        ],
        out_shape=[
            jax.ShapeDtypeStruct((rows, D_MODEL), F32),
            jax.ShapeDtypeStruct((rows, D_MODEL), F32),
            jax.ShapeDtypeStruct((rows, LANES), jnp.int32),
            jax.ShapeDtypeStruct((rows, LANES), F32),
            jax.ShapeDtypeStruct((rows, LANES), F32),
            jax.ShapeDtypeStruct((8, LANES), jnp.int32),
        ],
        scratch_shapes=[pltpu.VMEM((8, LANES), F32)],
        compiler_params=_cparams(("arbitrary",)),
        name="outproj_router",
    )(oa, ob, wo_bf, x2d, norm_g, whi, wlo, b_pad)


MOVE_TM = 256
MOVE_UNROLL = 8


def _scatter_kernel(pos_ref, u_ref, xs_ref, sem):
    base = pl.program_id(0) * MOVE_TM

    def body(t, carry):
        for k in range(TOP_K):
            dst = pos_ref[(base + t) * TOP_K + k]
            pltpu.make_async_copy(u_ref.at[pl.ds(t, 1)], xs_ref.at[pl.ds(dst, 1)], sem).start()
        return carry

    lax.fori_loop(0, MOVE_TM, body, 0, unroll=MOVE_UNROLL)
    for _ in range(TOP_K):
        pltpu.make_async_copy(u_ref, xs_ref.at[pl.ds(0, MOVE_TM)], sem).wait()


def _scatter_rows(pos_flat, u2):
    n_tok = u2.shape[0]
    return pl.pallas_call(
        _scatter_kernel,
        grid_spec=pltpu.PrefetchScalarGridSpec(
            num_scalar_prefetch=1,
            grid=(n_tok // MOVE_TM,),
            in_specs=[pl.BlockSpec((MOVE_TM, D_MODEL), lambda i, pos: (i, 0))],
            out_specs=pl.BlockSpec(memory_space=pl.ANY),
            scratch_shapes=[pltpu.SemaphoreType.DMA(())],
        ),
        out_shape=jax.ShapeDtypeStruct((n_tok * TOP_K, D_MODEL), u2.dtype),
        compiler_params=_cparams(("arbitrary",)),
        name="scatter_rows",
    )(pos_flat, u2)


FFN_TM = 128


def _ffn_kernel(tile_ref, exp_ref, lo_ref, hi_ref, first_ref, new_ref, slot_ref, next_ref,
                x_ref, w1_hbm, w3_hbm, w2_hbm, o_ref,
                xb_scr, w1_buf, w3_buf, w2_buf, w1_bf, w3_bf, w2_bf, sem):
    it = pl.program_id(0)
    lo = lo_ref[it]
    hi = hi_ref[it]
    slot = slot_ref[it]

    def weight_copies(expert, s):
        return (pltpu.make_async_copy(w1_hbm.at[expert], w1_buf.at[s], sem.at[s, 0]),
                pltpu.make_async_copy(w3_hbm.at[expert], w3_buf.at[s], sem.at[s, 1]),
                pltpu.make_async_copy(w2_hbm.at[expert], w2_buf.at[s], sem.at[s, 2]))

    @pl.when(it == 0)
    def _():
        for cp in weight_copies(exp_ref[0], slot):
            cp.start()

    @pl.when(new_ref[it] == 1)
    def _():
        for cp in weight_copies(exp_ref[it], slot):
            cp.wait()

        @pl.when(next_ref[it] >= 0)
        def _():
            for cp in weight_copies(next_ref[it], 1 - slot):
                cp.start()

        w1_bf[...] = w1_buf[slot].astype(BF16)
        w3_bf[...] = w3_buf[slot].astype(BF16)
        w2_bf[...] = w2_buf[slot].astype(BF16)

    @pl.when(first_ref[it] == 1)
    def _():
        xb_scr[...] = x_ref[...].astype(BF16)
        o_ref[...] = jnp.zeros_like(o_ref)

    @pl.when(hi > lo)
    def _():
        x = xb_scr[...]
        a = jnp.dot(x, w1_bf[...], preferred_element_type=F32)
        b = jnp.dot(x, w3_bf[...], preferred_element_type=F32)
        mid = a / (1.0 + jnp.exp(-a)) * b
        rows = tile_ref[it] * FFN_TM + lax.broadcasted_iota(jnp.int32, mid.shape, 0)
        mid = jnp.where(jnp.logical_and(rows >= lo, rows < hi), mid, 0.0).astype(BF16)
        o_ref[...] += jnp.dot(mid, w2_bf[...], preferred_element_type=F32)


def _grouped_ffn(items, xs, w1, w3, w2):
    n_items = items[0].shape[0]
    n_rows = xs.shape[0]

    def tile_map(i, t, *_):
        return (t[i], 0)

    return pl.pallas_call(
        _ffn_kernel,
        grid_spec=pltpu.PrefetchScalarGridSpec(
            num_scalar_prefetch=len(items),
            grid=(n_items,),
            in_specs=[
                pl.BlockSpec((FFN_TM, D_MODEL), tile_map),
                pl.BlockSpec(memory_space=pl.ANY),
                pl.BlockSpec(memory_space=pl.ANY),
                pl.BlockSpec(memory_space=pl.ANY),
            ],
            out_specs=pl.BlockSpec((FFN_TM, D_MODEL), tile_map),
            scratch_shapes=[
                pltpu.VMEM((FFN_TM, D_MODEL), BF16),
                pltpu.VMEM((2, D_MODEL, D_EXPERT), F32),
                pltpu.VMEM((2, D_MODEL, D_EXPERT), F32),
                pltpu.VMEM((2, D_EXPERT, D_MODEL), F32),
                pltpu.VMEM((D_MODEL, D_EXPERT), BF16),
                pltpu.VMEM((D_MODEL, D_EXPERT), BF16),
                pltpu.VMEM((D_EXPERT, D_MODEL), BF16),
                pltpu.SemaphoreType.DMA((2, 3)),
            ],
        ),
        out_shape=jax.ShapeDtypeStruct((n_rows, D_MODEL), F32),
        compiler_params=_cparams(("arbitrary",)),
        name="grouped_ffn",
    )(*items, xs, w1, w3, w2)


def _combine_kernel(pos_ref, h_ref, w0_ref, w1_ref, ys_ref, o_ref, ybuf, sem):
    step = pl.program_id(0)
    n_steps = pl.num_programs(0)

    def fetch(s, slot):
        def body(t, carry):
            for k in range(TOP_K):
                src = pos_ref[(s * MOVE_TM + t) * TOP_K + k]
                pltpu.make_async_copy(ys_ref.at[pl.ds(src, 1)], ybuf.at[slot, k, pl.ds(t, 1)], sem.at[slot]).start()
            return carry
        lax.fori_loop(0, MOVE_TM, body, 0, unroll=MOVE_UNROLL)

    @pl.when(step == 0)
    def _():
        fetch(0, 0)

    slot = step % 2

    @pl.when(step + 1 < n_steps)
    def _():
        fetch(step + 1, 1 - slot)

    for k in range(TOP_K):
        pltpu.make_async_copy(ys_ref.at[pl.ds(0, MOVE_TM)], ybuf.at[slot, k], sem.at[slot]).wait()
    reps = D_MODEL // LANES
    w0 = jnp.concatenate([w0_ref[...]] * reps, axis=1)
    w1 = jnp.concatenate([w1_ref[...]] * reps, axis=1)
    o_ref[...] = h_ref[...] + w0 * ybuf[slot, 0] + w1 * ybuf[slot, 1]


def _combine(pos_flat, h1, w0b, w1b, ys):
    rows = h1.shape[0]
    tm = MOVE_TM
    return pl.pallas_call(
        _combine_kernel,
        grid_spec=pltpu.PrefetchScalarGridSpec(
            num_scalar_prefetch=1,
            grid=(rows // tm,),
            in_specs=[
                pl.BlockSpec((tm, D_MODEL), lambda i, pos: (i, 0)),
                pl.BlockSpec((tm, LANES), lambda i, pos: (i, 0)),
                pl.BlockSpec((tm, LANES), lambda i, pos: (i, 0)),
                pl.BlockSpec(memory_space=pl.ANY),
            ],
            out_specs=pl.BlockSpec((tm, D_MODEL), lambda i, pos: (i, 0)),
            scratch_shapes=[pltpu.VMEM((2, TOP_K, tm, D_MODEL), F32), pltpu.SemaphoreType.DMA((2,))],
        ),
        out_shape=jax.ShapeDtypeStruct((rows, D_MODEL), F32),
        compiler_params=_cparams(("arbitrary",)),
        name="combine",
    )(pos_flat, h1, w0b, w1b, ys)


def _work_items(counts, n_assign):
    ends = jnp.cumsum(counts)
    starts = (ends - counts).astype(jnp.int32)
    n_tiles = n_assign // FFN_TM
    tile_starts = jnp.arange(n_tiles, dtype=jnp.int32) * FFN_TM
    seg_lo = jnp.sort(jnp.concatenate([tile_starts, starts]))
    seg_hi = jnp.concatenate([seg_lo[1:], jnp.array([n_assign], jnp.int32)])
    item_tile = jnp.minimum(seg_lo // FFN_TM, n_tiles - 1).astype(jnp.int32)
    item_exp = jnp.minimum(jnp.sum((ends[None, :] <= seg_lo[:, None]).astype(jnp.int32), axis=1), N_EXPERTS - 1)
    item_exp = item_exp.astype(jnp.int32)
    item_first = (seg_lo == item_tile * FFN_TM).astype(jnp.int32)
    n_items = seg_lo.shape[0]
    item_new = jnp.concatenate([jnp.ones((1,), jnp.int32), (item_exp[1:] != item_exp[:-1]).astype(jnp.int32)])
    ordinal = jnp.cumsum(item_new) - 1
    run_exp = jnp.full((n_items + 1,), -1, jnp.int32).at[ordinal].set(item_exp)
    item_next = run_exp[ordinal + 1]
    item_slot = (ordinal % 2).astype(jnp.int32)
    items = (item_tile, item_exp, seg_lo, seg_hi, item_first, item_new, item_slot, item_next)
    return starts, items


def kernel(x, meta_tokens, norm1_g, w_in, b_f, q_gain, k_gain, fox_out_gain, sb_out_gain, w_out, norm2_g,
           w_coarse, b_coarse, w_fine, b_fine, w1, w3, w2):
    assert norm1_g.shape[0] == 1, "single-layer block"
    n_batch, seq, _ = x.shape
    n_tok = n_batch * seq
    n_assign = n_tok * TOP_K
    scale = HEAD_DIM ** -0.5
    x2d = x.reshape(n_tok, D_MODEL)

    w_in_t = w_in[0].T
    w_in_bf = _cast_bf16(w_in_t, D_PROJ)
    w_out_bf = _cast_bf16(w_out[0], 2 * D_GROUP)
    wft_bf = jnp.pad(w_in_t[D_PROJ:], ((0, LANES - N_HEADS), (0, 0))).astype(BF16)
    bf_pad = jnp.pad(b_f[0], (0, LANES - N_HEADS)).reshape(1, LANES)
    ones = jnp.ones((HEAD_DIM,), F32)
    zeros = jnp.zeros((HEAD_DIM,), F32)
    gains = jnp.stack([jnp.stack([q_gain[0] * scale, ones]), jnp.stack([k_gain[0], ones]),
                       jnp.stack([ones, zeros]), jnp.stack([ones, zeros]), jnp.stack([ones * scale, zeros]),
                       jnp.stack([ones, zeros]), jnp.stack([ones, zeros])])
    meta_pad = jnp.pad(meta_tokens.astype(F32), ((0, BLK - N_META), (0, 0)))
    n1 = norm1_g[0].reshape(1, D_MODEL)

    proj, lf = _inproj(x2d, n1, w_in_bf, wft_bf, bf_pad, gains, tm=1024)
    proj_m, lf_m = _inproj(meta_pad, n1, w_in_bf, wft_bf, bf_pad, gains, tm=BLK)
    qx, kx, kxm = _cumgate(lf_m, lf, n_batch, seq)
    qk_bound = 1.02 * HEAD_DIM * scale * jnp.max(jnp.abs(q_gain[0])) * jnp.max(jnp.abs(k_gain[0]))
    bound = (2.0 * qk_bound).reshape(1).astype(F32)
    oa = _fox_attention(bound, proj, proj_m, qx, kx, kxm, fox_out_gain[0].reshape(N_HEADS, 1, HEAD_DIM),
                        n_batch, seq)
    ob = _sb_attention(proj, proj_m, sb_out_gain[0].reshape(N_HEADS, 1, HEAD_DIM), n_batch, seq)

    wr = jnp.pad(jnp.concatenate([w_fine[0], w_coarse[0]], axis=1),
                 ((0, 0), (0, LANES - N_GROUPS - N_EXPERTS)))
    wr_hi = wr.astype(BF16)
    wr_lo = (wr - wr_hi.astype(F32)).astype(BF16)
    br = jnp.pad(jnp.concatenate([b_fine[0], b_coarse[0]]), (0, LANES - N_GROUPS - N_EXPERTS)).reshape(1, LANES)
    h1, u2, route, w0b, w1b, cnt = _outproj_router(oa, ob, w_out_bf, x2d, norm2_g[0].reshape(1, D_MODEL),
                                                   wr_hi, wr_lo, br)
    starts, items = _work_items(cnt[0, :N_EXPERTS], n_assign)
    eid = route[:, 0:TOP_K]
    start_of = jnp.sum(jnp.where(eid[..., None] == jnp.arange(N_EXPERTS, dtype=jnp.int32), starts, 0), axis=-1)
    pos = (start_of + route[:, TOP_K:2 * TOP_K]).reshape(-1)
    xs = _scatter_rows(pos, u2)
    ys = _grouped_ffn(items, xs, w1[0], w3[0], w2[0])
    out = _combine(pos, h1, w0b, w1b, ys)
    return out.reshape(n_batch, seq, D_MODEL)
```

```python
import functools

import jax
import jax.numpy as jnp
from jax import lax
from jax.experimental import pallas as pl
from jax.experimental.pallas import tpu as pltpu

F32 = jnp.float32
BF16 = jnp.bfloat16

D_MODEL = 2048
N_META = 16
HEAD_DIM = 128
N_HEADS = 8
D_GROUP = N_HEADS * HEAD_DIM
N_PROJ_GROUPS = 7
D_PROJ = N_PROJ_GROUPS * D_GROUP
N_GROUPS = 8
EXPERTS_PER_GROUP = 8
N_EXPERTS = 64
TOP_K = 2
D_EXPERT = 512
EPS = 1e-6
LANES = 128
N_CHUNK = D_MODEL // LANES
MXU_HEADS = 2
BLK = 128
SKIP_LOG = 88.0
VMEM_LIMIT = 56 * 1024 * 1024


def _cparams(sem, vmem=VMEM_LIMIT):
    return pltpu.CompilerParams(dimension_semantics=sem, vmem_limit_bytes=vmem)


def _log_sigmoid(x):
    return jnp.minimum(x, 0.0) - jnp.log(1.0 + jnp.exp(-jnp.abs(x)))


def _split3(x):
    hi = x.astype(BF16)
    r1 = x - hi.astype(F32)
    mid = r1.astype(BF16)
    lo = (r1 - mid.astype(F32)).astype(BF16)
    return hi, mid, lo


def _dot_nt(a, b):
    return lax.dot_general(a, b, (((1,), (1,)), ((), ())), preferred_element_type=F32)


def _cast_kernel(x_ref, o_ref):
    o_ref[...] = x_ref[...].astype(o_ref.dtype)


def _cast_bf16(w, n_rows, tr=512):
    n_cols = w.shape[1]
    return pl.pallas_call(
        _cast_kernel,
        grid=(n_rows // tr,),
        in_specs=[pl.BlockSpec((tr, n_cols), lambda i: (i, 0))],
        out_specs=pl.BlockSpec((tr, n_cols), lambda i: (i, 0)),
        out_shape=jax.ShapeDtypeStruct((n_rows, n_cols), BF16),
        compiler_params=_cparams(("parallel",)),
        name="cast_bf16",
    )(w)


def _inproj_kernel(x_ref, g_ref, wt_ref, wft_ref, bf_ref, gain_ref, o_ref, lf_ref, u_ref, *, heads_per_tile):
    j = pl.program_id(1)

    @pl.when(j == 0)
    def _():
        x = x_ref[...]
        ms = jnp.mean(x * x, axis=-1, keepdims=True)
        u = (x * lax.rsqrt(ms + EPS) * g_ref[...]).astype(BF16)
        u_ref[...] = u
        f = _dot_nt(u, wft_ref[...]) + bf_ref[...]
        lf_ref[...] = _log_sigmoid(f)

    gain = gain_ref[0, 0:1, :]
    normed = gain_ref[0, 1:2, :] > 0.5
    u = u_ref[...]
    for s in range(heads_per_tile // MXU_HEADS):
        cols = pl.ds(s * MXU_HEADS * HEAD_DIM, MXU_HEADS * HEAD_DIM)
        acc = _dot_nt(u, wt_ref[cols, :])
        for hh in range(MXU_HEADS):
            y = acc[:, hh * HEAD_DIM:(hh + 1) * HEAD_DIM]
            ms = jnp.mean(y * y, axis=-1, keepdims=True)
            scale = jnp.where(normed, lax.rsqrt(ms + EPS), 1.0)
            o_ref[s * MXU_HEADS + hh] = (y * scale * gain).astype(BF16)


def _inproj(x2d, norm_g, wt_bf, wft_bf, bf_pad, gains, tm, tn=D_GROUP):
    rows = x2d.shape[0]
    hpt = tn // HEAD_DIM
    kern = functools.partial(_inproj_kernel, heads_per_tile=hpt)
    return pl.pallas_call(
        kern,
        grid=(rows // tm, D_PROJ // tn),
        in_specs=[
            pl.BlockSpec((tm, D_MODEL), lambda i, j: (i, 0)),
            pl.BlockSpec((1, D_MODEL), lambda i, j: (0, 0)),
            pl.BlockSpec((tn, D_MODEL), lambda i, j: (j, 0)),
            pl.BlockSpec((LANES, D_MODEL), lambda i, j: (0, 0)),
            pl.BlockSpec((1, LANES), lambda i, j: (0, 0)),
            pl.BlockSpec((1, 2, HEAD_DIM), lambda i, j: (j // (N_HEADS // hpt), 0, 0)),
        ],
        out_specs=[
            pl.BlockSpec((hpt, tm, HEAD_DIM), lambda i, j: (j, i, 0)),
            pl.BlockSpec((tm, LANES), lambda i, j: (i, 0)),
        ],
        out_shape=[
            jax.ShapeDtypeStruct((D_PROJ // HEAD_DIM, rows, HEAD_DIM), BF16),
            jax.ShapeDtypeStruct((rows, LANES), F32),
        ],
        scratch_shapes=[pltpu.VMEM((tm, D_MODEL), BF16)],
        compiler_params=_cparams(("parallel", "arbitrary")),
        name="inproj",
    )(x2d, norm_g, wt_bf, wft_bf, bf_pad, gains)


def _cumgate_kernel(lfm_ref, lf_ref, qx_ref, kx_ref, kxm_ref, *, n_blk):
    row = lax.broadcasted_iota(jnp.int32, (BLK, BLK), 0)
    col = lax.broadcasted_iota(jnp.int32, (BLK, BLK), 1)
    tri = (col <= row).astype(BF16)
    one = jnp.ones((BLK, LANES), F32)
    zero = jnp.zeros((BLK, LANES), F32)

    def prefix(x):
        hi, mid, lo = _split3(x)
        return (jnp.dot(tri, hi, preferred_element_type=F32)
                + jnp.dot(tri, mid, preferred_element_type=F32)
                + jnp.dot(tri, lo, preferred_element_type=F32))

    def ext(cum, h):
        c = jnp.broadcast_to(cum[:, h:h + 1], (BLK, LANES))
        hi, mid, lo = (t.astype(F32) for t in _split3(c))
        qx = jnp.where(col == 0, hi, jnp.where(col == 1, mid, jnp.where(col == 2, lo,
                       jnp.where(col < 6, one, zero))))
        kx = jnp.where(col < 3, one, jnp.where(col == 3, -hi, jnp.where(col == 4, -mid,
                       jnp.where(col == 5, -lo, zero))))
        return qx.astype(BF16), kx.astype(BF16)

    lfm = jnp.where(row < N_META, lfm_ref[...], 0.0)
    cum_m = prefix(lfm)
    for h in range(N_HEADS):
        _, kx = ext(cum_m, h)
        kxm_ref[h] = kx
    carry = cum_m[BLK - 1:BLK, :]
    for b in range(n_blk):
        cum = prefix(lf_ref[b * BLK:(b + 1) * BLK, :]) + carry
        carry = cum[BLK - 1:BLK, :]
        for h in range(N_HEADS):
            qx, kx = ext(cum, h)
            qx_ref[h, b * BLK:(b + 1) * BLK, :] = qx
            kx_ref[h, b * BLK:(b + 1) * BLK, :] = kx


def _cumgate(lf_meta, lf_real, n_batch, seq):
    kern = functools.partial(_cumgate_kernel, n_blk=seq // BLK)
    return pl.pallas_call(
        kern,
        grid=(n_batch,),
        in_specs=[
            pl.BlockSpec((BLK, LANES), lambda b: (0, 0)),
            pl.BlockSpec((seq, LANES), lambda b: (b, 0)),
        ],
        out_specs=[
            pl.BlockSpec((N_HEADS, seq, LANES), lambda b: (0, b, 0)),
            pl.BlockSpec((N_HEADS, seq, LANES), lambda b: (0, b, 0)),
            pl.BlockSpec((N_HEADS, BLK, LANES), lambda b: (0, 0, 0)),
        ],
        out_shape=[
            jax.ShapeDtypeStruct((N_HEADS, n_batch * seq, LANES), BF16),
            jax.ShapeDtypeStruct((N_HEADS, n_batch * seq, LANES), BF16),
            jax.ShapeDtypeStruct((N_HEADS, BLK, LANES), BF16),
        ],
        compiler_params=_cparams(("arbitrary",)),
        name="cumgate",
    )(lf_meta, lf_real)


def _bdot_nt(a, b):
    return lax.dot_general(a, b, (((2,), (2,)), ((0,), (0,))), preferred_element_type=F32)


def _bdot_nn(a, b):
    return lax.dot_general(a, b, (((2,), (1,)), ((0,), (0,))), preferred_element_type=F32)


def _fox_kernel(bound_ref, q_ref, qx_ref, k_ref, kx_ref, v_ref, g_ref, km_ref, kxm_ref, vm_ref, gain_ref,
                o_ref, m_scr, l_scr, acc_scr):
    i = pl.program_id(1)

    def sweep(kb, kxb, vb, mask, first):
        qa = jnp.concatenate([q_ref[...], qx_ref[...]], axis=2)
        ka = jnp.concatenate([kb, kxb], axis=2)
        s = _bdot_nt(qa, ka)
        if mask is not None:
            s = jnp.where(mask[None], s, -jnp.inf)
        m_cur = jnp.max(s, axis=2, keepdims=True)
        v1 = jnp.concatenate([vb, jnp.ones_like(vb)], axis=2)
        if first:
            m_col = m_cur
            pv = _bdot_nn(jnp.exp(s - m_col).astype(BF16), v1)
            l_scr[...] = pv[:, :, BLK:]
            acc_scr[...] = pv[:, :, :BLK]
            m_scr[...] = jnp.broadcast_to(m_col, m_scr.shape)
        else:
            m_prev = m_scr[...]
            m_new = jnp.maximum(m_prev, m_cur)
            m_col = m_new[:, :, 0:1]
            alpha = jnp.exp(m_prev - m_new)
            pv = _bdot_nn(jnp.exp(s - m_new).astype(BF16), v1)
            l_scr[...] = alpha * l_scr[...] + pv[:, :, BLK:]
            acc_scr[...] = alpha * acc_scr[...] + pv[:, :, :BLK]
            m_scr[...] = m_new
        return jnp.max(s[:, :, 0:1] - m_col)

    def real_block(start, width):
        start = pl.multiple_of(start, BLK)
        return (k_ref[:, pl.ds(start, width), :], kx_ref[:, pl.ds(start, width), :],
                v_ref[:, pl.ds(start, width), :])

    bound = bound_ref[0]
    first_start = jnp.maximum(i - 1, 0) * BLK
    row2 = lax.broadcasted_iota(jnp.int32, (BLK, 2 * BLK), 0)
    col2 = lax.broadcasted_iota(jnp.int32, (BLK, 2 * BLK), 1)
    causal = col2 + (first_start - i * BLK) <= row2
    gap0 = sweep(*real_block(first_start, 2 * BLK), causal, True)

    def cond(c):
        j, done = c
        return jnp.logical_and(j >= 0, done == 0)

    def body(c):
        j, _ = c
        gap = sweep(*real_block(j * BLK, BLK), None, False)
        return j - 1, (gap + bound < -SKIP_LOG).astype(jnp.int32)

    _, done = lax.while_loop(cond, body, (i - 2, (gap0 + bound < -SKIP_LOG).astype(jnp.int32)))

    @pl.when(done == 0)
    def _():
        col = lax.broadcasted_iota(jnp.int32, (BLK, BLK), 1)
        sweep(km_ref[...], kxm_ref[...], vm_ref[...], col < N_META, False)

    o = acc_scr[...] / l_scr[...]
    ms = jnp.mean(o * o, axis=-1, keepdims=True)
    gate = 1.0 / (1.0 + jnp.exp(-g_ref[...].astype(F32)))
    o_ref[...] = (o * lax.rsqrt(ms + EPS) * gain_ref[...] * gate).astype(BF16)


def _sb_kernel(q_ref, k_ref, v_ref, km_ref, vm_ref, gain_ref, o_ref, carry_scr, acc_scr):
    i = pl.program_id(1)

    def suffix_operator(width):
        r = lax.broadcasted_iota(jnp.int32, (width, width + BLK), 0)
        c = lax.broadcasted_iota(jnp.int32, (width, width + BLK), 1)
        return jnp.logical_or(c >= width, r > c).astype(BF16)

    def sweep(kb, vb, mask, first):
        width = kb.shape[1]
        suffix = suffix_operator(width)
        z = _bdot_nt(q_ref[...], kb)
        sp = jnp.maximum(z, 0.0) + jnp.log(1.0 + jnp.exp(-jnp.abs(z)))
        lk = -sp
        if mask is not None:
            lk = jnp.where(mask[None], lk, 0.0)
        hi = lk.astype(BF16)
        lo = (lk - hi.astype(F32)).astype(BF16)
        t = (jnp.dot(hi.reshape(N_HEADS * BLK, width), suffix, preferred_element_type=F32)
             + jnp.dot(lo.reshape(N_HEADS * BLK, width), suffix, preferred_element_type=F32))
        t = t.reshape(N_HEADS, BLK, width + BLK)
        later = t[:, :, :width]
        rowsum = t[:, :, width:]
        if not first:
            later = later + carry_scr[...]
        a = jnp.exp(z - sp + later)
        if mask is not None:
            a = jnp.where(mask[None], a, 0.0)
        pv = _bdot_nn(a.astype(BF16), vb)
        if first:
            acc_scr[...] = pv
            c_new = rowsum
        else:
            acc_scr[...] = acc_scr[...] + pv
            c_new = carry_scr[...] + rowsum
        carry_scr[...] = c_new
        return jnp.max(c_new[:, :, 0:1])

    def real_block(start, width):
        start = pl.multiple_of(start, BLK)
        return k_ref[:, pl.ds(start, width), :], v_ref[:, pl.ds(start, width), :]

    first_start = jnp.maximum(i - 1, 0) * BLK
    row2 = lax.broadcasted_iota(jnp.int32, (BLK, 2 * BLK), 0)
    col2 = lax.broadcasted_iota(jnp.int32, (BLK, 2 * BLK), 1)
    strict = col2 + (first_start - i * BLK) < row2
    top0 = sweep(*real_block(first_start, 2 * BLK), strict, True)

    def cond(c):
        j, done = c
        return jnp.logical_and(j >= 0, done == 0)

    def body(c):
        j, _ = c
        top = sweep(*real_block(j * BLK, BLK), None, False)
        return j - 1, (top < -SKIP_LOG).astype(jnp.int32)

    _, done = lax.while_loop(cond, body, (i - 2, (top0 < -SKIP_LOG).astype(jnp.int32)))

    @pl.when(done == 0)
    def _():
        col = lax.broadcasted_iota(jnp.int32, (BLK, BLK), 1)
        sweep(km_ref[...], vm_ref[...], col < N_META, False)

    o = acc_scr[...]
    ms = jnp.mean(o * o, axis=-1, keepdims=True)
    o_ref[...] = (o * lax.rsqrt(ms + EPS) * gain_ref[...]).astype(BF16)


def _head_spec_q(group, nq):
    return pl.BlockSpec((N_HEADS, BLK, HEAD_DIM), lambda b, i: (group, b * nq + i, 0))


def _head_spec_kv(group, seq):
    return pl.BlockSpec((N_HEADS, seq, HEAD_DIM), lambda b, i: (group, b, 0))


def _head_spec_meta(group):
    return pl.BlockSpec((N_HEADS, BLK, HEAD_DIM), lambda b, i: (group, 0, 0))


def _fox_attention(bound, proj, proj_m, qx, kx, kxm, out_gain, n_batch, seq):
    nq = seq // BLK
    rows = n_batch * seq
    return pl.pallas_call(
        _fox_kernel,
        grid=(n_batch, nq),
        in_specs=[
            pl.BlockSpec(memory_space=pltpu.SMEM),
            _head_spec_q(0, nq),
            _head_spec_q(0, nq),
            _head_spec_kv(1, seq),
            _head_spec_kv(0, seq),
            _head_spec_kv(2, seq),
            _head_spec_q(3, nq),
            _head_spec_meta(1),
            _head_spec_meta(0),
            _head_spec_meta(2),
            pl.BlockSpec((N_HEADS, 1, HEAD_DIM), lambda b, i: (0, 0, 0)),
        ],
        out_specs=pl.BlockSpec((N_HEADS, BLK, HEAD_DIM), lambda b, i: (0, b * nq + i, 0)),
        out_shape=jax.ShapeDtypeStruct((N_HEADS, rows, HEAD_DIM), BF16),
        scratch_shapes=[pltpu.VMEM((N_HEADS, BLK, LANES), F32)] * 3,
        compiler_params=_cparams(("parallel", "arbitrary")),
        name="fox_attention",
    )(bound, proj, qx, proj, kx, proj, proj, proj_m, kxm, proj_m, out_gain)


def _sb_attention(proj, proj_m, out_gain, n_batch, seq):
    nq = seq // BLK
    rows = n_batch * seq
    return pl.pallas_call(
        _sb_kernel,
        grid=(n_batch, nq),
        in_specs=[
            _head_spec_q(4, nq),
            _head_spec_kv(5, seq),
            _head_spec_kv(6, seq),
            _head_spec_meta(5),
            _head_spec_meta(6),
            pl.BlockSpec((N_HEADS, 1, HEAD_DIM), lambda b, i: (0, 0, 0)),
        ],
        out_specs=pl.BlockSpec((N_HEADS, BLK, HEAD_DIM), lambda b, i: (0, b * nq + i, 0)),
        out_shape=jax.ShapeDtypeStruct((N_HEADS, rows, HEAD_DIM), BF16),
        scratch_shapes=[pltpu.VMEM((N_HEADS, BLK, LANES), F32)] * 2,
        compiler_params=_cparams(("parallel", "arbitrary")),
        name="sb_attention",
    )(proj, proj, proj, proj_m, proj_m, out_gain)


ROUTE_TM = 256


def _router_kernel(oa_ref, ob_ref, wo_ref, x_ref, g_ref, whi_ref, wlo_ref, b_ref,
                   h_ref, u_ref, route_ref, w0_ref, w1_ref, cnt_ref, carry_scr):
    step = pl.program_id(0)
    lhs = jnp.concatenate([oa_ref[h] for h in range(N_HEADS)] + [ob_ref[h] for h in range(N_HEADS)], axis=1)
    x = x_ref[...] + jnp.dot(lhs, wo_ref[...], preferred_element_type=F32)
    h_ref[...] = x
    tm = x.shape[0]
    ms = jnp.mean(x * x, axis=-1, keepdims=True)
    u = x * lax.rsqrt(ms + EPS) * g_ref[...]
    u_ref[...] = u
    uhi = u.astype(BF16)
    ulo = (u - uhi.astype(F32)).astype(BF16)
    logits = (jnp.dot(uhi, whi_ref[...], preferred_element_type=F32)
              + jnp.dot(uhi, wlo_ref[...], preferred_element_type=F32)
              + jnp.dot(ulo, whi_ref[...], preferred_element_type=F32)) + b_ref[...]
    lane_i = lax.broadcasted_iota(jnp.int32, (tm, LANES), 1)
    lane = lane_i.astype(F32)
    big = float(4 * LANES)
    neg = -jnp.inf
    c = jnp.where(jnp.logical_and(lane_i >= N_EXPERTS, lane_i < N_EXPERTS + N_GROUPS), logits, neg)
    cmax = jnp.max(c, axis=1, keepdims=True)
    g_sel = jnp.min(jnp.where(c == cmax, lane, big), axis=1, keepdims=True) - N_EXPERTS
    g_gate = 1.0 / jnp.sum(jnp.exp(c - cmax), axis=1, keepdims=True)
    lo = g_sel * EXPERTS_PER_GROUP
    in_group = jnp.logical_and(lane >= lo, lane < lo + EXPERTS_PER_GROUP)
    f = jnp.where(in_group, logits, neg)
    t1 = jnp.max(f, axis=1, keepdims=True)
    i1 = jnp.min(jnp.where(f == t1, lane, big), axis=1, keepdims=True)
    f2 = jnp.where(lane == i1, neg, f)
    t2 = jnp.max(f2, axis=1, keepdims=True)
    i2 = jnp.min(jnp.where(f2 == t2, lane, big), axis=1, keepdims=True)
    d = jnp.exp(t2 - t1)
    w_first = g_gate / (1.0 + d)
    w0_ref[...] = jnp.broadcast_to(w_first, (tm, LANES))
    w1_ref[...] = jnp.broadcast_to(w_first * d, (tm, LANES))

    @pl.when(step == 0)
    def _():
        carry_scr[...] = jnp.zeros_like(carry_scr)

    oh0 = (lane == i1).astype(F32)
    oh1 = (lane == i2).astype(F32)
    oh = oh0 + oh1
    r = lax.broadcasted_iota(jnp.int32, (tm, tm), 0)
    cc = lax.broadcasted_iota(jnp.int32, (tm, tm), 1)
    before = (cc < r).astype(BF16)
    seen = jnp.dot(before, oh.astype(BF16), preferred_element_type=F32) + carry_scr[0:1, :]
    rank0 = jnp.sum(oh0 * seen, axis=1, keepdims=True)
    rank1 = jnp.sum(oh1 * seen, axis=1, keepdims=True)
    total = seen[tm - 1:tm, :] + oh[tm - 1:tm, :]
    carry_scr[...] = jnp.broadcast_to(total, carry_scr.shape)
    cnt_ref[...] = jnp.broadcast_to(total, cnt_ref.shape).astype(jnp.int32)
    vals = jnp.where(lane_i == 0, i1, jnp.where(lane_i == 1, i2, jnp.where(lane_i == 2, rank0,
                     jnp.where(lane_i == 3, rank1, 0.0))))
    route_ref[...] = vals.astype(jnp.int32)


def _outproj_router(oa, ob, wo_bf, x2d, norm_g, whi, wlo, b_pad):
    rows = x2d.shape[0]
    tm = ROUTE_TM
    return pl.pallas_call(
        _router_kernel,
        grid=(rows // tm,),
        in_specs=[
            pl.BlockSpec((N_HEADS, tm, HEAD_DIM), lambda i: (0, i, 0)),
            pl.BlockSpec((N_HEADS, tm, HEAD_DIM), lambda i: (0, i, 0)),
            pl.BlockSpec((2 * D_GROUP, D_MODEL), lambda i: (0, 0)),
            pl.BlockSpec((tm, D_MODEL), lambda i: (i, 0)),
            pl.BlockSpec((1, D_MODEL), lambda i: (0, 0)),
            pl.BlockSpec((D_MODEL, LANES), lambda i: (0, 0)),
            pl.BlockSpec((D_MODEL, LANES), lambda i: (0, 0)),
            pl.BlockSpec((1, LANES), lambda i: (0, 0)),
        ],
        out_specs=[
            pl.BlockSpec((tm, D_MODEL), lambda i: (i, 0)),
            pl.BlockSpec((tm, D_MODEL), lambda i: (i, 0)),
            pl.BlockSpec((tm, LANES), lambda i: (i, 0)),
            pl.BlockSpec((tm, LANES), lambda i: (i, 0)),
            pl.BlockSpec((tm, LANES), lambda i: (i, 0)),
            pl.BlockSpec((8, LANES), lambda i: (0, 0)),
        ],
        out_shape=[
            jax.ShapeDtypeStruct((rows, D_MODEL), F32),
            jax.ShapeDtypeStruct((rows, D_MODEL), F32),
            jax.ShapeDtypeStruct((rows, LANES), jnp.int32),
            jax.ShapeDtypeStruct((rows, LANES), F32),
            jax.ShapeDtypeStruct((rows, LANES), F32),
            jax.ShapeDtypeStruct((8, LANES), jnp.int32),
        ],
        scratch_shapes=[pltpu.VMEM((8, LANES), F32)],
        compiler_params=_cparams(("arbitrary",)),
        name="outproj_router",
    )(oa, ob, wo_bf, x2d, norm_g, whi, wlo, b_pad)


MOVE_TM = 256
MOVE_UNROLL = 8


def _scatter_kernel(pos_ref, u_ref, xs_ref, sem):
    base = pl.program_id(0) * MOVE_TM

    def body(t, carry):
        for k in range(TOP_K):
            dst = pos_ref[(base + t) * TOP_K + k]
            pltpu.make_async_copy(u_ref.at[pl.ds(t, 1)], xs_ref.at[pl.ds(dst, 1)], sem).start(priority=k)
        return carry

    lax.fori_loop(0, MOVE_TM, body, 0, unroll=MOVE_UNROLL)
    for _ in range(TOP_K):
        pltpu.make_async_copy(u_ref, xs_ref.at[pl.ds(0, MOVE_TM)], sem).wait()


def _scatter_rows(pos_flat, u2):
    n_tok = u2.shape[0]
    return pl.pallas_call(
        _scatter_kernel,
        grid_spec=pltpu.PrefetchScalarGridSpec(
            num_scalar_prefetch=1,
            grid=(n_tok // MOVE_TM,),
            in_specs=[pl.BlockSpec((MOVE_TM, D_MODEL), lambda i, pos: (i, 0))],
            out_specs=pl.BlockSpec(memory_space=pl.ANY),
            scratch_shapes=[pltpu.SemaphoreType.DMA(())],
        ),
        out_shape=jax.ShapeDtypeStruct((n_tok * TOP_K, D_MODEL), u2.dtype),
        compiler_params=_cparams(("arbitrary",)),
        name="scatter_rows",
    )(pos_flat, u2)


FFN_TM = 128
WEIGHT_DMA_PRIORITY = 1


def _ffn_kernel(tile_ref, exp_ref, lo_ref, hi_ref, first_ref, new_ref, slot_ref, next_ref,
                x_ref, w1_hbm, w3_hbm, w2_hbm, o_ref,
                xb_scr, w1_buf, w3_buf, w2_buf, w1_bf, w3_bf, w2_bf, sem):
    it = pl.program_id(0)
    lo = lo_ref[it]
    hi = hi_ref[it]
    slot = slot_ref[it]

    def weight_copies(expert, s):
        return (pltpu.make_async_copy(w1_hbm.at[expert], w1_buf.at[s], sem.at[s, 0]),
                pltpu.make_async_copy(w3_hbm.at[expert], w3_buf.at[s], sem.at[s, 1]),
                pltpu.make_async_copy(w2_hbm.at[expert], w2_buf.at[s], sem.at[s, 2]))

    @pl.when(it == 0)
    def _():
        for cp in weight_copies(exp_ref[0], slot):
            cp.start(priority=WEIGHT_DMA_PRIORITY)

    @pl.when(new_ref[it] == 1)
    def _():
        for cp in weight_copies(exp_ref[it], slot):
            cp.wait()

        @pl.when(next_ref[it] >= 0)
        def _():
            for cp in weight_copies(next_ref[it], 1 - slot):
                cp.start(priority=WEIGHT_DMA_PRIORITY)

        w1_bf[...] = w1_buf[slot].astype(BF16)
        w3_bf[...] = w3_buf[slot].astype(BF16)
        w2_bf[...] = w2_buf[slot].astype(BF16)

    @pl.when(first_ref[it] == 1)
    def _():
        xb_scr[...] = x_ref[...].astype(BF16)
        o_ref[...] = jnp.zeros_like(o_ref)

    @pl.when(hi > lo)
    def _():
        x = xb_scr[...]
        a = jnp.dot(x, w1_bf[...], preferred_element_type=F32)
        b = jnp.dot(x, w3_bf[...], preferred_element_type=F32)
        mid = a / (1.0 + jnp.exp(-a)) * b
        rows = tile_ref[it] * FFN_TM + lax.broadcasted_iota(jnp.int32, mid.shape, 0)
        mid = jnp.where(jnp.logical_and(rows >= lo, rows < hi), mid, 0.0).astype(BF16)
        o_ref[...] += jnp.dot(mid, w2_bf[...], preferred_element_type=F32)


def _grouped_ffn(items, xs, w1, w3, w2):
    n_items = items[0].shape[0]
    n_rows = xs.shape[0]

    def tile_map(i, t, *_):
        return (t[i], 0)

    return pl.pallas_call(
        _ffn_kernel,
        grid_spec=pltpu.PrefetchScalarGridSpec(
            num_scalar_prefetch=len(items),
            grid=(n_items,),
            in_specs=[
                pl.BlockSpec((FFN_TM, D_MODEL), tile_map),
                pl.BlockSpec(memory_space=pl.ANY),
                pl.BlockSpec(memory_space=pl.ANY),
                pl.BlockSpec(memory_space=pl.ANY),
            ],
            out_specs=pl.BlockSpec((FFN_TM, D_MODEL), tile_map),
            scratch_shapes=[
                pltpu.VMEM((FFN_TM, D_MODEL), BF16),
                pltpu.VMEM((2, D_MODEL, D_EXPERT), F32),
                pltpu.VMEM((2, D_MODEL, D_EXPERT), F32),
                pltpu.VMEM((2, D_EXPERT, D_MODEL), F32),
                pltpu.VMEM((D_MODEL, D_EXPERT), BF16),
                pltpu.VMEM((D_MODEL, D_EXPERT), BF16),
                pltpu.VMEM((D_EXPERT, D_MODEL), BF16),
                pltpu.SemaphoreType.DMA((2, 3)),
            ],
        ),
        out_shape=jax.ShapeDtypeStruct((n_rows, D_MODEL), F32),
        compiler_params=_cparams(("arbitrary",)),
        name="grouped_ffn",
    )(*items, xs, w1, w3, w2)


def _combine_kernel(pos_ref, h_ref, w0_ref, w1_ref, ys_ref, o_ref, ybuf, sem):
    step = pl.program_id(0)
    n_steps = pl.num_programs(0)

    def fetch(s, slot):
        def body(t, carry):
            for k in range(TOP_K):
                src = pos_ref[(s * MOVE_TM + t) * TOP_K + k]
                pltpu.make_async_copy(ys_ref.at[pl.ds(src, 1)], ybuf.at[slot, k, pl.ds(t, 1)],
                                      sem.at[slot]).start(priority=k)
            return carry
        lax.fori_loop(0, MOVE_TM, body, 0, unroll=MOVE_UNROLL)

    @pl.when(step == 0)
    def _():
        fetch(0, 0)

    slot = step % 2

    @pl.when(step + 1 < n_steps)
    def _():
        fetch(step + 1, 1 - slot)

    for k in range(TOP_K):
        pltpu.make_async_copy(ys_ref.at[pl.ds(0, MOVE_TM)], ybuf.at[slot, k], sem.at[slot]).wait()
    reps = D_MODEL // LANES
    w0 = jnp.concatenate([w0_ref[...]] * reps, axis=1)
    w1 = jnp.concatenate([w1_ref[...]] * reps, axis=1)
    o_ref[...] = h_ref[...] + w0 * ybuf[slot, 0] + w1 * ybuf[slot, 1]


def _combine(pos_flat, h1, w0b, w1b, ys):
    rows = h1.shape[0]
    tm = MOVE_TM
    return pl.pallas_call(
        _combine_kernel,
        grid_spec=pltpu.PrefetchScalarGridSpec(
            num_scalar_prefetch=1,
            grid=(rows // tm,),
            in_specs=[
                pl.BlockSpec((tm, D_MODEL), lambda i, pos: (i, 0)),
                pl.BlockSpec((tm, LANES), lambda i, pos: (i, 0)),
                pl.BlockSpec((tm, LANES), lambda i, pos: (i, 0)),
                pl.BlockSpec(memory_space=pl.ANY),
            ],
            out_specs=pl.BlockSpec((tm, D_MODEL), lambda i, pos: (i, 0)),
            scratch_shapes=[pltpu.VMEM((2, TOP_K, tm, D_MODEL), F32), pltpu.SemaphoreType.DMA((2,))],
        ),
        out_shape=jax.ShapeDtypeStruct((rows, D_MODEL), F32),
        compiler_params=_cparams(("arbitrary",)),
        name="combine",
    )(pos_flat, h1, w0b, w1b, ys)


def _work_items(counts, n_assign):
    ends = jnp.cumsum(counts)
    starts = (ends - counts).astype(jnp.int32)
    n_tiles = n_assign // FFN_TM
    tile_starts = jnp.arange(n_tiles, dtype=jnp.int32) * FFN_TM
    seg_lo = jnp.sort(jnp.concatenate([tile_starts, starts]))
    seg_hi = jnp.concatenate([seg_lo[1:], jnp.array([n_assign], jnp.int32)])
    item_tile = jnp.minimum(seg_lo // FFN_TM, n_tiles - 1).astype(jnp.int32)
    item_exp = jnp.minimum(jnp.sum((ends[None, :] <= seg_lo[:, None]).astype(jnp.int32), axis=1), N_EXPERTS - 1)
    item_exp = item_exp.astype(jnp.int32)
    item_first = (seg_lo == item_tile * FFN_TM).astype(jnp.int32)
    n_items = seg_lo.shape[0]
    item_new = jnp.concatenate([jnp.ones((1,), jnp.int32), (item_exp[1:] != item_exp[:-1]).astype(jnp.int32)])
    ordinal = jnp.cumsum(item_new) - 1
    run_exp = jnp.full((n_items + 1,), -1, jnp.int32).at[ordinal].set(item_exp)
    item_next = run_exp[ordinal + 1]
    item_slot = (ordinal % 2).astype(jnp.int32)
    items = (item_tile, item_exp, seg_lo, seg_hi, item_first, item_new, item_slot, item_next)
    return starts, items


def kernel(x, meta_tokens, norm1_g, w_in, b_f, q_gain, k_gain, fox_out_gain, sb_out_gain, w_out, norm2_g,
           w_coarse, b_coarse, w_fine, b_fine, w1, w3, w2):
    assert norm1_g.shape[0] == 1, "single-layer block"
    n_batch, seq, _ = x.shape
    n_tok = n_batch * seq
    n_assign = n_tok * TOP_K
    scale = HEAD_DIM ** -0.5
    x2d = x.reshape(n_tok, D_MODEL)

    w_in_t = w_in[0].T
    w_in_bf = _cast_bf16(w_in_t, D_PROJ)
    w_out_bf = _cast_bf16(w_out[0], 2 * D_GROUP)
    wft_bf = jnp.pad(w_in_t[D_PROJ:], ((0, LANES - N_HEADS), (0, 0))).astype(BF16)
    bf_pad = jnp.pad(b_f[0], (0, LANES - N_HEADS)).reshape(1, LANES)
    ones = jnp.ones((HEAD_DIM,), F32)
    zeros = jnp.zeros((HEAD_DIM,), F32)
    gains = jnp.stack([jnp.stack([q_gain[0] * scale, ones]), jnp.stack([k_gain[0], ones]),
                       jnp.stack([ones, zeros]), jnp.stack([ones, zeros]), jnp.stack([ones * scale, zeros]),
                       jnp.stack([ones, zeros]), jnp.stack([ones, zeros])])
    meta_pad = jnp.pad(meta_tokens.astype(F32), ((0, BLK - N_META), (0, 0)))
    n1 = norm1_g[0].reshape(1, D_MODEL)

    proj, lf = _inproj(x2d, n1, w_in_bf, wft_bf, bf_pad, gains, tm=1024)
    proj_m, lf_m = _inproj(meta_pad, n1, w_in_bf, wft_bf, bf_pad, gains, tm=BLK)
    qx, kx, kxm = _cumgate(lf_m, lf, n_batch, seq)
    qk_bound = 1.02 * HEAD_DIM * scale * jnp.max(jnp.abs(q_gain[0])) * jnp.max(jnp.abs(k_gain[0]))
    bound = (2.0 * qk_bound).reshape(1).astype(F32)
    oa = _fox_attention(bound, proj, proj_m, qx, kx, kxm, fox_out_gain[0].reshape(N_HEADS, 1, HEAD_DIM),
                        n_batch, seq)
    ob = _sb_attention(proj, proj_m, sb_out_gain[0].reshape(N_HEADS, 1, HEAD_DIM), n_batch, seq)

    wr = jnp.pad(jnp.concatenate([w_fine[0], w_coarse[0]], axis=1),
                 ((0, 0), (0, LANES - N_GROUPS - N_EXPERTS)))
    wr_hi = wr.astype(BF16)
    wr_lo = (wr - wr_hi.astype(F32)).astype(BF16)
    br = jnp.pad(jnp.concatenate([b_fine[0], b_coarse[0]]), (0, LANES - N_GROUPS - N_EXPERTS)).reshape(1, LANES)
    h1, u2, route, w0b, w1b, cnt = _outproj_router(oa, ob, w_out_bf, x2d, norm2_g[0].reshape(1, D_MODEL),
                                                   wr_hi, wr_lo, br)
    starts, items = _work_items(cnt[0, :N_EXPERTS], n_assign)
    eid = route[:, 0:TOP_K]
    start_of = jnp.sum(jnp.where(eid[..., None] == jnp.arange(N_EXPERTS, dtype=jnp.int32), starts, 0), axis=-1)
    pos = (start_of + route[:, TOP_K:2 * TOP_K]).reshape(-1)
    xs = _scatter_rows(pos, u2)
    ys = _grouped_ffn(items, xs, w1[0], w3[0], w2[0])
    out = _combine(pos, h1, w0b, w1b, ys)
    return out.reshape(n_batch, seq, D_MODEL)
```

```python
import functools

import jax
import jax.numpy as jnp
from jax import lax
from jax.experimental import pallas as pl
from jax.experimental.pallas import tpu as pltpu

F32 = jnp.float32
BF16 = jnp.bfloat16

D_MODEL = 2048
N_META = 16
HEAD_DIM = 128
N_HEADS = 8
D_GROUP = N_HEADS * HEAD_DIM
N_PROJ_GROUPS = 7
D_PROJ = N_PROJ_GROUPS * D_GROUP
N_GROUPS = 8
EXPERTS_PER_GROUP = 8
N_EXPERTS = 64
TOP_K = 2
D_EXPERT = 512
EPS = 1e-6
LANES = 128
D_PACKED = D_MODEL // 2
MXU_HEADS = 2
BLK = 128
SKIP_LOG = 88.0
VMEM_LIMIT = 56 * 1024 * 1024


def _cparams(sem, vmem=VMEM_LIMIT):
    return pltpu.CompilerParams(dimension_semantics=sem, vmem_limit_bytes=vmem)


def _log_sigmoid(x):
    return jnp.minimum(x, 0.0) - jnp.log(1.0 + jnp.exp(-jnp.abs(x)))


def _split3(x):
    hi = x.astype(BF16)
    r1 = x - hi.astype(F32)
    mid = r1.astype(BF16)
    lo = (r1 - mid.astype(F32)).astype(BF16)
    return hi, mid, lo


def _dot_nt(a, b):
    return lax.dot_general(a, b, (((1,), (1,)), ((), ())), preferred_element_type=F32)


def _pack_bf16_pairs(x):
    half = x.shape[1] // 2
    xr = x.astype(BF16).astype(F32)
    hi = lax.bitcast_convert_type(xr[:, :half], jnp.uint32)
    lo = lax.bitcast_convert_type(xr[:, half:], jnp.uint32)
    return hi | lax.shift_right_logical(lo, jnp.uint32(16))


def _unpack_bf16_pairs(p):
    hi = lax.bitcast_convert_type(p & jnp.uint32(0xFFFF0000), F32)
    lo = lax.bitcast_convert_type(lax.shift_left(p, jnp.uint32(16)), F32)
    return jnp.concatenate([hi, lo], axis=1)


def _cast_kernel(x_ref, o_ref):
    o_ref[...] = x_ref[...].astype(o_ref.dtype)


def _cast_bf16(w, n_rows, tr=512):
    n_cols = w.shape[1]
    return pl.pallas_call(
        _cast_kernel,
        grid=(n_rows // tr,),
        in_specs=[pl.BlockSpec((tr, n_cols), lambda i: (i, 0))],
        out_specs=pl.BlockSpec((tr, n_cols), lambda i: (i, 0)),
        out_shape=jax.ShapeDtypeStruct((n_rows, n_cols), BF16),
        compiler_params=_cparams(("parallel",)),
        name="cast_bf16",
    )(w)


def _inproj_kernel(x_ref, g_ref, wt_ref, wft_ref, bf_ref, gain_ref, o_ref, lf_ref, u_ref, *, heads_per_tile):
    j = pl.program_id(1)

    @pl.when(j == 0)
    def _():
        x = x_ref[...]
        ms = jnp.mean(x * x, axis=-1, keepdims=True)
        u = (x * lax.rsqrt(ms + EPS) * g_ref[...]).astype(BF16)
        u_ref[...] = u
        f = _dot_nt(u, wft_ref[...]) + bf_ref[...]
        lf_ref[...] = _log_sigmoid(f)

    gain = gain_ref[0, 0:1, :]
    normed = gain_ref[0, 1:2, :] > 0.5
    u = u_ref[...]
    for s in range(heads_per_tile // MXU_HEADS):
        cols = pl.ds(s * MXU_HEADS * HEAD_DIM, MXU_HEADS * HEAD_DIM)
        acc = _dot_nt(u, wt_ref[cols, :])
        for hh in range(MXU_HEADS):
            y = acc[:, hh * HEAD_DIM:(hh + 1) * HEAD_DIM]
            ms = jnp.mean(y * y, axis=-1, keepdims=True)
            scale = jnp.where(normed, lax.rsqrt(ms + EPS), 1.0)
            o_ref[s * MXU_HEADS + hh] = (y * scale * gain).astype(BF16)


def _inproj(x2d, norm_g, wt_bf, wft_bf, bf_pad, gains, tm, tn=D_GROUP):
    rows = x2d.shape[0]
    hpt = tn // HEAD_DIM
    kern = functools.partial(_inproj_kernel, heads_per_tile=hpt)
    return pl.pallas_call(
        kern,
        grid=(rows // tm, D_PROJ // tn),
        in_specs=[
            pl.BlockSpec((tm, D_MODEL), lambda i, j: (i, 0)),
            pl.BlockSpec((1, D_MODEL), lambda i, j: (0, 0)),
            pl.BlockSpec((tn, D_MODEL), lambda i, j: (j, 0)),
            pl.BlockSpec((LANES, D_MODEL), lambda i, j: (0, 0)),
            pl.BlockSpec((1, LANES), lambda i, j: (0, 0)),
            pl.BlockSpec((1, 2, HEAD_DIM), lambda i, j: (j // (N_HEADS // hpt), 0, 0)),
        ],
        out_specs=[
            pl.BlockSpec((hpt, tm, HEAD_DIM), lambda i, j: (j, i, 0)),
            pl.BlockSpec((tm, LANES), lambda i, j: (i, 0)),
        ],
        out_shape=[
            jax.ShapeDtypeStruct((D_PROJ // HEAD_DIM, rows, HEAD_DIM), BF16),
            jax.ShapeDtypeStruct((rows, LANES), F32),
        ],
        scratch_shapes=[pltpu.VMEM((tm, D_MODEL), BF16)],
        compiler_params=_cparams(("parallel", "arbitrary")),
        name="inproj",
    )(x2d, norm_g, wt_bf, wft_bf, bf_pad, gains)


def _cumgate_kernel(lfm_ref, lf_ref, qx_ref, kx_ref, kxm_ref, *, n_blk):
    row = lax.broadcasted_iota(jnp.int32, (BLK, BLK), 0)
    col = lax.broadcasted_iota(jnp.int32, (BLK, BLK), 1)
    tri = (col <= row).astype(BF16)
    one = jnp.ones((BLK, LANES), F32)
    zero = jnp.zeros((BLK, LANES), F32)

    def prefix(x):
        hi, mid, lo = _split3(x)
        return (jnp.dot(tri, hi, preferred_element_type=F32)
                + jnp.dot(tri, mid, preferred_element_type=F32)
                + jnp.dot(tri, lo, preferred_element_type=F32))

    def ext(cum, h):
        c = jnp.broadcast_to(cum[:, h:h + 1], (BLK, LANES))
        hi, mid, lo = (t.astype(F32) for t in _split3(c))
        qx = jnp.where(col == 0, hi, jnp.where(col == 1, mid, jnp.where(col == 2, lo,
                       jnp.where(col < 6, one, zero))))
        kx = jnp.where(col < 3, one, jnp.where(col == 3, -hi, jnp.where(col == 4, -mid,
                       jnp.where(col == 5, -lo, zero))))
        return qx.astype(BF16), kx.astype(BF16)

    lfm = jnp.where(row < N_META, lfm_ref[...], 0.0)
    cum_m = prefix(lfm)
    for h in range(N_HEADS):
        _, kx = ext(cum_m, h)
        kxm_ref[h] = kx
    carry = cum_m[BLK - 1:BLK, :]
    for b in range(n_blk):
        cum = prefix(lf_ref[b * BLK:(b + 1) * BLK, :]) + carry
        carry = cum[BLK - 1:BLK, :]
        for h in range(N_HEADS):
            qx, kx = ext(cum, h)
            qx_ref[h, b * BLK:(b + 1) * BLK, :] = qx
            kx_ref[h, b * BLK:(b + 1) * BLK, :] = kx


def _cumgate(lf_meta, lf_real, n_batch, seq):
    kern = functools.partial(_cumgate_kernel, n_blk=seq // BLK)
    return pl.pallas_call(
        kern,
        grid=(n_batch,),
        in_specs=[
            pl.BlockSpec((BLK, LANES), lambda b: (0, 0)),
            pl.BlockSpec((seq, LANES), lambda b: (b, 0)),
        ],
        out_specs=[
            pl.BlockSpec((N_HEADS, seq, LANES), lambda b: (0, b, 0)),
            pl.BlockSpec((N_HEADS, seq, LANES), lambda b: (0, b, 0)),
            pl.BlockSpec((N_HEADS, BLK, LANES), lambda b: (0, 0, 0)),
        ],
        out_shape=[
            jax.ShapeDtypeStruct((N_HEADS, n_batch * seq, LANES), BF16),
            jax.ShapeDtypeStruct((N_HEADS, n_batch * seq, LANES), BF16),
            jax.ShapeDtypeStruct((N_HEADS, BLK, LANES), BF16),
        ],
        compiler_params=_cparams(("arbitrary",)),
        name="cumgate",
    )(lf_meta, lf_real)


def _bdot_nt(a, b):
    return lax.dot_general(a, b, (((2,), (2,)), ((0,), (0,))), preferred_element_type=F32)


def _bdot_nn(a, b):
    return lax.dot_general(a, b, (((2,), (1,)), ((0,), (0,))), preferred_element_type=F32)


def _fox_kernel(bound_ref, q_ref, qx_ref, k_ref, kx_ref, v_ref, g_ref, km_ref, kxm_ref, vm_ref, gain_ref,
                o_ref, m_scr, l_scr, acc_scr):
    i = pl.program_id(1)

    def sweep(kb, kxb, vb, mask, first):
        qa = jnp.concatenate([q_ref[...], qx_ref[...]], axis=2)
        ka = jnp.concatenate([kb, kxb], axis=2)
        s = _bdot_nt(qa, ka)
        if mask is not None:
            s = jnp.where(mask[None], s, -jnp.inf)
        m_cur = jnp.max(s, axis=2, keepdims=True)
        v1 = jnp.concatenate([vb, jnp.ones_like(vb)], axis=2)
        if first:
            m_col = m_cur
            pv = _bdot_nn(jnp.exp(s - m_col).astype(BF16), v1)
            l_scr[...] = pv[:, :, BLK:]
            acc_scr[...] = pv[:, :, :BLK]
            m_scr[...] = jnp.broadcast_to(m_col, m_scr.shape)
        else:
            m_prev = m_scr[...]
            m_new = jnp.maximum(m_prev, m_cur)
            m_col = m_new[:, :, 0:1]
            alpha = jnp.exp(m_prev - m_new)
            pv = _bdot_nn(jnp.exp(s - m_new).astype(BF16), v1)
            l_scr[...] = alpha * l_scr[...] + pv[:, :, BLK:]
            acc_scr[...] = alpha * acc_scr[...] + pv[:, :, :BLK]
            m_scr[...] = m_new
        return jnp.max(s[:, :, 0:1] - m_col)

    def real_block(start, width):
        start = pl.multiple_of(start, BLK)
        return (k_ref[:, pl.ds(start, width), :], kx_ref[:, pl.ds(start, width), :],
                v_ref[:, pl.ds(start, width), :])

    bound = bound_ref[0]
    first_start = jnp.maximum(i - 1, 0) * BLK
    row2 = lax.broadcasted_iota(jnp.int32, (BLK, 2 * BLK), 0)
    col2 = lax.broadcasted_iota(jnp.int32, (BLK, 2 * BLK), 1)
    causal = col2 + (first_start - i * BLK) <= row2
    gap0 = sweep(*real_block(first_start, 2 * BLK), causal, True)

    def cond(c):
        j, done = c
        return jnp.logical_and(j >= 0, done == 0)

    def body(c):
        j, _ = c
        gap = sweep(*real_block(j * BLK, BLK), None, False)
        return j - 1, (gap + bound < -SKIP_LOG).astype(jnp.int32)

    _, done = lax.while_loop(cond, body, (i - 2, (gap0 + bound < -SKIP_LOG).astype(jnp.int32)))

    @pl.when(done == 0)
    def _():
        col = lax.broadcasted_iota(jnp.int32, (BLK, BLK), 1)
        sweep(km_ref[...], kxm_ref[...], vm_ref[...], col < N_META, False)

    o = acc_scr[...] / l_scr[...]
    ms = jnp.mean(o * o, axis=-1, keepdims=True)
    gate = 1.0 / (1.0 + jnp.exp(-g_ref[...].astype(F32)))
    o_ref[...] = (o * lax.rsqrt(ms + EPS) * gain_ref[...] * gate).astype(BF16)


def _sb_kernel(q_ref, k_ref, v_ref, km_ref, vm_ref, gain_ref, o_ref, carry_scr, acc_scr):
    i = pl.program_id(1)

    def suffix_operator(width):
        r = lax.broadcasted_iota(jnp.int32, (width, width + BLK), 0)
        c = lax.broadcasted_iota(jnp.int32, (width, width + BLK), 1)
        return jnp.logical_or(c >= width, r > c).astype(BF16)

    def sweep(kb, vb, mask, first):
        width = kb.shape[1]
        suffix = suffix_operator(width)
        z = _bdot_nt(q_ref[...], kb)
        sp = jnp.maximum(z, 0.0) + jnp.log(1.0 + jnp.exp(-jnp.abs(z)))
        lk = -sp
        if mask is not None:
            lk = jnp.where(mask[None], lk, 0.0)
        hi = lk.astype(BF16)
        lo = (lk - hi.astype(F32)).astype(BF16)
        t = (jnp.dot(hi.reshape(N_HEADS * BLK, width), suffix, preferred_element_type=F32)
             + jnp.dot(lo.reshape(N_HEADS * BLK, width), suffix, preferred_element_type=F32))
        t = t.reshape(N_HEADS, BLK, width + BLK)
        later = t[:, :, :width]
        rowsum = t[:, :, width:]
        if not first:
            later = later + carry_scr[...]
        a = jnp.exp(z - sp + later)
        if mask is not None:
            a = jnp.where(mask[None], a, 0.0)
        pv = _bdot_nn(a.astype(BF16), vb)
        if first:
            acc_scr[...] = pv
            c_new = rowsum
        else:
            acc_scr[...] = acc_scr[...] + pv
            c_new = carry_scr[...] + rowsum
        carry_scr[...] = c_new
        return jnp.max(c_new[:, :, 0:1])

    def real_block(start, width):
        start = pl.multiple_of(start, BLK)
        return k_ref[:, pl.ds(start, width), :], v_ref[:, pl.ds(start, width), :]

    first_start = jnp.maximum(i - 1, 0) * BLK
    row2 = lax.broadcasted_iota(jnp.int32, (BLK, 2 * BLK), 0)
    col2 = lax.broadcasted_iota(jnp.int32, (BLK, 2 * BLK), 1)
    strict = col2 + (first_start - i * BLK) < row2
    top0 = sweep(*real_block(first_start, 2 * BLK), strict, True)

    def cond(c):
        j, done = c
        return jnp.logical_and(j >= 0, done == 0)

    def body(c):
        j, _ = c
        top = sweep(*real_block(j * BLK, BLK), None, False)
        return j - 1, (top < -SKIP_LOG).astype(jnp.int32)

    _, done = lax.while_loop(cond, body, (i - 2, (top0 < -SKIP_LOG).astype(jnp.int32)))

    @pl.when(done == 0)
    def _():
        col = lax.broadcasted_iota(jnp.int32, (BLK, BLK), 1)
        sweep(km_ref[...], vm_ref[...], col < N_META, False)

    o = acc_scr[...]
    ms = jnp.mean(o * o, axis=-1, keepdims=True)
    o_ref[...] = (o * lax.rsqrt(ms + EPS) * gain_ref[...]).astype(BF16)


def _head_spec_q(group, nq):
    return pl.BlockSpec((N_HEADS, BLK, HEAD_DIM), lambda b, i: (group, b * nq + i, 0))


def _head_spec_kv(group, seq):
    return pl.BlockSpec((N_HEADS, seq, HEAD_DIM), lambda b, i: (group, b, 0))


def _head_spec_meta(group):
    return pl.BlockSpec((N_HEADS, BLK, HEAD_DIM), lambda b, i: (group, 0, 0))


def _fox_attention(bound, proj, proj_m, qx, kx, kxm, out_gain, n_batch, seq):
    nq = seq // BLK
    rows = n_batch * seq
    return pl.pallas_call(
        _fox_kernel,
        grid=(n_batch, nq),
        in_specs=[
            pl.BlockSpec(memory_space=pltpu.SMEM),
            _head_spec_q(0, nq),
            _head_spec_q(0, nq),
            _head_spec_kv(1, seq),
            _head_spec_kv(0, seq),
            _head_spec_kv(2, seq),
            _head_spec_q(3, nq),
            _head_spec_meta(1),
            _head_spec_meta(0),
            _head_spec_meta(2),
            pl.BlockSpec((N_HEADS, 1, HEAD_DIM), lambda b, i: (0, 0, 0)),
        ],
        out_specs=pl.BlockSpec((N_HEADS, BLK, HEAD_DIM), lambda b, i: (0, b * nq + i, 0)),
        out_shape=jax.ShapeDtypeStruct((N_HEADS, rows, HEAD_DIM), BF16),
        scratch_shapes=[pltpu.VMEM((N_HEADS, BLK, LANES), F32)] * 3,
        compiler_params=_cparams(("parallel", "arbitrary")),
        name="fox_attention",
    )(bound, proj, qx, proj, kx, proj, proj, proj_m, kxm, proj_m, out_gain)


def _sb_attention(proj, proj_m, out_gain, n_batch, seq):
    nq = seq // BLK
    rows = n_batch * seq
    return pl.pallas_call(
        _sb_kernel,
        grid=(n_batch, nq),
        in_specs=[
            _head_spec_q(4, nq),
            _head_spec_kv(5, seq),
            _head_spec_kv(6, seq),
            _head_spec_meta(5),
            _head_spec_meta(6),
            pl.BlockSpec((N_HEADS, 1, HEAD_DIM), lambda b, i: (0, 0, 0)),
        ],
        out_specs=pl.BlockSpec((N_HEADS, BLK, HEAD_DIM), lambda b, i: (0, b * nq + i, 0)),
        out_shape=jax.ShapeDtypeStruct((N_HEADS, rows, HEAD_DIM), BF16),
        scratch_shapes=[pltpu.VMEM((N_HEADS, BLK, LANES), F32)] * 2,
        compiler_params=_cparams(("parallel", "arbitrary")),
        name="sb_attention",
    )(proj, proj, proj, proj_m, proj_m, out_gain)


ROUTE_TM = 256


def _router_kernel(oa_ref, ob_ref, wo_ref, x_ref, g_ref, whi_ref, wlo_ref, b_ref,
                   h_ref, u_ref, route_ref, w0_ref, w1_ref, cnt_ref, carry_scr):
    step = pl.program_id(0)
    lhs = jnp.concatenate([oa_ref[h] for h in range(N_HEADS)] + [ob_ref[h] for h in range(N_HEADS)], axis=1)
    x = x_ref[...] + jnp.dot(lhs, wo_ref[...], preferred_element_type=F32)
    h_ref[...] = x
    tm = x.shape[0]
    ms = jnp.mean(x * x, axis=-1, keepdims=True)
    u = x * lax.rsqrt(ms + EPS) * g_ref[...]
    u_ref[...] = _pack_bf16_pairs(u)
    uhi = u.astype(BF16)
    ulo = (u - uhi.astype(F32)).astype(BF16)
    logits = (jnp.dot(uhi, whi_ref[...], preferred_element_type=F32)
              + jnp.dot(uhi, wlo_ref[...], preferred_element_type=F32)
              + jnp.dot(ulo, whi_ref[...], preferred_element_type=F32)) + b_ref[...]
    lane_i = lax.broadcasted_iota(jnp.int32, (tm, LANES), 1)
    lane = lane_i.astype(F32)
    big = float(4 * LANES)
    neg = -jnp.inf
    c = jnp.where(jnp.logical_and(lane_i >= N_EXPERTS, lane_i < N_EXPERTS + N_GROUPS), logits, neg)
    cmax = jnp.max(c, axis=1, keepdims=True)
    g_sel = jnp.min(jnp.where(c == cmax, lane, big), axis=1, keepdims=True) - N_EXPERTS
    g_gate = 1.0 / jnp.sum(jnp.exp(c - cmax), axis=1, keepdims=True)
    lo = g_sel * EXPERTS_PER_GROUP
    in_group = jnp.logical_and(lane >= lo, lane < lo + EXPERTS_PER_GROUP)
    f = jnp.where(in_group, logits, neg)
    t1 = jnp.max(f, axis=1, keepdims=True)
    i1 = jnp.min(jnp.where(f == t1, lane, big), axis=1, keepdims=True)
    f2 = jnp.where(lane == i1, neg, f)
    t2 = jnp.max(f2, axis=1, keepdims=True)
    i2 = jnp.min(jnp.where(f2 == t2, lane, big), axis=1, keepdims=True)
    d = jnp.exp(t2 - t1)
    w_first = g_gate / (1.0 + d)
    w0_ref[...] = jnp.broadcast_to(w_first, (tm, LANES))
    w1_ref[...] = jnp.broadcast_to(w_first * d, (tm, LANES))

    @pl.when(step == 0)
    def _():
        carry_scr[...] = jnp.zeros_like(carry_scr)

    oh0 = (lane == i1).astype(F32)
    oh1 = (lane == i2).astype(F32)
    oh = oh0 + oh1
    r = lax.broadcasted_iota(jnp.int32, (tm, tm), 0)
    cc = lax.broadcasted_iota(jnp.int32, (tm, tm), 1)
    before = (cc < r).astype(BF16)
    seen = jnp.dot(before, oh.astype(BF16), preferred_element_type=F32) + carry_scr[0:1, :]
    rank0 = jnp.sum(oh0 * seen, axis=1, keepdims=True)
    rank1 = jnp.sum(oh1 * seen, axis=1, keepdims=True)
    total = seen[tm - 1:tm, :] + oh[tm - 1:tm, :]
    carry_scr[...] = jnp.broadcast_to(total, carry_scr.shape)
    cnt_ref[...] = jnp.broadcast_to(total, cnt_ref.shape).astype(jnp.int32)
    vals = jnp.where(lane_i == 0, i1, jnp.where(lane_i == 1, i2, jnp.where(lane_i == 2, rank0,
                     jnp.where(lane_i == 3, rank1, 0.0))))
    route_ref[...] = vals.astype(jnp.int32)


def _outproj_router(oa, ob, wo_bf, x2d, norm_g, whi, wlo, b_pad):
    rows = x2d.shape[0]
    tm = ROUTE_TM
    return pl.pallas_call(
        _router_kernel,
        grid=(rows // tm,),
        in_specs=[
            pl.BlockSpec((N_HEADS, tm, HEAD_DIM), lambda i: (0, i, 0)),
            pl.BlockSpec((N_HEADS, tm, HEAD_DIM), lambda i: (0, i, 0)),
            pl.BlockSpec((2 * D_GROUP, D_MODEL), lambda i: (0, 0)),
            pl.BlockSpec((tm, D_MODEL), lambda i: (i, 0)),
            pl.BlockSpec((1, D_MODEL), lambda i: (0, 0)),
            pl.BlockSpec((D_MODEL, LANES), lambda i: (0, 0)),
            pl.BlockSpec((D_MODEL, LANES), lambda i: (0, 0)),
            pl.BlockSpec((1, LANES), lambda i: (0, 0)),
        ],
        out_specs=[
            pl.BlockSpec((tm, D_MODEL), lambda i: (i, 0)),
            pl.BlockSpec((tm, D_PACKED), lambda i: (i, 0)),
            pl.BlockSpec((tm, LANES), lambda i: (i, 0)),
            pl.BlockSpec((tm, LANES), lambda i: (i, 0)),
            pl.BlockSpec((tm, LANES), lambda i: (i, 0)),
            pl.BlockSpec((8, LANES), lambda i: (0, 0)),
        ],
        out_shape=[
            jax.ShapeDtypeStruct((rows, D_MODEL), F32),
            jax.ShapeDtypeStruct((rows, D_PACKED), jnp.uint32),
            jax.ShapeDtypeStruct((rows, LANES), jnp.int32),
            jax.ShapeDtypeStruct((rows, LANES), F32),
            jax.ShapeDtypeStruct((rows, LANES), F32),
            jax.ShapeDtypeStruct((8, LANES), jnp.int32),
        ],
        scratch_shapes=[pltpu.VMEM((8, LANES), F32)],
        compiler_params=_cparams(("arbitrary",)),
        name="outproj_router",
    )(oa, ob, wo_bf, x2d, norm_g, whi, wlo, b_pad)


MOVE_TM = 256


def _scatter_kernel(pos_ref, u_ref, xs_ref, sem):
    base = pl.program_id(0) * MOVE_TM

    for t in range(MOVE_TM):
        for k in range(TOP_K):
            dst = pos_ref[(base + t) * TOP_K + k]
            pltpu.make_async_copy(u_ref.at[pl.ds(t, 1)], xs_ref.at[pl.ds(dst, 1)], sem).start(priority=k)
    for _ in range(TOP_K):
        pltpu.make_async_copy(u_ref, xs_ref.at[pl.ds(0, MOVE_TM)], sem).wait()


def _scatter_rows(pos_flat, u2):
    n_tok = u2.shape[0]
    return pl.pallas_call(
        _scatter_kernel,
        grid_spec=pltpu.PrefetchScalarGridSpec(
            num_scalar_prefetch=1,
            grid=(n_tok // MOVE_TM,),
            in_specs=[pl.BlockSpec((MOVE_TM, D_PACKED), lambda i, pos: (i, 0))],
            out_specs=pl.BlockSpec(memory_space=pl.ANY),
            scratch_shapes=[pltpu.SemaphoreType.DMA(())],
        ),
        out_shape=jax.ShapeDtypeStruct((n_tok * TOP_K, D_PACKED), u2.dtype),
        compiler_params=_cparams(("arbitrary",)),
        name="scatter_rows",
    )(pos_flat, u2)


FFN_TM = 128
WEIGHT_DMA_PRIORITY = 1


def _ffn_kernel(tile_ref, exp_ref, lo_ref, hi_ref, first_ref, last_ref, new_ref, slot_ref, next_ref,
                x_ref, w1_hbm, w3_hbm, w2_hbm, o_ref,
                xb_scr, acc_scr, w1_buf, w3_buf, w2_buf, w1_bf, w3_bf, w2_bf, sem):
    it = pl.program_id(0)
    lo = lo_ref[it]
    hi = hi_ref[it]
    slot = slot_ref[it]

    def weight_copies(expert, s):
        return (pltpu.make_async_copy(w1_hbm.at[expert], w1_buf.at[s], sem.at[s, 0]),
                pltpu.make_async_copy(w3_hbm.at[expert], w3_buf.at[s], sem.at[s, 1]),
                pltpu.make_async_copy(w2_hbm.at[expert], w2_buf.at[s], sem.at[s, 2]))

    @pl.when(it == 0)
    def _():
        for cp in weight_copies(exp_ref[0], slot):
            cp.start(priority=WEIGHT_DMA_PRIORITY)

    @pl.when(new_ref[it] == 1)
    def _():
        for cp in weight_copies(exp_ref[it], slot):
            cp.wait()

        @pl.when(next_ref[it] >= 0)
        def _():
            for cp in weight_copies(next_ref[it], 1 - slot):
                cp.start(priority=WEIGHT_DMA_PRIORITY)

        w1_bf[...] = w1_buf[slot].astype(BF16)
        w3_bf[...] = w3_buf[slot].astype(BF16)
        w2_bf[...] = w2_buf[slot].astype(BF16)

    @pl.when(first_ref[it] == 1)
    def _():
        xb_scr[...] = _unpack_bf16_pairs(x_ref[...]).astype(BF16)
        acc_scr[...] = jnp.zeros_like(acc_scr)

    @pl.when(hi > lo)
    def _():
        x = xb_scr[...]
        a = jnp.dot(x, w1_bf[...], preferred_element_type=F32)
        b = jnp.dot(x, w3_bf[...], preferred_element_type=F32)
        mid = a / (1.0 + jnp.exp(-a)) * b
        rows = tile_ref[it] * FFN_TM + lax.broadcasted_iota(jnp.int32, mid.shape, 0)
        mid = jnp.where(jnp.logical_and(rows >= lo, rows < hi), mid, 0.0).astype(BF16)
        acc_scr[...] += jnp.dot(mid, w2_bf[...], preferred_element_type=F32)

    @pl.when(last_ref[it] == 1)
    def _():
        o_ref[...] = _pack_bf16_pairs(acc_scr[...])


def _grouped_ffn(items, xs, w1, w3, w2):
    n_items = items[0].shape[0]
    n_rows = xs.shape[0]

    def tile_map(i, t, *_):
        return (t[i], 0)

    return pl.pallas_call(
        _ffn_kernel,
        grid_spec=pltpu.PrefetchScalarGridSpec(
            num_scalar_prefetch=len(items),
            grid=(n_items,),
            in_specs=[
                pl.BlockSpec((FFN_TM, D_PACKED), tile_map),
                pl.BlockSpec(memory_space=pl.ANY),
                pl.BlockSpec(memory_space=pl.ANY),
                pl.BlockSpec(memory_space=pl.ANY),
            ],
            out_specs=pl.BlockSpec((FFN_TM, D_PACKED), tile_map),
            scratch_shapes=[
                pltpu.VMEM((FFN_TM, D_MODEL), BF16),
                pltpu.VMEM((FFN_TM, D_MODEL), F32),
                pltpu.VMEM((2, D_MODEL, D_EXPERT), F32),
                pltpu.VMEM((2, D_MODEL, D_EXPERT), F32),
                pltpu.VMEM((2, D_EXPERT, D_MODEL), F32),
                pltpu.VMEM((D_MODEL, D_EXPERT), BF16),
                pltpu.VMEM((D_MODEL, D_EXPERT), BF16),
                pltpu.VMEM((D_EXPERT, D_MODEL), BF16),
                pltpu.SemaphoreType.DMA((2, 3)),
            ],
        ),
        out_shape=jax.ShapeDtypeStruct((n_rows, D_PACKED), jnp.uint32),
        compiler_params=_cparams(("arbitrary",)),
        name="grouped_ffn",
    )(*items, xs, w1, w3, w2)


def _combine_kernel(pos_ref, h_ref, w0_ref, w1_ref, ys_ref, o_ref, ybuf, sem):
    step = pl.program_id(0)
    n_steps = pl.num_programs(0)

    def fetch(s, slot):
        for t in range(MOVE_TM):
            for k in range(TOP_K):
                src = pos_ref[(s * MOVE_TM + t) * TOP_K + k]
                pltpu.make_async_copy(ys_ref.at[pl.ds(src, 1)], ybuf.at[slot, k, pl.ds(t, 1)],
                                      sem.at[slot]).start(priority=k)

    @pl.when(step == 0)
    def _():
        fetch(0, 0)

    slot = step % 2

    @pl.when(step + 1 < n_steps)
    def _():
        fetch(step + 1, 1 - slot)

    for k in range(TOP_K):
        pltpu.make_async_copy(ys_ref.at[pl.ds(0, MOVE_TM)], ybuf.at[slot, k], sem.at[slot]).wait()
    reps = D_MODEL // LANES
    w0 = jnp.concatenate([w0_ref[...]] * reps, axis=1)
    w1 = jnp.concatenate([w1_ref[...]] * reps, axis=1)
    o_ref[...] = (h_ref[...] + w0 * _unpack_bf16_pairs(ybuf[slot, 0])
                  + w1 * _unpack_bf16_pairs(ybuf[slot, 1]))


def _combine(pos_flat, h1, w0b, w1b, ys):
    rows = h1.shape[0]
    tm = MOVE_TM
    return pl.pallas_call(
        _combine_kernel,
        grid_spec=pltpu.PrefetchScalarGridSpec(
            num_scalar_prefetch=1,
            grid=(rows // tm,),
            in_specs=[
                pl.BlockSpec((tm, D_MODEL), lambda i, pos: (i, 0)),
                pl.BlockSpec((tm, LANES), lambda i, pos: (i, 0)),
                pl.BlockSpec((tm, LANES), lambda i, pos: (i, 0)),
                pl.BlockSpec(memory_space=pl.ANY),
            ],
            out_specs=pl.BlockSpec((tm, D_MODEL), lambda i, pos: (i, 0)),
            scratch_shapes=[pltpu.VMEM((2, TOP_K, tm, D_PACKED), jnp.uint32), pltpu.SemaphoreType.DMA((2,))],
        ),
        out_shape=jax.ShapeDtypeStruct((rows, D_MODEL), F32),
        compiler_params=_cparams(("arbitrary",)),
        name="combine",
    )(pos_flat, h1, w0b, w1b, ys)


def _work_items(counts, n_assign):
    ends = jnp.cumsum(counts)
    starts = (ends - counts).astype(jnp.int32)
    n_tiles = n_assign // FFN_TM
    tile_starts = jnp.arange(n_tiles, dtype=jnp.int32) * FFN_TM
    seg_lo = jnp.sort(jnp.concatenate([tile_starts, starts]))
    seg_hi = jnp.concatenate([seg_lo[1:], jnp.array([n_assign], jnp.int32)])
    item_tile = jnp.minimum(seg_lo // FFN_TM, n_tiles - 1).astype(jnp.int32)
    item_exp = jnp.minimum(jnp.sum((ends[None, :] <= seg_lo[:, None]).astype(jnp.int32), axis=1), N_EXPERTS - 1)
    item_exp = item_exp.astype(jnp.int32)
    item_first = (seg_lo == item_tile * FFN_TM).astype(jnp.int32)
    n_items = seg_lo.shape[0]
    item_new = jnp.concatenate([jnp.ones((1,), jnp.int32), (item_exp[1:] != item_exp[:-1]).astype(jnp.int32)])
    ordinal = jnp.cumsum(item_new) - 1
    run_exp = jnp.full((n_items + 1,), -1, jnp.int32).at[ordinal].set(item_exp)
    item_next = run_exp[ordinal + 1]
    item_slot = (ordinal % 2).astype(jnp.int32)
    item_last = (seg_hi == (item_tile + 1) * FFN_TM).astype(jnp.int32)
    items = (item_tile, item_exp, seg_lo, seg_hi, item_first, item_last, item_new, item_slot, item_next)
    return starts, items


def kernel(x, meta_tokens, norm1_g, w_in, b_f, q_gain, k_gain, fox_out_gain, sb_out_gain, w_out, norm2_g,
           w_coarse, b_coarse, w_fine, b_fine, w1, w3, w2):
    assert norm1_g.shape[0] == 1, "single-layer block"
    n_batch, seq, _ = x.shape
    n_tok = n_batch * seq
    n_assign = n_tok * TOP_K
    scale = HEAD_DIM ** -0.5
    x2d = x.reshape(n_tok, D_MODEL)

    w_in_t = w_in[0].T
    w_in_bf = _cast_bf16(w_in_t, D_PROJ)
    w_out_bf = _cast_bf16(w_out[0], 2 * D_GROUP)
    wft_bf = jnp.pad(w_in_t[D_PROJ:], ((0, LANES - N_HEADS), (0, 0))).astype(BF16)
    bf_pad = jnp.pad(b_f[0], (0, LANES - N_HEADS)).reshape(1, LANES)
    ones = jnp.ones((HEAD_DIM,), F32)
    zeros = jnp.zeros((HEAD_DIM,), F32)
    gains = jnp.stack([jnp.stack([q_gain[0] * scale, ones]), jnp.stack([k_gain[0], ones]),
                       jnp.stack([ones, zeros]), jnp.stack([ones, zeros]), jnp.stack([ones * scale, zeros]),
                       jnp.stack([ones, zeros]), jnp.stack([ones, zeros])])
    meta_pad = jnp.pad(meta_tokens.astype(F32), ((0, BLK - N_META), (0, 0)))
    n1 = norm1_g[0].reshape(1, D_MODEL)

    proj, lf = _inproj(x2d, n1, w_in_bf, wft_bf, bf_pad, gains, tm=1024)
    proj_m, lf_m = _inproj(meta_pad, n1, w_in_bf, wft_bf, bf_pad, gains, tm=BLK)
    qx, kx, kxm = _cumgate(lf_m, lf, n_batch, seq)
    qk_bound = 1.02 * HEAD_DIM * scale * jnp.max(jnp.abs(q_gain[0])) * jnp.max(jnp.abs(k_gain[0]))
    bound = (2.0 * qk_bound).reshape(1).astype(F32)
    oa = _fox_attention(bound, proj, proj_m, qx, kx, kxm, fox_out_gain[0].reshape(N_HEADS, 1, HEAD_DIM),
                        n_batch, seq)
    ob = _sb_attention(proj, proj_m, sb_out_gain[0].reshape(N_HEADS, 1, HEAD_DIM), n_batch, seq)

    wr = jnp.pad(jnp.concatenate([w_fine[0], w_coarse[0]], axis=1),
                 ((0, 0), (0, LANES - N_GROUPS - N_EXPERTS)))
    wr_hi = wr.astype(BF16)
    wr_lo = (wr - wr_hi.astype(F32)).astype(BF16)
    br = jnp.pad(jnp.concatenate([b_fine[0], b_coarse[0]]), (0, LANES - N_GROUPS - N_EXPERTS)).reshape(1, LANES)
    h1, u2, route, w0b, w1b, cnt = _outproj_router(oa, ob, w_out_bf, x2d, norm2_g[0].reshape(1, D_MODEL),
                                                   wr_hi, wr_lo, br)
    starts, items = _work_items(cnt[0, :N_EXPERTS], n_assign)
    eid = route[:, 0:TOP_K]
    start_of = jnp.sum(jnp.where(eid[..., None] == jnp.arange(N_EXPERTS, dtype=jnp.int32), starts, 0), axis=-1)
    pos = (start_of + route[:, TOP_K:2 * TOP_K]).reshape(-1)
    xs = _scatter_rows(pos, u2)
    ys = _grouped_ffn(items, xs, w1[0], w3[0], w2[0])
    out = _combine(pos, h1, w0b, w1b, ys)
    return out.reshape(n_batch, seq, D_MODEL)
```

```python
import functools

import jax
import jax.numpy as jnp
from jax import lax
from jax.experimental import pallas as pl
from jax.experimental.pallas import tpu as pltpu

F32 = jnp.float32
BF16 = jnp.bfloat16

D_MODEL = 2048
N_META = 16
HEAD_DIM = 128
N_HEADS = 8
D_GROUP = N_HEADS * HEAD_DIM
N_PROJ_GROUPS = 7
D_PROJ = N_PROJ_GROUPS * D_GROUP
N_GROUPS = 8
EXPERTS_PER_GROUP = 8
N_EXPERTS = 64
TOP_K = 2
D_EXPERT = 512
EPS = 1e-6
LANES = 128
D_PACKED = D_MODEL // 2
MXU_HEADS = 2
BLK = 128
SKIP_LOG = 88.0
VMEM_LIMIT = 56 * 1024 * 1024


def _cparams(sem, vmem=VMEM_LIMIT):
    return pltpu.CompilerParams(dimension_semantics=sem, vmem_limit_bytes=vmem)


def _log_sigmoid(x):
    return jnp.minimum(x, 0.0) - jnp.log(1.0 + jnp.exp(-jnp.abs(x)))


def _split3(x):
    hi = x.astype(BF16)
    r1 = x - hi.astype(F32)
    mid = r1.astype(BF16)
    lo = (r1 - mid.astype(F32)).astype(BF16)
    return hi, mid, lo


def _dot_nt(a, b):
    return lax.dot_general(a, b, (((1,), (1,)), ((), ())), preferred_element_type=F32)


def _pack_bf16_pairs(x):
    half = x.shape[1] // 2
    xr = x.astype(BF16).astype(F32)
    hi = lax.bitcast_convert_type(xr[:, :half], jnp.uint32)
    lo = lax.bitcast_convert_type(xr[:, half:], jnp.uint32)
    return hi | lax.shift_right_logical(lo, jnp.uint32(16))


def _unpack_bf16_pairs(p):
    hi = lax.bitcast_convert_type(p & jnp.uint32(0xFFFF0000), F32)
    lo = lax.bitcast_convert_type(lax.shift_left(p, jnp.uint32(16)), F32)
    return jnp.concatenate([hi, lo], axis=1)


def _cast_kernel(x_ref, o_ref):
    o_ref[...] = x_ref[...].astype(o_ref.dtype)


def _cast_bf16(w, n_rows, tr=512):
    n_cols = w.shape[1]
    return pl.pallas_call(
        _cast_kernel,
        grid=(n_rows // tr,),
        in_specs=[pl.BlockSpec((tr, n_cols), lambda i: (i, 0))],
        out_specs=pl.BlockSpec((tr, n_cols), lambda i: (i, 0)),
        out_shape=jax.ShapeDtypeStruct((n_rows, n_cols), BF16),
        compiler_params=_cparams(("parallel",)),
        name="cast_bf16",
    )(w)


def _inproj_kernel(x_ref, g_ref, wt_ref, wft_ref, bf_ref, gain_ref, o_ref, lf_ref, u_ref, *, heads_per_tile):
    j = pl.program_id(1)

    @pl.when(j == 0)
    def _():
        x = x_ref[...]
        ms = jnp.mean(x * x, axis=-1, keepdims=True)
        u = (x * lax.rsqrt(ms + EPS) * g_ref[...]).astype(BF16)
        u_ref[...] = u
        f = _dot_nt(u, wft_ref[...]) + bf_ref[...]
        lf_ref[...] = _log_sigmoid(f)

    gain = gain_ref[0, 0:1, :]
    normed = gain_ref[0, 1:2, :] > 0.5
    u = u_ref[...]
    for s in range(heads_per_tile // MXU_HEADS):
        cols = pl.ds(s * MXU_HEADS * HEAD_DIM, MXU_HEADS * HEAD_DIM)
        acc = _dot_nt(u, wt_ref[cols, :])
        for hh in range(MXU_HEADS):
            y = acc[:, hh * HEAD_DIM:(hh + 1) * HEAD_DIM]
            ms = jnp.mean(y * y, axis=-1, keepdims=True)
            scale = jnp.where(normed, lax.rsqrt(ms + EPS), 1.0)
            o_ref[s * MXU_HEADS + hh] = (y * scale * gain).astype(BF16)


def _inproj(x2d, norm_g, wt_bf, wft_bf, bf_pad, gains, tm, tn=D_GROUP):
    rows = x2d.shape[0]
    hpt = tn // HEAD_DIM
    kern = functools.partial(_inproj_kernel, heads_per_tile=hpt)
    return pl.pallas_call(
        kern,
        grid=(rows // tm, D_PROJ // tn),
        in_specs=[
            pl.BlockSpec((tm, D_MODEL), lambda i, j: (i, 0)),
            pl.BlockSpec((1, D_MODEL), lambda i, j: (0, 0)),
            pl.BlockSpec((tn, D_MODEL), lambda i, j: (j, 0)),
            pl.BlockSpec((LANES, D_MODEL), lambda i, j: (0, 0)),
            pl.BlockSpec((1, LANES), lambda i, j: (0, 0)),
            pl.BlockSpec((1, 2, HEAD_DIM), lambda i, j: (j // (N_HEADS // hpt), 0, 0)),
        ],
        out_specs=[
            pl.BlockSpec((hpt, tm, HEAD_DIM), lambda i, j: (j, i, 0)),
            pl.BlockSpec((tm, LANES), lambda i, j: (i, 0)),
        ],
        out_shape=[
            jax.ShapeDtypeStruct((D_PROJ // HEAD_DIM, rows, HEAD_DIM), BF16),
            jax.ShapeDtypeStruct((rows, LANES), F32),
        ],
        scratch_shapes=[pltpu.VMEM((tm, D_MODEL), BF16)],
        compiler_params=_cparams(("parallel", "arbitrary")),
        name="inproj",
    )(x2d, norm_g, wt_bf, wft_bf, bf_pad, gains)


def _cumgate_kernel(lfm_ref, lf_ref, qx_ref, kx_ref, kxm_ref, *, n_blk):
    row = lax.broadcasted_iota(jnp.int32, (BLK, BLK), 0)
    col = lax.broadcasted_iota(jnp.int32, (BLK, BLK), 1)
    tri = (col <= row).astype(BF16)
    one = jnp.ones((BLK, LANES), F32)
    zero = jnp.zeros((BLK, LANES), F32)

    def prefix(x):
        hi, mid, lo = _split3(x)
        return (jnp.dot(tri, hi, preferred_element_type=F32)
                + jnp.dot(tri, mid, preferred_element_type=F32)
                + jnp.dot(tri, lo, preferred_element_type=F32))

    def ext(cum, h):
        c = jnp.broadcast_to(cum[:, h:h + 1], (BLK, LANES))
        hi, mid, lo = (t.astype(F32) for t in _split3(c))
        qx = jnp.where(col == 0, hi, jnp.where(col == 1, mid, jnp.where(col == 2, lo,
                       jnp.where(col < 6, one, zero))))
        kx = jnp.where(col < 3, one, jnp.where(col == 3, -hi, jnp.where(col == 4, -mid,
                       jnp.where(col == 5, -lo, zero))))
        return qx.astype(BF16), kx.astype(BF16)

    lfm = jnp.where(row < N_META, lfm_ref[...], 0.0)
    cum_m = prefix(lfm)
    for h in range(N_HEADS):
        _, kx = ext(cum_m, h)
        kxm_ref[h] = kx
    carry = cum_m[BLK - 1:BLK, :]
    for b in range(n_blk):
        cum = prefix(lf_ref[b * BLK:(b + 1) * BLK, :]) + carry
        carry = cum[BLK - 1:BLK, :]
        for h in range(N_HEADS):
            qx, kx = ext(cum, h)
            qx_ref[h, b * BLK:(b + 1) * BLK, :] = qx
            kx_ref[h, b * BLK:(b + 1) * BLK, :] = kx


def _cumgate(lf_meta, lf_real, n_batch, seq):
    kern = functools.partial(_cumgate_kernel, n_blk=seq // BLK)
    return pl.pallas_call(
        kern,
        grid=(n_batch,),
        in_specs=[
            pl.BlockSpec((BLK, LANES), lambda b: (0, 0)),
            pl.BlockSpec((seq, LANES), lambda b: (b, 0)),
        ],
        out_specs=[
            pl.BlockSpec((N_HEADS, seq, LANES), lambda b: (0, b, 0)),
            pl.BlockSpec((N_HEADS, seq, LANES), lambda b: (0, b, 0)),
            pl.BlockSpec((N_HEADS, BLK, LANES), lambda b: (0, 0, 0)),
        ],
        out_shape=[
            jax.ShapeDtypeStruct((N_HEADS, n_batch * seq, LANES), BF16),
            jax.ShapeDtypeStruct((N_HEADS, n_batch * seq, LANES), BF16),
            jax.ShapeDtypeStruct((N_HEADS, BLK, LANES), BF16),
        ],
        compiler_params=_cparams(("arbitrary",)),
        name="cumgate",
    )(lf_meta, lf_real)


def _bdot_nt(a, b):
    return lax.dot_general(a, b, (((2,), (2,)), ((0,), (0,))), preferred_element_type=F32)


def _bdot_nn(a, b):
    return lax.dot_general(a, b, (((2,), (1,)), ((0,), (0,))), preferred_element_type=F32)


def _fox_kernel(bound_ref, q_ref, qx_ref, k_ref, kx_ref, v_ref, g_ref, km_ref, kxm_ref, vm_ref, gain_ref,
                o_ref, m_scr, l_scr, acc_scr):
    i = pl.program_id(1)

    def sweep(kb, kxb, vb, mask, first):
        qa = jnp.concatenate([q_ref[...], qx_ref[...]], axis=2)
        ka = jnp.concatenate([kb, kxb], axis=2)
        s = _bdot_nt(qa, ka)
        if mask is not None:
            s = jnp.where(mask[None], s, -jnp.inf)
        m_cur = jnp.max(s, axis=2, keepdims=True)
        v1 = jnp.concatenate([vb, jnp.ones_like(vb)], axis=2)
        if first:
            m_col = m_cur
            pv = _bdot_nn(jnp.exp(s - m_col).astype(BF16), v1)
            l_scr[...] = pv[:, :, BLK:]
            acc_scr[...] = pv[:, :, :BLK]
            m_scr[...] = jnp.broadcast_to(m_col, m_scr.shape)
        else:
            m_prev = m_scr[...]
            m_new = jnp.maximum(m_prev, m_cur)
            m_col = m_new[:, :, 0:1]
            alpha = jnp.exp(m_prev - m_new)
            pv = _bdot_nn(jnp.exp(s - m_new).astype(BF16), v1)
            l_scr[...] = alpha * l_scr[...] + pv[:, :, BLK:]
            acc_scr[...] = alpha * acc_scr[...] + pv[:, :, :BLK]
            m_scr[...] = m_new
        return jnp.max(s[:, :, 0:1] - m_col)

    def real_block(start, width):
        start = pl.multiple_of(start, BLK)
        return (k_ref[:, pl.ds(start, width), :], kx_ref[:, pl.ds(start, width), :],
                v_ref[:, pl.ds(start, width), :])

    bound = bound_ref[0]
    first_start = jnp.maximum(i - 1, 0) * BLK
    row2 = lax.broadcasted_iota(jnp.int32, (BLK, 2 * BLK), 0)
    col2 = lax.broadcasted_iota(jnp.int32, (BLK, 2 * BLK), 1)
    causal = col2 + (first_start - i * BLK) <= row2
    gap0 = sweep(*real_block(first_start, 2 * BLK), causal, True)

    def cond(c):
        j, done = c
        return jnp.logical_and(j >= 0, done == 0)

    def body(c):
        j, _ = c
        gap = sweep(*real_block(j * BLK, BLK), None, False)
        return j - 1, (gap + bound < -SKIP_LOG).astype(jnp.int32)

    _, done = lax.while_loop(cond, body, (i - 2, (gap0 + bound < -SKIP_LOG).astype(jnp.int32)))

    @pl.when(done == 0)
    def _():
        col = lax.broadcasted_iota(jnp.int32, (BLK, BLK), 1)
        sweep(km_ref[...], kxm_ref[...], vm_ref[...], col < N_META, False)

    o = acc_scr[...] / l_scr[...]
    ms = jnp.mean(o * o, axis=-1, keepdims=True)
    gate = 1.0 / (1.0 + jnp.exp(-g_ref[...].astype(F32)))
    o_ref[...] = (o * lax.rsqrt(ms + EPS) * gain_ref[...] * gate).astype(BF16)


def _sb_kernel(q_ref, k_ref, v_ref, km_ref, vm_ref, gain_ref, o_ref, carry_scr, acc_scr):
    i = pl.program_id(1)

    def suffix_operator(width):
        r = lax.broadcasted_iota(jnp.int32, (width, width + BLK), 0)
        c = lax.broadcasted_iota(jnp.int32, (width, width + BLK), 1)
        return jnp.logical_or(c >= width, r > c).astype(BF16)

    def sweep(kb, vb, mask, first):
        width = kb.shape[1]
        suffix = suffix_operator(width)
        z = _bdot_nt(q_ref[...], kb)
        sp = jnp.maximum(z, 0.0) + jnp.log(1.0 + jnp.exp(-jnp.abs(z)))
        lk = -sp
        if mask is not None:
            lk = jnp.where(mask[None], lk, 0.0)
        hi = lk.astype(BF16)
        lo = (lk - hi.astype(F32)).astype(BF16)
        t = (jnp.dot(hi.reshape(N_HEADS * BLK, width), suffix, preferred_element_type=F32)
             + jnp.dot(lo.reshape(N_HEADS * BLK, width), suffix, preferred_element_type=F32))
        t = t.reshape(N_HEADS, BLK, width + BLK)
        later = t[:, :, :width]
        rowsum = t[:, :, width:]
        if not first:
            later = later + carry_scr[...]
        a = jnp.exp(z - sp + later)
        if mask is not None:
            a = jnp.where(mask[None], a, 0.0)
        pv = _bdot_nn(a.astype(BF16), vb)
        if first:
            acc_scr[...] = pv
            c_new = rowsum
        else:
            acc_scr[...] = acc_scr[...] + pv
            c_new = carry_scr[...] + rowsum
        carry_scr[...] = c_new
        return jnp.max(c_new[:, :, 0:1])

    def real_block(start, width):
        start = pl.multiple_of(start, BLK)
        return k_ref[:, pl.ds(start, width), :], v_ref[:, pl.ds(start, width), :]

    first_start = jnp.maximum(i - 1, 0) * BLK
    row2 = lax.broadcasted_iota(jnp.int32, (BLK, 2 * BLK), 0)
    col2 = lax.broadcasted_iota(jnp.int32, (BLK, 2 * BLK), 1)
    strict = col2 + (first_start - i * BLK) < row2
    top0 = sweep(*real_block(first_start, 2 * BLK), strict, True)

    def cond(c):
        j, done = c
        return jnp.logical_and(j >= 0, done == 0)

    def body(c):
        j, _ = c
        top = sweep(*real_block(j * BLK, BLK), None, False)
        return j - 1, (top < -SKIP_LOG).astype(jnp.int32)

    _, done = lax.while_loop(cond, body, (i - 2, (top0 < -SKIP_LOG).astype(jnp.int32)))

    @pl.when(done == 0)
    def _():
        col = lax.broadcasted_iota(jnp.int32, (BLK, BLK), 1)
        sweep(km_ref[...], vm_ref[...], col < N_META, False)

    o = acc_scr[...]
    ms = jnp.mean(o * o, axis=-1, keepdims=True)
    o_ref[...] = (o * lax.rsqrt(ms + EPS) * gain_ref[...]).astype(BF16)


def _head_spec_q(group, nq):
    return pl.BlockSpec((N_HEADS, BLK, HEAD_DIM), lambda b, i: (group, b * nq + i, 0))


def _head_spec_kv(group, seq):
    return pl.BlockSpec((N_HEADS, seq, HEAD_DIM), lambda b, i: (group, b, 0))


def _head_spec_meta(group):
    return pl.BlockSpec((N_HEADS, BLK, HEAD_DIM), lambda b, i: (group, 0, 0))


def _fox_attention(bound, proj, proj_m, qx, kx, kxm, out_gain, n_batch, seq):
    nq = seq // BLK
    rows = n_batch * seq
    return pl.pallas_call(
        _fox_kernel,
        grid=(n_batch, nq),
        in_specs=[
            pl.BlockSpec(memory_space=pltpu.SMEM),
            _head_spec_q(0, nq),
            _head_spec_q(0, nq),
            _head_spec_kv(1, seq),
            _head_spec_kv(0, seq),
            _head_spec_kv(2, seq),
            _head_spec_q(3, nq),
            _head_spec_meta(1),
            _head_spec_meta(0),
            _head_spec_meta(2),
            pl.BlockSpec((N_HEADS, 1, HEAD_DIM), lambda b, i: (0, 0, 0)),
        ],
        out_specs=pl.BlockSpec((N_HEADS, BLK, HEAD_DIM), lambda b, i: (0, b * nq + i, 0)),
        out_shape=jax.ShapeDtypeStruct((N_HEADS, rows, HEAD_DIM), BF16),
        scratch_shapes=[pltpu.VMEM((N_HEADS, BLK, LANES), F32)] * 3,
        compiler_params=_cparams(("parallel", "arbitrary")),
        name="fox_attention",
    )(bound, proj, qx, proj, kx, proj, proj, proj_m, kxm, proj_m, out_gain)


def _sb_attention(proj, proj_m, out_gain, n_batch, seq):
    nq = seq // BLK
    rows = n_batch * seq
    return pl.pallas_call(
        _sb_kernel,
        grid=(n_batch, nq),
        in_specs=[
            _head_spec_q(4, nq),
            _head_spec_kv(5, seq),
            _head_spec_kv(6, seq),
            _head_spec_meta(5),
            _head_spec_meta(6),
            pl.BlockSpec((N_HEADS, 1, HEAD_DIM), lambda b, i: (0, 0, 0)),
        ],
        out_specs=pl.BlockSpec((N_HEADS, BLK, HEAD_DIM), lambda b, i: (0, b * nq + i, 0)),
        out_shape=jax.ShapeDtypeStruct((N_HEADS, rows, HEAD_DIM), BF16),
        scratch_shapes=[pltpu.VMEM((N_HEADS, BLK, LANES), F32)] * 2,
        compiler_params=_cparams(("parallel", "arbitrary")),
        name="sb_attention",
    )(proj, proj, proj, proj_m, proj_m, out_gain)


ROUTE_TM = 256


def _router_kernel(oa_ref, ob_ref, wo_ref, x_ref, g_ref, whi_ref, wlo_ref, b_ref,
                   h_ref, u_ref, route_ref, w0_ref, w1_ref, cnt_ref, carry_scr):
    step = pl.program_id(0)
    lhs = jnp.concatenate([oa_ref[h] for h in range(N_HEADS)] + [ob_ref[h] for h in range(N_HEADS)], axis=1)
    x = x_ref[...] + jnp.dot(lhs, wo_ref[...], preferred_element_type=F32)
    h_ref[...] = x
    tm = x.shape[0]
    ms = jnp.mean(x * x, axis=-1, keepdims=True)
    u = x * lax.rsqrt(ms + EPS) * g_ref[...]
    u_ref[...] = _pack_bf16_pairs(u)
    uhi = u.astype(BF16)
    ulo = (u - uhi.astype(F32)).astype(BF16)
    logits = (jnp.dot(uhi, whi_ref[...], preferred_element_type=F32)
              + jnp.dot(uhi, wlo_ref[...], preferred_element_type=F32)
              + jnp.dot(ulo, whi_ref[...], preferred_element_type=F32)) + b_ref[...]
    lane_i = lax.broadcasted_iota(jnp.int32, (tm, LANES), 1)
    lane = lane_i.astype(F32)
    big = float(4 * LANES)
    neg = -jnp.inf
    c = jnp.where(jnp.logical_and(lane_i >= N_EXPERTS, lane_i < N_EXPERTS + N_GROUPS), logits, neg)
    cmax = jnp.max(c, axis=1, keepdims=True)
    g_sel = jnp.min(jnp.where(c == cmax, lane, big), axis=1, keepdims=True) - N_EXPERTS
    g_gate = 1.0 / jnp.sum(jnp.exp(c - cmax), axis=1, keepdims=True)
    lo = g_sel * EXPERTS_PER_GROUP
    in_group = jnp.logical_and(lane >= lo, lane < lo + EXPERTS_PER_GROUP)
    f = jnp.where(in_group, logits, neg)
    t1 = jnp.max(f, axis=1, keepdims=True)
    i1 = jnp.min(jnp.where(f == t1, lane, big), axis=1, keepdims=True)
    f2 = jnp.where(lane == i1, neg, f)
    t2 = jnp.max(f2, axis=1, keepdims=True)
    i2 = jnp.min(jnp.where(f2 == t2, lane, big), axis=1, keepdims=True)
    d = jnp.exp(t2 - t1)
    w_first = g_gate / (1.0 + d)
    w0_ref[...] = jnp.broadcast_to(w_first, (tm, LANES))
    w1_ref[...] = jnp.broadcast_to(w_first * d, (tm, LANES))

    @pl.when(step == 0)
    def _():
        carry_scr[...] = jnp.zeros_like(carry_scr)

    oh0 = (lane == i1).astype(F32)
    oh1 = (lane == i2).astype(F32)
    oh = oh0 + oh1
    r = lax.broadcasted_iota(jnp.int32, (tm, tm), 0)
    cc = lax.broadcasted_iota(jnp.int32, (tm, tm), 1)
    before = (cc < r).astype(BF16)
    seen = jnp.dot(before, oh.astype(BF16), preferred_element_type=F32) + carry_scr[0:1, :]
    rank0 = jnp.sum(oh0 * seen, axis=1, keepdims=True)
    rank1 = jnp.sum(oh1 * seen, axis=1, keepdims=True)
    total = seen[tm - 1:tm, :] + oh[tm - 1:tm, :]
    carry_scr[...] = jnp.broadcast_to(total, carry_scr.shape)
    cnt_ref[...] = jnp.broadcast_to(total, cnt_ref.shape).astype(jnp.int32)
    vals = jnp.where(lane_i == 0, i1, jnp.where(lane_i == 1, i2, jnp.where(lane_i == 2, rank0,
                     jnp.where(lane_i == 3, rank1, 0.0))))
    route_ref[...] = vals.astype(jnp.int32)


def _outproj_router(oa, ob, wo_bf, x2d, norm_g, whi, wlo, b_pad):
    rows = x2d.shape[0]
    tm = ROUTE_TM
    return pl.pallas_call(
        _router_kernel,
        grid=(rows // tm,),
        in_specs=[
            pl.BlockSpec((N_HEADS, tm, HEAD_DIM), lambda i: (0, i, 0)),
            pl.BlockSpec((N_HEADS, tm, HEAD_DIM), lambda i: (0, i, 0)),
            pl.BlockSpec((2 * D_GROUP, D_MODEL), lambda i: (0, 0)),
            pl.BlockSpec((tm, D_MODEL), lambda i: (i, 0)),
            pl.BlockSpec((1, D_MODEL), lambda i: (0, 0)),
            pl.BlockSpec((D_MODEL, LANES), lambda i: (0, 0)),
            pl.BlockSpec((D_MODEL, LANES), lambda i: (0, 0)),
            pl.BlockSpec((1, LANES), lambda i: (0, 0)),
        ],
        out_specs=[
            pl.BlockSpec((tm, D_MODEL), lambda i: (i, 0)),
            pl.BlockSpec((tm, D_PACKED), lambda i: (i, 0)),
            pl.BlockSpec((tm, LANES), lambda i: (i, 0)),
            pl.BlockSpec((tm, LANES), lambda i: (i, 0)),
            pl.BlockSpec((tm, LANES), lambda i: (i, 0)),
            pl.BlockSpec((8, LANES), lambda i: (0, 0)),
        ],
        out_shape=[
            jax.ShapeDtypeStruct((rows, D_MODEL), F32),
            jax.ShapeDtypeStruct((rows, D_PACKED), jnp.uint32),
            jax.ShapeDtypeStruct((rows, LANES), jnp.int32),
            jax.ShapeDtypeStruct((rows, LANES), F32),
            jax.ShapeDtypeStruct((rows, LANES), F32),
            jax.ShapeDtypeStruct((8, LANES), jnp.int32),
        ],
        scratch_shapes=[pltpu.VMEM((8, LANES), F32)],
        compiler_params=_cparams(("arbitrary",)),
        name="outproj_router",
    )(oa, ob, wo_bf, x2d, norm_g, whi, wlo, b_pad)


MOVE_TM = 256


def _scatter_kernel(pos_ref, u_ref, xs_ref, sem):
    base = pl.program_id(0) * MOVE_TM

    for t in range(MOVE_TM):
        for k in range(TOP_K):
            dst = pos_ref[(base + t) * TOP_K + k]
            pltpu.make_async_copy(u_ref.at[pl.ds(t, 1)], xs_ref.at[pl.ds(dst, 1)], sem).start(priority=k)
    for _ in range(TOP_K):
        pltpu.make_async_copy(u_ref, xs_ref.at[pl.ds(0, MOVE_TM)], sem).wait()


def _scatter_rows(pos_flat, u2):
    n_tok = u2.shape[0]
    return pl.pallas_call(
        _scatter_kernel,
        grid_spec=pltpu.PrefetchScalarGridSpec(
            num_scalar_prefetch=1,
            grid=(n_tok // MOVE_TM,),
            in_specs=[pl.BlockSpec((MOVE_TM, D_PACKED), lambda i, pos: (i, 0))],
            out_specs=pl.BlockSpec(memory_space=pl.ANY),
            scratch_shapes=[pltpu.SemaphoreType.DMA(())],
        ),
        out_shape=jax.ShapeDtypeStruct((n_tok * TOP_K, D_PACKED), u2.dtype),
        compiler_params=_cparams(("arbitrary",)),
        name="scatter_rows",
    )(pos_flat, u2)


FFN_TM = 128


def _ffn_kernel(tile_ref, exp_ref, lo_ref, hi_ref, first_ref, last_ref, new_ref, slot_ref, next_ref,
                x_ref, w1_hbm, w3_hbm, w2_hbm, o_ref,
                xb_scr, acc_scr, w1_buf, w3_buf, w2_buf, w1_bf, w3_bf, w2_bf, sem):
    it = pl.program_id(0)
    lo = lo_ref[it]
    hi = hi_ref[it]
    slot = slot_ref[it]

    def weight_copies(expert, s):
        half = D_EXPERT // 2
        return ((pltpu.make_async_copy(w1_hbm.at[expert], w1_buf.at[s], sem.at[s, 0]), 0),
                (pltpu.make_async_copy(w3_hbm.at[expert], w3_buf.at[s], sem.at[s, 1]), 1),
                (pltpu.make_async_copy(w2_hbm.at[expert, pl.ds(0, half)], w2_buf.at[s, pl.ds(0, half)],
                                       sem.at[s, 2]), 0),
                (pltpu.make_async_copy(w2_hbm.at[expert, pl.ds(half, half)], w2_buf.at[s, pl.ds(half, half)],
                                       sem.at[s, 3]), 1))

    @pl.when(it == 0)
    def _():
        for cp, queue in weight_copies(exp_ref[0], slot):
            cp.start(priority=queue)

    @pl.when(new_ref[it] == 1)
    def _():
        @pl.when(next_ref[it] >= 0)
        def _():
            for cp, queue in weight_copies(next_ref[it], 1 - slot):
                cp.start(priority=queue)

        for cp, _ in weight_copies(exp_ref[it], slot):
            cp.wait()

        w1_bf[...] = w1_buf[slot].astype(BF16)
        w3_bf[...] = w3_buf[slot].astype(BF16)
        w2_bf[...] = w2_buf[slot].astype(BF16)

    @pl.when(first_ref[it] == 1)
    def _():
        xb_scr[...] = _unpack_bf16_pairs(x_ref[...]).astype(BF16)
        acc_scr[...] = jnp.zeros_like(acc_scr)

    @pl.when(hi > lo)
    def _():
        x = xb_scr[...]
        a = jnp.dot(x, w1_bf[...], preferred_element_type=F32)
        b = jnp.dot(x, w3_bf[...], preferred_element_type=F32)
        mid = a / (1.0 + jnp.exp(-a)) * b
        rows = tile_ref[it] * FFN_TM + lax.broadcasted_iota(jnp.int32, mid.shape, 0)
        mid = jnp.where(jnp.logical_and(rows >= lo, rows < hi), mid, 0.0).astype(BF16)
        acc_scr[...] += jnp.dot(mid, w2_bf[...], preferred_element_type=F32)

    @pl.when(last_ref[it] == 1)
    def _():
        o_ref[...] = _pack_bf16_pairs(acc_scr[...])


def _grouped_ffn(items, xs, w1, w3, w2):
    n_items = items[0].shape[0]
    n_rows = xs.shape[0]

    def tile_map(i, t, *_):
        return (t[i], 0)

    return pl.pallas_call(
        _ffn_kernel,
        grid_spec=pltpu.PrefetchScalarGridSpec(
            num_scalar_prefetch=len(items),
            grid=(n_items,),
            in_specs=[
                pl.BlockSpec((FFN_TM, D_PACKED), tile_map),
                pl.BlockSpec(memory_space=pl.ANY),
                pl.BlockSpec(memory_space=pl.ANY),
                pl.BlockSpec(memory_space=pl.ANY),
            ],
            out_specs=pl.BlockSpec((FFN_TM, D_PACKED), tile_map),
            scratch_shapes=[
                pltpu.VMEM((FFN_TM, D_MODEL), BF16),
                pltpu.VMEM((FFN_TM, D_MODEL), F32),
                pltpu.VMEM((2, D_MODEL, D_EXPERT), F32),
                pltpu.VMEM((2, D_MODEL, D_EXPERT), F32),
                pltpu.VMEM((2, D_EXPERT, D_MODEL), F32),
                pltpu.VMEM((D_MODEL, D_EXPERT), BF16),
                pltpu.VMEM((D_MODEL, D_EXPERT), BF16),
                pltpu.VMEM((D_EXPERT, D_MODEL), BF16),
                pltpu.SemaphoreType.DMA((2, 4)),
            ],
        ),
        out_shape=jax.ShapeDtypeStruct((n_rows, D_PACKED), jnp.uint32),
        compiler_params=_cparams(("arbitrary",)),
        name="grouped_ffn",
    )(*items, xs, w1, w3, w2)


def _combine_kernel(pos_ref, h_ref, w0_ref, w1_ref, ys_ref, o_ref, ybuf, sem):
    step = pl.program_id(0)
    n_steps = pl.num_programs(0)

    def fetch(s, slot):
        for t in range(MOVE_TM):
            for k in range(TOP_K):
                src = pos_ref[(s * MOVE_TM + t) * TOP_K + k]
                pltpu.make_async_copy(ys_ref.at[pl.ds(src, 1)], ybuf.at[slot, k, pl.ds(t, 1)],
                                      sem.at[slot]).start(priority=k)

    @pl.when(step == 0)
    def _():
        fetch(0, 0)

    slot = step % 2

    @pl.when(step + 1 < n_steps)
    def _():
        fetch(step + 1, 1 - slot)

    for k in range(TOP_K):
        pltpu.make_async_copy(ys_ref.at[pl.ds(0, MOVE_TM)], ybuf.at[slot, k], sem.at[slot]).wait()
    reps = D_MODEL // LANES
    w0 = jnp.concatenate([w0_ref[...]] * reps, axis=1)
    w1 = jnp.concatenate([w1_ref[...]] * reps, axis=1)
    o_ref[...] = (h_ref[...] + w0 * _unpack_bf16_pairs(ybuf[slot, 0])
                  + w1 * _unpack_bf16_pairs(ybuf[slot, 1]))


def _combine(pos_flat, h1, w0b, w1b, ys):
    rows = h1.shape[0]
    tm = MOVE_TM
    return pl.pallas_call(
        _combine_kernel,
        grid_spec=pltpu.PrefetchScalarGridSpec(
            num_scalar_prefetch=1,
            grid=(rows // tm,),
            in_specs=[
                pl.BlockSpec((tm, D_MODEL), lambda i, pos: (i, 0)),
                pl.BlockSpec((tm, LANES), lambda i, pos: (i, 0)),
                pl.BlockSpec((tm, LANES), lambda i, pos: (i, 0)),
                pl.BlockSpec(memory_space=pl.ANY),
            ],
            out_specs=pl.BlockSpec((tm, D_MODEL), lambda i, pos: (i, 0)),
            scratch_shapes=[pltpu.VMEM((2, TOP_K, tm, D_PACKED), jnp.uint32), pltpu.SemaphoreType.DMA((2,))],
        ),
        out_shape=jax.ShapeDtypeStruct((rows, D_MODEL), F32),
        compiler_params=_cparams(("arbitrary",)),
        name="combine",
    )(pos_flat, h1, w0b, w1b, ys)


def _work_items(counts, n_assign):
    ends = jnp.cumsum(counts)
    starts = (ends - counts).astype(jnp.int32)
    n_tiles = n_assign // FFN_TM
    tile_starts = jnp.arange(n_tiles, dtype=jnp.int32) * FFN_TM
    seg_lo = jnp.sort(jnp.concatenate([tile_starts, starts]))
    seg_hi = jnp.concatenate([seg_lo[1:], jnp.array([n_assign], jnp.int32)])
    item_tile = jnp.minimum(seg_lo // FFN_TM, n_tiles - 1).astype(jnp.int32)
    item_exp = jnp.minimum(jnp.sum((ends[None, :] <= seg_lo[:, None]).astype(jnp.int32), axis=1), N_EXPERTS - 1)
    item_exp = item_exp.astype(jnp.int32)
    item_first = (seg_lo == item_tile * FFN_TM).astype(jnp.int32)
    n_items = seg_lo.shape[0]
    item_new = jnp.concatenate([jnp.ones((1,), jnp.int32), (item_exp[1:] != item_exp[:-1]).astype(jnp.int32)])
    ordinal = jnp.cumsum(item_new) - 1
    run_exp = jnp.full((n_items + 1,), -1, jnp.int32).at[ordinal].set(item_exp)
    item_next = run_exp[ordinal + 1]
    item_slot = (ordinal % 2).astype(jnp.int32)
    item_last = (seg_hi == (item_tile + 1) * FFN_TM).astype(jnp.int32)
    items = (item_tile, item_exp, seg_lo, seg_hi, item_first, item_last, item_new, item_slot, item_next)
    return starts, items


def kernel(x, meta_tokens, norm1_g, w_in, b_f, q_gain, k_gain, fox_out_gain, sb_out_gain, w_out, norm2_g,
           w_coarse, b_coarse, w_fine, b_fine, w1, w3, w2):
    assert norm1_g.shape[0] == 1, "single-layer block"
    n_batch, seq, _ = x.shape
    n_tok = n_batch * seq
    n_assign = n_tok * TOP_K
    scale = HEAD_DIM ** -0.5
    x2d = x.reshape(n_tok, D_MODEL)

    w_in_t = w_in[0].T
    w_in_bf = _cast_bf16(w_in_t, D_PROJ)
    w_out_bf = _cast_bf16(w_out[0], 2 * D_GROUP)
    wft_bf = jnp.pad(w_in_t[D_PROJ:], ((0, LANES - N_HEADS), (0, 0))).astype(BF16)
    bf_pad = jnp.pad(b_f[0], (0, LANES - N_HEADS)).reshape(1, LANES)
    ones = jnp.ones((HEAD_DIM,), F32)
    zeros = jnp.zeros((HEAD_DIM,), F32)
    gains = jnp.stack([jnp.stack([q_gain[0] * scale, ones]), jnp.stack([k_gain[0], ones]),
                       jnp.stack([ones, zeros]), jnp.stack([ones, zeros]), jnp.stack([ones * scale, zeros]),
                       jnp.stack([ones, zeros]), jnp.stack([ones, zeros])])
    meta_pad = jnp.pad(meta_tokens.astype(F32), ((0, BLK - N_META), (0, 0)))
    n1 = norm1_g[0].reshape(1, D_MODEL)

    proj, lf = _inproj(x2d, n1, w_in_bf, wft_bf, bf_pad, gains, tm=1024)
    proj_m, lf_m = _inproj(meta_pad, n1, w_in_bf, wft_bf, bf_pad, gains, tm=BLK)
    qx, kx, kxm = _cumgate(lf_m, lf, n_batch, seq)
    qk_bound = 1.02 * HEAD_DIM * scale * jnp.max(jnp.abs(q_gain[0])) * jnp.max(jnp.abs(k_gain[0]))
    bound = (2.0 * qk_bound).reshape(1).astype(F32)
    oa = _fox_attention(bound, proj, proj_m, qx, kx, kxm, fox_out_gain[0].reshape(N_HEADS, 1, HEAD_DIM),
                        n_batch, seq)
    ob = _sb_attention(proj, proj_m, sb_out_gain[0].reshape(N_HEADS, 1, HEAD_DIM), n_batch, seq)

    wr = jnp.pad(jnp.concatenate([w_fine[0], w_coarse[0]], axis=1),
                 ((0, 0), (0, LANES - N_GROUPS - N_EXPERTS)))
    wr_hi = wr.astype(BF16)
    wr_lo = (wr - wr_hi.astype(F32)).astype(BF16)
    br = jnp.pad(jnp.concatenate([b_fine[0], b_coarse[0]]), (0, LANES - N_GROUPS - N_EXPERTS)).reshape(1, LANES)
    h1, u2, route, w0b, w1b, cnt = _outproj_router(oa, ob, w_out_bf, x2d, norm2_g[0].reshape(1, D_MODEL),
                                                   wr_hi, wr_lo, br)
    starts, items = _work_items(cnt[0, :N_EXPERTS], n_assign)
    eid = route[:, 0:TOP_K]
    start_of = jnp.sum(jnp.where(eid[..., None] == jnp.arange(N_EXPERTS, dtype=jnp.int32), starts, 0), axis=-1)
    pos = (start_of + route[:, TOP_K:2 * TOP_K]).reshape(-1)
    xs = _scatter_rows(pos, u2)
    ys = _grouped_ffn(items, xs, w1[0], w3[0], w2[0])
    out = _combine(pos, h1, w0b, w1b, ys)
    return out.reshape(n_batch, seq, D_MODEL)
```

```python
import functools

import jax
import jax.numpy as jnp
from jax import lax
from jax.experimental import pallas as pl
from jax.experimental.pallas import tpu as pltpu

F32 = jnp.float32
BF16 = jnp.bfloat16

D_MODEL = 2048
N_META = 16
HEAD_DIM = 128
N_HEADS = 8
D_GROUP = N_HEADS * HEAD_DIM
N_PROJ_GROUPS = 7
D_PROJ = N_PROJ_GROUPS * D_GROUP
N_GROUPS = 8
EXPERTS_PER_GROUP = 8
N_EXPERTS = 64
TOP_K = 2
D_EXPERT = 512
EPS = 1e-6
LANES = 128
D_PACKED = D_MODEL // 2
MXU_HEADS = 2
BLK = 128
FOX_FIRST_BLOCKS = 3
SKIP_LOG = 88.0
VMEM_LIMIT = 56 * 1024 * 1024


def _cparams(sem, vmem=VMEM_LIMIT):
    return pltpu.CompilerParams(dimension_semantics=sem, vmem_limit_bytes=vmem)


def _log_sigmoid(x):
    return jnp.minimum(x, 0.0) - jnp.log(1.0 + jnp.exp(-jnp.abs(x)))


def _split3(x):
    hi = x.astype(BF16)
    r1 = x - hi.astype(F32)
    mid = r1.astype(BF16)
    lo = (r1 - mid.astype(F32)).astype(BF16)
    return hi, mid, lo


def _dot_nt(a, b):
    return lax.dot_general(a, b, (((1,), (1,)), ((), ())), preferred_element_type=F32)


def _pack_bf16_pairs(x):
    half = x.shape[1] // 2
    xr = x.astype(BF16).astype(F32)
    hi = lax.bitcast_convert_type(xr[:, :half], jnp.uint32)
    lo = lax.bitcast_convert_type(xr[:, half:], jnp.uint32)
    return hi | lax.shift_right_logical(lo, jnp.uint32(16))


def _unpack_bf16_pairs(p):
    hi = lax.bitcast_convert_type(p & jnp.uint32(0xFFFF0000), F32)
    lo = lax.bitcast_convert_type(lax.shift_left(p, jnp.uint32(16)), F32)
    return jnp.concatenate([hi, lo], axis=1)


def _cast_kernel(x_ref, o_ref):
    o_ref[...] = x_ref[...].astype(o_ref.dtype)


def _cast_bf16(w, n_rows, tr=512):
    n_cols = w.shape[1]
    return pl.pallas_call(
        _cast_kernel,
        grid=(n_rows // tr,),
        in_specs=[pl.BlockSpec((tr, n_cols), lambda i: (i, 0))],
        out_specs=pl.BlockSpec((tr, n_cols), lambda i: (i, 0)),
        out_shape=jax.ShapeDtypeStruct((n_rows, n_cols), BF16),
        compiler_params=_cparams(("parallel",)),
        name="cast_bf16",
    )(w)


def _inproj_kernel(x_ref, g_ref, wt_ref, wft_ref, bf_ref, gain_ref, o_ref, lf_ref, u_ref, *, heads_per_tile):
    j = pl.program_id(1)

    @pl.when(j == 0)
    def _():
        x = x_ref[...]
        ms = jnp.mean(x * x, axis=-1, keepdims=True)
        u = (x * lax.rsqrt(ms + EPS) * g_ref[...]).astype(BF16)
        u_ref[...] = u
        f = _dot_nt(u, wft_ref[...]) + bf_ref[...]
        lf_ref[...] = _log_sigmoid(f)

    gain = gain_ref[0, 0:1, :]
    normed = gain_ref[0, 1:2, :] > 0.5
    u = u_ref[...]
    for s in range(heads_per_tile // MXU_HEADS):
        cols = pl.ds(s * MXU_HEADS * HEAD_DIM, MXU_HEADS * HEAD_DIM)
        acc = _dot_nt(u, wt_ref[cols, :])
        for hh in range(MXU_HEADS):
            y = acc[:, hh * HEAD_DIM:(hh + 1) * HEAD_DIM]
            ms = jnp.mean(y * y, axis=-1, keepdims=True)
            scale = jnp.where(normed, lax.rsqrt(ms + EPS), 1.0)
            o_ref[s * MXU_HEADS + hh] = (y * scale * gain).astype(BF16)


def _inproj(x2d, norm_g, wt_bf, wft_bf, bf_pad, gains, tm, tn=D_GROUP):
    rows = x2d.shape[0]
    hpt = tn // HEAD_DIM
    kern = functools.partial(_inproj_kernel, heads_per_tile=hpt)
    return pl.pallas_call(
        kern,
        grid=(rows // tm, D_PROJ // tn),
        in_specs=[
            pl.BlockSpec((tm, D_MODEL), lambda i, j: (i, 0)),
            pl.BlockSpec((1, D_MODEL), lambda i, j: (0, 0)),
            pl.BlockSpec((tn, D_MODEL), lambda i, j: (j, 0)),
            pl.BlockSpec((LANES, D_MODEL), lambda i, j: (0, 0)),
            pl.BlockSpec((1, LANES), lambda i, j: (0, 0)),
            pl.BlockSpec((1, 2, HEAD_DIM), lambda i, j: (j // (N_HEADS // hpt), 0, 0)),
        ],
        out_specs=[
            pl.BlockSpec((hpt, tm, HEAD_DIM), lambda i, j: (j, i, 0)),
            pl.BlockSpec((tm, LANES), lambda i, j: (i, 0)),
        ],
        out_shape=[
            jax.ShapeDtypeStruct((D_PROJ // HEAD_DIM, rows, HEAD_DIM), BF16),
            jax.ShapeDtypeStruct((rows, LANES), F32),
        ],
        scratch_shapes=[pltpu.VMEM((tm, D_MODEL), BF16)],
        compiler_params=_cparams(("parallel", "arbitrary")),
        name="inproj",
    )(x2d, norm_g, wt_bf, wft_bf, bf_pad, gains)


def _cumgate_kernel(lfm_ref, lf_ref, qx_ref, kx_ref, kxm_ref, *, n_blk):
    row = lax.broadcasted_iota(jnp.int32, (BLK, BLK), 0)
    col = lax.broadcasted_iota(jnp.int32, (BLK, BLK), 1)
    tri = (col <= row).astype(BF16)
    one = jnp.ones((BLK, LANES), F32)
    zero = jnp.zeros((BLK, LANES), F32)

    def prefix(x):
        hi, mid, lo = _split3(x)
        return (jnp.dot(tri, hi, preferred_element_type=F32)
                + jnp.dot(tri, mid, preferred_element_type=F32)
                + jnp.dot(tri, lo, preferred_element_type=F32))

    def ext(cum, h):
        c = jnp.broadcast_to(cum[:, h:h + 1], (BLK, LANES))
        hi, mid, lo = (t.astype(F32) for t in _split3(c))
        qx = jnp.where(col == 0, hi, jnp.where(col == 1, mid, jnp.where(col == 2, lo,
                       jnp.where(col < 6, one, zero))))
        kx = jnp.where(col < 3, one, jnp.where(col == 3, -hi, jnp.where(col == 4, -mid,
                       jnp.where(col == 5, -lo, zero))))
        return qx.astype(BF16), kx.astype(BF16)

    lfm = jnp.where(row < N_META, lfm_ref[...], 0.0)
    cum_m = prefix(lfm)
    for h in range(N_HEADS):
        _, kx = ext(cum_m, h)
        kxm_ref[h] = kx
    carry = cum_m[BLK - 1:BLK, :]
    for b in range(n_blk):
        cum = prefix(lf_ref[b * BLK:(b + 1) * BLK, :]) + carry
        carry = cum[BLK - 1:BLK, :]
        for h in range(N_HEADS):
            qx, kx = ext(cum, h)
            qx_ref[h, b * BLK:(b + 1) * BLK, :] = qx
            kx_ref[h, b * BLK:(b + 1) * BLK, :] = kx


def _cumgate(lf_meta, lf_real, n_batch, seq):
    kern = functools.partial(_cumgate_kernel, n_blk=seq // BLK)
    return pl.pallas_call(
        kern,
        grid=(n_batch,),
        in_specs=[
            pl.BlockSpec((BLK, LANES), lambda b: (0, 0)),
            pl.BlockSpec((seq, LANES), lambda b: (b, 0)),
        ],
        out_specs=[
            pl.BlockSpec((N_HEADS, seq, LANES), lambda b: (0, b, 0)),
            pl.BlockSpec((N_HEADS, seq, LANES), lambda b: (0, b, 0)),
            pl.BlockSpec((N_HEADS, BLK, LANES), lambda b: (0, 0, 0)),
        ],
        out_shape=[
            jax.ShapeDtypeStruct((N_HEADS, n_batch * seq, LANES), BF16),
            jax.ShapeDtypeStruct((N_HEADS, n_batch * seq, LANES), BF16),
            jax.ShapeDtypeStruct((N_HEADS, BLK, LANES), BF16),
        ],
        compiler_params=_cparams(("arbitrary",)),
        name="cumgate",
    )(lf_meta, lf_real)


def _bdot_nt(a, b):
    return lax.dot_general(a, b, (((2,), (2,)), ((0,), (0,))), preferred_element_type=F32)


def _bdot_nn(a, b):
    return lax.dot_general(a, b, (((2,), (1,)), ((0,), (0,))), preferred_element_type=F32)


def _fox_kernel(bound_ref, q_ref, qx_ref, k_ref, kx_ref, v_ref, g_ref, km_ref, kxm_ref, vm_ref, gain_ref,
                o_ref, m_scr, l_scr, acc_scr):
    i = pl.program_id(1)

    def sweep(kb, kxb, vb, mask, first):
        qa = jnp.concatenate([q_ref[...], qx_ref[...]], axis=2)
        ka = jnp.concatenate([kb, kxb], axis=2)
        s = _bdot_nt(qa, ka)
        if mask is not None:
            s = jnp.where(mask[None], s, -jnp.inf)
        m_cur = jnp.max(s, axis=2, keepdims=True)
        v1 = jnp.concatenate([vb, jnp.ones_like(vb)], axis=2)
        if first:
            m_col = m_cur
            pv = _bdot_nn(jnp.exp(s - m_col).astype(BF16), v1)
            l_scr[...] = pv[:, :, BLK:]
            acc_scr[...] = pv[:, :, :BLK]
            m_scr[...] = jnp.broadcast_to(m_col, m_scr.shape)
        else:
            m_prev = m_scr[...]
            m_new = jnp.maximum(m_prev, m_cur)
            m_col = m_new[:, :, 0:1]
            alpha = jnp.exp(m_prev - m_new)
            pv = _bdot_nn(jnp.exp(s - m_new).astype(BF16), v1)
            l_scr[...] = alpha * l_scr[...] + pv[:, :, BLK:]
            acc_scr[...] = alpha * acc_scr[...] + pv[:, :, :BLK]
            m_scr[...] = m_new
        return jnp.max(s[:, :, 0:1] - m_col)

    def real_block(start, width):
        start = pl.multiple_of(start, BLK)
        return (k_ref[:, pl.ds(start, width), :], kx_ref[:, pl.ds(start, width), :],
                v_ref[:, pl.ds(start, width), :])

    bound = bound_ref[0]
    first_start = jnp.maximum(i - (FOX_FIRST_BLOCKS - 1), 0) * BLK
    row2 = lax.broadcasted_iota(jnp.int32, (BLK, FOX_FIRST_BLOCKS * BLK), 0)
    col2 = lax.broadcasted_iota(jnp.int32, (BLK, FOX_FIRST_BLOCKS * BLK), 1)
    causal = col2 + (first_start - i * BLK) <= row2
    gap0 = sweep(*real_block(first_start, FOX_FIRST_BLOCKS * BLK), causal, True)

    def cond(c):
        j, done = c
        return jnp.logical_and(j >= 0, done == 0)

    def body(c):
        j, _ = c
        gap = sweep(*real_block(j * BLK, BLK), None, False)
        return j - 1, (gap + bound < -SKIP_LOG).astype(jnp.int32)

    _, done = lax.while_loop(cond, body, (i - FOX_FIRST_BLOCKS, (gap0 + bound < -SKIP_LOG).astype(jnp.int32)))

    @pl.when(done == 0)
    def _():
        col = lax.broadcasted_iota(jnp.int32, (BLK, BLK), 1)
        sweep(km_ref[...], kxm_ref[...], vm_ref[...], col < N_META, False)

    o = acc_scr[...] / l_scr[...]
    ms = jnp.mean(o * o, axis=-1, keepdims=True)
    gate = 1.0 / (1.0 + jnp.exp(-g_ref[...].astype(F32)))
    o_ref[...] = (o * lax.rsqrt(ms + EPS) * gain_ref[...] * gate).astype(BF16)


def _sb_kernel(q_ref, k_ref, v_ref, km_ref, vm_ref, gain_ref, o_ref, carry_scr, acc_scr):
    i = pl.program_id(1)

    def suffix_operator(width):
        r = lax.broadcasted_iota(jnp.int32, (width, width + BLK), 0)
        c = lax.broadcasted_iota(jnp.int32, (width, width + BLK), 1)
        return jnp.logical_or(c >= width, r > c).astype(BF16)

    def sweep(kb, vb, mask, first):
        width = kb.shape[1]
        suffix = suffix_operator(width)
        z = _bdot_nt(q_ref[...], kb)
        sp = jnp.maximum(z, 0.0) + jnp.log(1.0 + jnp.exp(-jnp.abs(z)))
        lk = -sp
        if mask is not None:
            lk = jnp.where(mask[None], lk, 0.0)
        hi = lk.astype(BF16)
        lo = (lk - hi.astype(F32)).astype(BF16)
        t = (jnp.dot(hi.reshape(N_HEADS * BLK, width), suffix, preferred_element_type=F32)
             + jnp.dot(lo.reshape(N_HEADS * BLK, width), suffix, preferred_element_type=F32))
        t = t.reshape(N_HEADS, BLK, width + BLK)
        later = t[:, :, :width]
        rowsum = t[:, :, width:]
        if not first:
            later = later + carry_scr[...]
        a = jnp.exp(z - sp + later)
        if mask is not None:
            a = jnp.where(mask[None], a, 0.0)
        pv = _bdot_nn(a.astype(BF16), vb)
        if first:
            acc_scr[...] = pv
            c_new = rowsum
        else:
            acc_scr[...] = acc_scr[...] + pv
            c_new = carry_scr[...] + rowsum
        carry_scr[...] = c_new
        return jnp.max(c_new[:, :, 0:1])

    def real_block(start, width):
        start = pl.multiple_of(start, BLK)
        return k_ref[:, pl.ds(start, width), :], v_ref[:, pl.ds(start, width), :]

    first_start = jnp.maximum(i - 1, 0) * BLK
    row2 = lax.broadcasted_iota(jnp.int32, (BLK, 2 * BLK), 0)
    col2 = lax.broadcasted_iota(jnp.int32, (BLK, 2 * BLK), 1)
    strict = col2 + (first_start - i * BLK) < row2
    top0 = sweep(*real_block(first_start, 2 * BLK), strict, True)

    def cond(c):
        j, done = c
        return jnp.logical_and(j >= 0, done == 0)

    def body(c):
        j, _ = c
        top = sweep(*real_block(j * BLK, BLK), None, False)
        return j - 1, (top < -SKIP_LOG).astype(jnp.int32)

    _, done = lax.while_loop(cond, body, (i - 2, (top0 < -SKIP_LOG).astype(jnp.int32)))

    @pl.when(done == 0)
    def _():
        col = lax.broadcasted_iota(jnp.int32, (BLK, BLK), 1)
        sweep(km_ref[...], vm_ref[...], col < N_META, False)

    o = acc_scr[...]
    ms = jnp.mean(o * o, axis=-1, keepdims=True)
    o_ref[...] = (o * lax.rsqrt(ms + EPS) * gain_ref[...]).astype(BF16)


def _head_spec_q(group, nq):
    return pl.BlockSpec((N_HEADS, BLK, HEAD_DIM), lambda b, i: (group, b * nq + i, 0))


def _head_spec_kv(group, seq):
    return pl.BlockSpec((N_HEADS, seq, HEAD_DIM), lambda b, i: (group, b, 0))


def _head_spec_meta(group):
    return pl.BlockSpec((N_HEADS, BLK, HEAD_DIM), lambda b, i: (group, 0, 0))


def _fox_attention(bound, proj, proj_m, qx, kx, kxm, out_gain, n_batch, seq):
    nq = seq // BLK
    rows = n_batch * seq
    return pl.pallas_call(
        _fox_kernel,
        grid=(n_batch, nq),
        in_specs=[
            pl.BlockSpec(memory_space=pltpu.SMEM),
            _head_spec_q(0, nq),
            _head_spec_q(0, nq),
            _head_spec_kv(1, seq),
            _head_spec_kv(0, seq),
            _head_spec_kv(2, seq),
            _head_spec_q(3, nq),
            _head_spec_meta(1),
            _head_spec_meta(0),
            _head_spec_meta(2),
            pl.BlockSpec((N_HEADS, 1, HEAD_DIM), lambda b, i: (0, 0, 0)),
        ],
        out_specs=pl.BlockSpec((N_HEADS, BLK, HEAD_DIM), lambda b, i: (0, b * nq + i, 0)),
        out_shape=jax.ShapeDtypeStruct((N_HEADS, rows, HEAD_DIM), BF16),
        scratch_shapes=[pltpu.VMEM((N_HEADS, BLK, LANES), F32)] * 3,
        compiler_params=_cparams(("parallel", "arbitrary")),
        name="fox_attention",
    )(bound, proj, qx, proj, kx, proj, proj, proj_m, kxm, proj_m, out_gain)


def _sb_attention(proj, proj_m, out_gain, n_batch, seq):
    nq = seq // BLK
    rows = n_batch * seq
    return pl.pallas_call(
        _sb_kernel,
        grid=(n_batch, nq),
        in_specs=[
            _head_spec_q(4, nq),
            _head_spec_kv(5, seq),
            _head_spec_kv(6, seq),
            _head_spec_meta(5),
            _head_spec_meta(6),
            pl.BlockSpec((N_HEADS, 1, HEAD_DIM), lambda b, i: (0, 0, 0)),
        ],
        out_specs=pl.BlockSpec((N_HEADS, BLK, HEAD_DIM), lambda b, i: (0, b * nq + i, 0)),
        out_shape=jax.ShapeDtypeStruct((N_HEADS, rows, HEAD_DIM), BF16),
        scratch_shapes=[pltpu.VMEM((N_HEADS, BLK, LANES), F32)] * 2,
        compiler_params=_cparams(("parallel", "arbitrary")),
        name="sb_attention",
    )(proj, proj, proj, proj_m, proj_m, out_gain)


ROUTE_TM = 256


def _router_kernel(oa_ref, ob_ref, wo_ref, x_ref, g_ref, whi_ref, wlo_ref, b_ref,
                   h_ref, u_ref, route_ref, w0_ref, w1_ref, cnt_ref, carry_scr):
    step = pl.program_id(0)
    lhs = jnp.concatenate([oa_ref[h] for h in range(N_HEADS)] + [ob_ref[h] for h in range(N_HEADS)], axis=1)
    x = x_ref[...] + jnp.dot(lhs, wo_ref[...], preferred_element_type=F32)
    h_ref[...] = x
    tm = x.shape[0]
    ms = jnp.mean(x * x, axis=-1, keepdims=True)
    u = x * lax.rsqrt(ms + EPS) * g_ref[...]
    u_ref[...] = _pack_bf16_pairs(u)
    uhi = u.astype(BF16)
    ulo = (u - uhi.astype(F32)).astype(BF16)
    logits = (jnp.dot(uhi, whi_ref[...], preferred_element_type=F32)
              + jnp.dot(uhi, wlo_ref[...], preferred_element_type=F32)
              + jnp.dot(ulo, whi_ref[...], preferred_element_type=F32)) + b_ref[...]
    lane_i = lax.broadcasted_iota(jnp.int32, (tm, LANES), 1)
    lane = lane_i.astype(F32)
    big = float(4 * LANES)
    neg = -jnp.inf
    c = jnp.where(jnp.logical_and(lane_i >= N_EXPERTS, lane_i < N_EXPERTS + N_GROUPS), logits, neg)
    cmax = jnp.max(c, axis=1, keepdims=True)
    g_sel = jnp.min(jnp.where(c == cmax, lane, big), axis=1, keepdims=True) - N_EXPERTS
    g_gate = 1.0 / jnp.sum(jnp.exp(c - cmax), axis=1, keepdims=True)
    lo = g_sel * EXPERTS_PER_GROUP
    in_group = jnp.logical_and(lane >= lo, lane < lo + EXPERTS_PER_GROUP)
    f = jnp.where(in_group, logits, neg)
    t1 = jnp.max(f, axis=1, keepdims=True)
    i1 = jnp.min(jnp.where(f == t1, lane, big), axis=1, keepdims=True)
    f2 = jnp.where(lane == i1, neg, f)
    t2 = jnp.max(f2, axis=1, keepdims=True)
    i2 = jnp.min(jnp.where(f2 == t2, lane, big), axis=1, keepdims=True)
    d = jnp.exp(t2 - t1)
    w_first = g_gate / (1.0 + d)
    w0_ref[...] = jnp.broadcast_to(w_first, (tm, LANES))
    w1_ref[...] = jnp.broadcast_to(w_first * d, (tm, LANES))

    @pl.when(step == 0)
    def _():
        carry_scr[...] = jnp.zeros_like(carry_scr)

    oh0 = (lane == i1).astype(F32)
    oh1 = (lane == i2).astype(F32)
    oh = oh0 + oh1
    r = lax.broadcasted_iota(jnp.int32, (tm, tm), 0)
    cc = lax.broadcasted_iota(jnp.int32, (tm, tm), 1)
    before = (cc < r).astype(BF16)
    seen = jnp.dot(before, oh.astype(BF16), preferred_element_type=F32) + carry_scr[0:1, :]
    rank0 = jnp.sum(oh0 * seen, axis=1, keepdims=True)
    rank1 = jnp.sum(oh1 * seen, axis=1, keepdims=True)
    total = seen[tm - 1:tm, :] + oh[tm - 1:tm, :]
    carry_scr[...] = jnp.broadcast_to(total, carry_scr.shape)
    cnt_ref[...] = jnp.broadcast_to(total, cnt_ref.shape).astype(jnp.int32)
    vals = jnp.where(lane_i == 0, i1, jnp.where(lane_i == 1, i2, jnp.where(lane_i == 2, rank0,
                     jnp.where(lane_i == 3, rank1, 0.0))))
    route_ref[...] = vals.astype(jnp.int32)


def _outproj_router(oa, ob, wo_bf, x2d, norm_g, whi, wlo, b_pad):
    rows = x2d.shape[0]
    tm = ROUTE_TM
    return pl.pallas_call(
        _router_kernel,
        grid=(rows // tm,),
        in_specs=[
            pl.BlockSpec((N_HEADS, tm, HEAD_DIM), lambda i: (0, i, 0)),
            pl.BlockSpec((N_HEADS, tm, HEAD_DIM), lambda i: (0, i, 0)),
            pl.BlockSpec((2 * D_GROUP, D_MODEL), lambda i: (0, 0)),
            pl.BlockSpec((tm, D_MODEL), lambda i: (i, 0)),
            pl.BlockSpec((1, D_MODEL), lambda i: (0, 0)),
            pl.BlockSpec((D_MODEL, LANES), lambda i: (0, 0)),
            pl.BlockSpec((D_MODEL, LANES), lambda i: (0, 0)),
            pl.BlockSpec((1, LANES), lambda i: (0, 0)),
        ],
        out_specs=[
            pl.BlockSpec((tm, D_MODEL), lambda i: (i, 0)),
            pl.BlockSpec((tm, D_PACKED), lambda i: (i, 0)),
            pl.BlockSpec((tm, LANES), lambda i: (i, 0)),
            pl.BlockSpec((tm, LANES), lambda i: (i, 0)),
            pl.BlockSpec((tm, LANES), lambda i: (i, 0)),
            pl.BlockSpec((8, LANES), lambda i: (0, 0)),
        ],
        out_shape=[
            jax.ShapeDtypeStruct((rows, D_MODEL), F32),
            jax.ShapeDtypeStruct((rows, D_PACKED), jnp.uint32),
            jax.ShapeDtypeStruct((rows, LANES), jnp.int32),
            jax.ShapeDtypeStruct((rows, LANES), F32),
            jax.ShapeDtypeStruct((rows, LANES), F32),
            jax.ShapeDtypeStruct((8, LANES), jnp.int32),
        ],
        scratch_shapes=[pltpu.VMEM((8, LANES), F32)],
        compiler_params=_cparams(("arbitrary",)),
        name="outproj_router",
    )(oa, ob, wo_bf, x2d, norm_g, whi, wlo, b_pad)


MOVE_TM = 256


def _scatter_kernel(pos_ref, u_ref, xs_ref, sem):
    base = pl.program_id(0) * MOVE_TM

    for t in range(MOVE_TM):
        for k in range(TOP_K):
            dst = pos_ref[(base + t) * TOP_K + k]
            pltpu.make_async_copy(u_ref.at[pl.ds(t, 1)], xs_ref.at[pl.ds(dst, 1)], sem).start(priority=k)
    for _ in range(TOP_K):
        pltpu.make_async_copy(u_ref, xs_ref.at[pl.ds(0, MOVE_TM)], sem).wait()


def _scatter_rows(pos_flat, u2):
    n_tok = u2.shape[0]
    return pl.pallas_call(
        _scatter_kernel,
        grid_spec=pltpu.PrefetchScalarGridSpec(
            num_scalar_prefetch=1,
            grid=(n_tok // MOVE_TM,),
            in_specs=[pl.BlockSpec((MOVE_TM, D_PACKED), lambda i, pos: (i, 0))],
            out_specs=pl.BlockSpec(memory_space=pl.ANY),
            scratch_shapes=[pltpu.SemaphoreType.DMA(())],
        ),
        out_shape=jax.ShapeDtypeStruct((n_tok * TOP_K, D_PACKED), u2.dtype),
        compiler_params=_cparams(("arbitrary",)),
        name="scatter_rows",
    )(pos_flat, u2)


FFN_TM = 128
WEIGHT_SLOTS = 3


def _ffn_kernel(tile_ref, exp_ref, lo_ref, hi_ref, first_ref, last_ref, new_ref, slot_ref, next_ref, next2_ref,
                x_ref, w1_hbm, w3_hbm, w2_hbm, o_ref,
                xb_scr, acc_scr, w1_buf, w3_buf, w2_buf, w1_bf, w3_bf, w2_bf, sem):
    it = pl.program_id(0)
    lo = lo_ref[it]
    hi = hi_ref[it]
    slot = slot_ref[it]

    def weight_copies(expert, s):
        half = D_EXPERT // 2
        return ((pltpu.make_async_copy(w1_hbm.at[expert], w1_buf.at[s], sem.at[s, 0]), 0),
                (pltpu.make_async_copy(w3_hbm.at[expert], w3_buf.at[s], sem.at[s, 1]), 1),
                (pltpu.make_async_copy(w2_hbm.at[expert, pl.ds(0, half)], w2_buf.at[s, pl.ds(0, half)],
                                       sem.at[s, 2]), 0),
                (pltpu.make_async_copy(w2_hbm.at[expert, pl.ds(half, half)], w2_buf.at[s, pl.ds(half, half)],
                                       sem.at[s, 3]), 1))

    def next_slot(s, ahead):
        s = s + ahead
        return jnp.where(s >= WEIGHT_SLOTS, s - WEIGHT_SLOTS, s)

    @pl.when(it == 0)
    def _():
        for cp, queue in weight_copies(exp_ref[0], slot):
            cp.start(priority=queue)

        @pl.when(next_ref[0] >= 0)
        def _():
            for cp, queue in weight_copies(next_ref[0], next_slot(slot, 1)):
                cp.start(priority=queue)

    @pl.when(new_ref[it] == 1)
    def _():
        @pl.when(next2_ref[it] >= 0)
        def _():
            for cp, queue in weight_copies(next2_ref[it], next_slot(slot, 2)):
                cp.start(priority=queue)

        for cp, _ in weight_copies(exp_ref[it], slot):
            cp.wait()

        w1_bf[...] = w1_buf[slot].astype(BF16)
        w3_bf[...] = w3_buf[slot].astype(BF16)
        w2_bf[...] = w2_buf[slot].astype(BF16)

    @pl.when(first_ref[it] == 1)
    def _():
        xb_scr[...] = _unpack_bf16_pairs(x_ref[...]).astype(BF16)
        acc_scr[...] = jnp.zeros_like(acc_scr)

    @pl.when(hi > lo)
    def _():
        x = xb_scr[...]
        a = jnp.dot(x, w1_bf[...], preferred_element_type=F32)
        b = jnp.dot(x, w3_bf[...], preferred_element_type=F32)
        mid = a / (1.0 + jnp.exp(-a)) * b
        rows = tile_ref[it] * FFN_TM + lax.broadcasted_iota(jnp.int32, mid.shape, 0)
        mid = jnp.where(jnp.logical_and(rows >= lo, rows < hi), mid, 0.0).astype(BF16)
        acc_scr[...] += jnp.dot(mid, w2_bf[...], preferred_element_type=F32)

    @pl.when(last_ref[it] == 1)
    def _():
        o_ref[...] = _pack_bf16_pairs(acc_scr[...])


def _grouped_ffn(items, xs, w1, w3, w2):
    n_items = items[0].shape[0]
    n_rows = xs.shape[0]

    def tile_map(i, t, *_):
        return (t[i], 0)

    return pl.pallas_call(
        _ffn_kernel,
        grid_spec=pltpu.PrefetchScalarGridSpec(
            num_scalar_prefetch=len(items),
            grid=(n_items,),
            in_specs=[
                pl.BlockSpec((FFN_TM, D_PACKED), tile_map),
                pl.BlockSpec(memory_space=pl.ANY),
                pl.BlockSpec(memory_space=pl.ANY),
                pl.BlockSpec(memory_space=pl.ANY),
            ],
            out_specs=pl.BlockSpec((FFN_TM, D_PACKED), tile_map),
            scratch_shapes=[
                pltpu.VMEM((FFN_TM, D_MODEL), BF16),
                pltpu.VMEM((FFN_TM, D_MODEL), F32),
                pltpu.VMEM((WEIGHT_SLOTS, D_MODEL, D_EXPERT), F32),
                pltpu.VMEM((WEIGHT_SLOTS, D_MODEL, D_EXPERT), F32),
                pltpu.VMEM((WEIGHT_SLOTS, D_EXPERT, D_MODEL), F32),
                pltpu.VMEM((D_MODEL, D_EXPERT), BF16),
                pltpu.VMEM((D_MODEL, D_EXPERT), BF16),
                pltpu.VMEM((D_EXPERT, D_MODEL), BF16),
                pltpu.SemaphoreType.DMA((WEIGHT_SLOTS, 4)),
            ],
        ),
        out_shape=jax.ShapeDtypeStruct((n_rows, D_PACKED), jnp.uint32),
        compiler_params=_cparams(("arbitrary",)),
        name="grouped_ffn",
    )(*items, xs, w1, w3, w2)


def _combine_kernel(pos_ref, h_ref, w0_ref, w1_ref, ys_ref, o_ref, ybuf, sem):
    step = pl.program_id(0)
    n_steps = pl.num_programs(0)

    def fetch(s, slot):
        for t in range(MOVE_TM):
            for k in range(TOP_K):
                src = pos_ref[(s * MOVE_TM + t) * TOP_K + k]
                pltpu.make_async_copy(ys_ref.at[pl.ds(src, 1)], ybuf.at[slot, k, pl.ds(t, 1)],
                                      sem.at[slot]).start(priority=k)

    @pl.when(step == 0)
    def _():
        fetch(0, 0)

    slot = step % 2

    @pl.when(step + 1 < n_steps)
    def _():
        fetch(step + 1, 1 - slot)

    for k in range(TOP_K):
        pltpu.make_async_copy(ys_ref.at[pl.ds(0, MOVE_TM)], ybuf.at[slot, k], sem.at[slot]).wait()
    reps = D_MODEL // LANES
    w0 = jnp.concatenate([w0_ref[...]] * reps, axis=1)
    w1 = jnp.concatenate([w1_ref[...]] * reps, axis=1)
    o_ref[...] = (h_ref[...] + w0 * _unpack_bf16_pairs(ybuf[slot, 0])
                  + w1 * _unpack_bf16_pairs(ybuf[slot, 1]))


def _combine(pos_flat, h1, w0b, w1b, ys):
    rows = h1.shape[0]
    tm = MOVE_TM
    return pl.pallas_call(
        _combine_kernel,
        grid_spec=pltpu.PrefetchScalarGridSpec(
            num_scalar_prefetch=1,
            grid=(rows // tm,),
            in_specs=[
                pl.BlockSpec((tm, D_MODEL), lambda i, pos: (i, 0)),
                pl.BlockSpec((tm, LANES), lambda i, pos: (i, 0)),
                pl.BlockSpec((tm, LANES), lambda i, pos: (i, 0)),
                pl.BlockSpec(memory_space=pl.ANY),
            ],
            out_specs=pl.BlockSpec((tm, D_MODEL), lambda i, pos: (i, 0)),
            scratch_shapes=[pltpu.VMEM((2, TOP_K, tm, D_PACKED), jnp.uint32), pltpu.SemaphoreType.DMA((2,))],
        ),
        out_shape=jax.ShapeDtypeStruct((rows, D_MODEL), F32),
        compiler_params=_cparams(("arbitrary",)),
        name="combine",
    )(pos_flat, h1, w0b, w1b, ys)


def _work_items(counts, n_assign):
    ends = jnp.cumsum(counts)
    starts = (ends - counts).astype(jnp.int32)
    n_tiles = n_assign // FFN_TM
    tile_starts = jnp.arange(n_tiles, dtype=jnp.int32) * FFN_TM
    seg_lo = jnp.sort(jnp.concatenate([tile_starts, starts]))
    seg_hi = jnp.concatenate([seg_lo[1:], jnp.array([n_assign], jnp.int32)])
    item_tile = jnp.minimum(seg_lo // FFN_TM, n_tiles - 1).astype(jnp.int32)
    item_exp = jnp.minimum(jnp.sum((ends[None, :] <= seg_lo[:, None]).astype(jnp.int32), axis=1), N_EXPERTS - 1)
    item_exp = item_exp.astype(jnp.int32)
    item_first = (seg_lo == item_tile * FFN_TM).astype(jnp.int32)
    n_items = seg_lo.shape[0]
    item_new = jnp.concatenate([jnp.ones((1,), jnp.int32), (item_exp[1:] != item_exp[:-1]).astype(jnp.int32)])
    ordinal = jnp.cumsum(item_new) - 1
    run_exp = jnp.full((n_items + 2,), -1, jnp.int32).at[ordinal].set(item_exp)
    item_next = run_exp[ordinal + 1]
    item_next2 = run_exp[ordinal + 2]
    item_slot = (ordinal % WEIGHT_SLOTS).astype(jnp.int32)
    item_last = (seg_hi == (item_tile + 1) * FFN_TM).astype(jnp.int32)
    items = (item_tile, item_exp, seg_lo, seg_hi, item_first, item_last, item_new, item_slot, item_next,
             item_next2)
    return starts, items


def kernel(x, meta_tokens, norm1_g, w_in, b_f, q_gain, k_gain, fox_out_gain, sb_out_gain, w_out, norm2_g,
           w_coarse, b_coarse, w_fine, b_fine, w1, w3, w2):
    assert norm1_g.shape[0] == 1, "single-layer block"
    n_batch, seq, _ = x.shape
    n_tok = n_batch * seq
    n_assign = n_tok * TOP_K
    scale = HEAD_DIM ** -0.5
    x2d = x.reshape(n_tok, D_MODEL)

    w_in_t = w_in[0].T
    w_in_bf = _cast_bf16(w_in_t, D_PROJ)
    w_out_bf = _cast_bf16(w_out[0], 2 * D_GROUP)
    wft_bf = jnp.pad(w_in_t[D_PROJ:], ((0, LANES - N_HEADS), (0, 0))).astype(BF16)
    bf_pad = jnp.pad(b_f[0], (0, LANES - N_HEADS)).reshape(1, LANES)
    ones = jnp.ones((HEAD_DIM,), F32)
    zeros = jnp.zeros((HEAD_DIM,), F32)
    gains = jnp.stack([jnp.stack([q_gain[0] * scale, ones]), jnp.stack([k_gain[0], ones]),
                       jnp.stack([ones, zeros]), jnp.stack([ones, zeros]), jnp.stack([ones * scale, zeros]),
                       jnp.stack([ones, zeros]), jnp.stack([ones, zeros])])
    meta_pad = jnp.pad(meta_tokens.astype(F32), ((0, BLK - N_META), (0, 0)))
    n1 = norm1_g[0].reshape(1, D_MODEL)

    proj, lf = _inproj(x2d, n1, w_in_bf, wft_bf, bf_pad, gains, tm=1024)
    proj_m, lf_m = _inproj(meta_pad, n1, w_in_bf, wft_bf, bf_pad, gains, tm=BLK)
    qx, kx, kxm = _cumgate(lf_m, lf, n_batch, seq)
    qk_bound = 1.02 * HEAD_DIM * scale * jnp.max(jnp.abs(q_gain[0])) * jnp.max(jnp.abs(k_gain[0]))
    bound = (2.0 * qk_bound).reshape(1).astype(F32)
    oa = _fox_attention(bound, proj, proj_m, qx, kx, kxm, fox_out_gain[0].reshape(N_HEADS, 1, HEAD_DIM),
                        n_batch, seq)
    ob = _sb_attention(proj, proj_m, sb_out_gain[0].reshape(N_HEADS, 1, HEAD_DIM), n_batch, seq)

    wr = jnp.pad(jnp.concatenate([w_fine[0], w_coarse[0]], axis=1),
                 ((0, 0), (0, LANES - N_GROUPS - N_EXPERTS)))
    wr_hi = wr.astype(BF16)
    wr_lo = (wr - wr_hi.astype(F32)).astype(BF16)
    br = jnp.pad(jnp.concatenate([b_fine[0], b_coarse[0]]), (0, LANES - N_GROUPS - N_EXPERTS)).reshape(1, LANES)
    h1, u2, route, w0b, w1b, cnt = _outproj_router(oa, ob, w_out_bf, x2d, norm2_g[0].reshape(1, D_MODEL),
                                                   wr_hi, wr_lo, br)
    starts, items = _work_items(cnt[0, :N_EXPERTS], n_assign)
    eid = route[:, 0:TOP_K]
    start_of = jnp.sum(jnp.where(eid[..., None] == jnp.arange(N_EXPERTS, dtype=jnp.int32), starts, 0), axis=-1)
    pos = (start_of + route[:, TOP_K:2 * TOP_K]).reshape(-1)
    xs = _scatter_rows(pos, u2)
    ys = _grouped_ffn(items, xs, w1[0], w3[0], w2[0])
    out = _combine(pos, h1, w0b, w1b, ys)
    return out.reshape(n_batch, seq, D_MODEL)
```

```python
import functools

import jax
import jax.numpy as jnp
from jax import lax
from jax.experimental import pallas as pl
from jax.experimental.pallas import tpu as pltpu

F32 = jnp.float32
BF16 = jnp.bfloat16

D_MODEL = 2048
N_META = 16
HEAD_DIM = 128
N_HEADS = 8
D_GROUP = N_HEADS * HEAD_DIM
N_PROJ_GROUPS = 7
D_PROJ = N_PROJ_GROUPS * D_GROUP
N_GROUPS = 8
EXPERTS_PER_GROUP = 8
N_EXPERTS = 64
TOP_K = 2
D_EXPERT = 512
EPS = 1e-6
LANES = 128
D_PACKED = D_MODEL // 2
MXU_HEADS = 2
BLK = 128
FOX_FIRST_BLOCKS = 3
SKIP_LOG = 88.0
VMEM_LIMIT = 56 * 1024 * 1024


def _cparams(sem, vmem=VMEM_LIMIT):
    return pltpu.CompilerParams(dimension_semantics=sem, vmem_limit_bytes=vmem)


def _log_sigmoid(x):
    return jnp.minimum(x, 0.0) - jnp.log(1.0 + jnp.exp(-jnp.abs(x)))


def _split3(x):
    hi = x.astype(BF16)
    r1 = x - hi.astype(F32)
    mid = r1.astype(BF16)
    lo = (r1 - mid.astype(F32)).astype(BF16)
    return hi, mid, lo


def _dot_nt(a, b):
    return lax.dot_general(a, b, (((1,), (1,)), ((), ())), preferred_element_type=F32)


def _pack_bf16_pairs(x):
    half = x.shape[1] // 2
    xr = x.astype(BF16).astype(F32)
    hi = lax.bitcast_convert_type(xr[:, :half], jnp.uint32)
    lo = lax.bitcast_convert_type(xr[:, half:], jnp.uint32)
    return hi | lax.shift_right_logical(lo, jnp.uint32(16))


def _unpack_bf16_pairs(p):
    hi = lax.bitcast_convert_type(p & jnp.uint32(0xFFFF0000), F32)
    lo = lax.bitcast_convert_type(lax.shift_left(p, jnp.uint32(16)), F32)
    return jnp.concatenate([hi, lo], axis=1)


def _cast_kernel(x_ref, o_ref):
    o_ref[...] = x_ref[...].astype(o_ref.dtype)


def _cast_bf16(w, n_rows, tr=512):
    n_cols = w.shape[1]
    return pl.pallas_call(
        _cast_kernel,
        grid=(n_rows // tr,),
        in_specs=[pl.BlockSpec((tr, n_cols), lambda i: (i, 0))],
        out_specs=pl.BlockSpec((tr, n_cols), lambda i: (i, 0)),
        out_shape=jax.ShapeDtypeStruct((n_rows, n_cols), BF16),
        compiler_params=_cparams(("parallel",)),
        name="cast_bf16",
    )(w)


def _inproj_kernel(x_ref, g_ref, wt_ref, wft_ref, bf_ref, gain_ref, o_ref, lf_ref, *rest, heads_per_tile):
    wbf_ref = rest[0] if len(rest) == 2 else None
    u_ref = rest[-1]
    j = pl.program_id(1)

    @pl.when(j == 0)
    def _():
        x = x_ref[...]
        ms = jnp.mean(x * x, axis=-1, keepdims=True)
        u = (x * lax.rsqrt(ms + EPS) * g_ref[...]).astype(BF16)
        u_ref[...] = u
        f = _dot_nt(u, wft_ref[...]) + bf_ref[...]
        lf_ref[...] = _log_sigmoid(f)

    gain = gain_ref[0, 0:1, :]
    normed = gain_ref[0, 1:2, :] > 0.5
    u = u_ref[...]
    for s in range(heads_per_tile // MXU_HEADS):
        cols = pl.ds(s * MXU_HEADS * HEAD_DIM, MXU_HEADS * HEAD_DIM)
        w = wt_ref[cols, :]
        if wbf_ref is not None:
            w = w.astype(BF16)
            wbf_ref[cols, :] = w
        acc = _dot_nt(u, w)
        for hh in range(MXU_HEADS):
            y = acc[:, hh * HEAD_DIM:(hh + 1) * HEAD_DIM]
            ms = jnp.mean(y * y, axis=-1, keepdims=True)
            scale = jnp.where(normed, lax.rsqrt(ms + EPS), 1.0)
            o_ref[s * MXU_HEADS + hh] = (y * scale * gain).astype(BF16)


def _inproj(x2d, norm_g, wt, wft_bf, bf_pad, gains, tm, tn=D_GROUP):
    rows = x2d.shape[0]
    hpt = tn // HEAD_DIM
    kern = functools.partial(_inproj_kernel, heads_per_tile=hpt)
    out_specs = [
        pl.BlockSpec((hpt, tm, HEAD_DIM), lambda i, j: (j, i, 0)),
        pl.BlockSpec((tm, LANES), lambda i, j: (i, 0)),
    ]
    out_shape = [
        jax.ShapeDtypeStruct((D_PROJ // HEAD_DIM, rows, HEAD_DIM), BF16),
        jax.ShapeDtypeStruct((rows, LANES), F32),
    ]
    if wt.dtype != BF16:
        assert rows == tm, "the bf16 weight copy is written once per column tile"
        out_specs.append(pl.BlockSpec((tn, D_MODEL), lambda i, j: (j, 0)))
        out_shape.append(jax.ShapeDtypeStruct((D_PROJ, D_MODEL), BF16))
    return pl.pallas_call(
        kern,
        grid=(rows // tm, D_PROJ // tn),
        in_specs=[
            pl.BlockSpec((tm, D_MODEL), lambda i, j: (i, 0)),
            pl.BlockSpec((1, D_MODEL), lambda i, j: (0, 0)),
            pl.BlockSpec((tn, D_MODEL), lambda i, j: (j, 0)),
            pl.BlockSpec((LANES, D_MODEL), lambda i, j: (0, 0)),
            pl.BlockSpec((1, LANES), lambda i, j: (0, 0)),
            pl.BlockSpec((1, 2, HEAD_DIM), lambda i, j: (j // (N_HEADS // hpt), 0, 0)),
        ],
        out_specs=out_specs,
        out_shape=out_shape,
        scratch_shapes=[pltpu.VMEM((tm, D_MODEL), BF16)],
        compiler_params=_cparams(("parallel", "arbitrary")),
        name="inproj",
    )(x2d, norm_g, wt, wft_bf, bf_pad, gains)


def _cumgate_kernel(lfm_ref, lf_ref, qx_ref, kx_ref, kxm_ref, *, n_blk):
    row = lax.broadcasted_iota(jnp.int32, (BLK, BLK), 0)
    col = lax.broadcasted_iota(jnp.int32, (BLK, BLK), 1)
    tri = (col <= row).astype(BF16)

    def prefix(x):
        hi, mid, lo = _split3(x)
        return (jnp.dot(tri, hi, preferred_element_type=F32)
                + jnp.dot(tri, mid, preferred_element_type=F32)
                + jnp.dot(tri, lo, preferred_element_type=F32))

    one = jnp.ones((BLK, LANES), F32)
    zero = jnp.zeros((BLK, LANES), F32)

    def ext(cum, h):
        c = jnp.broadcast_to(cum[:, h:h + 1], (BLK, LANES))
        hi, mid, lo = (t.astype(F32) for t in _split3(c))
        qx = jnp.where(col == 0, hi, jnp.where(col == 1, mid, jnp.where(col == 2, lo,
                       jnp.where(col < 6, one, zero))))
        kx = jnp.where(col < 3, one, jnp.where(col == 3, -hi, jnp.where(col == 4, -mid,
                       jnp.where(col == 5, -lo, zero))))
        return qx.astype(BF16), kx.astype(BF16)

    lfm = jnp.where(row < N_META, lfm_ref[...], 0.0)
    cum_m = prefix(lfm)
    for h in range(N_HEADS):
        _, kx = ext(cum_m, h)
        kxm_ref[h] = kx
    carry = cum_m[BLK - 1:BLK, :]
    for b in range(n_blk):
        cum = prefix(lf_ref[b * BLK:(b + 1) * BLK, :]) + carry
        carry = cum[BLK - 1:BLK, :]
        for h in range(N_HEADS):
            qx, kx = ext(cum, h)
            qx_ref[h, b * BLK:(b + 1) * BLK, :] = qx
            kx_ref[h, b * BLK:(b + 1) * BLK, :] = kx


def _cumgate(lf_meta, lf_real, n_batch, seq):
    kern = functools.partial(_cumgate_kernel, n_blk=seq // BLK)
    return pl.pallas_call(
        kern,
        grid=(n_batch,),
        in_specs=[
            pl.BlockSpec((BLK, LANES), lambda b: (0, 0)),
            pl.BlockSpec((seq, LANES), lambda b: (b, 0)),
        ],
        out_specs=[
            pl.BlockSpec((N_HEADS, seq, LANES), lambda b: (0, b, 0)),
            pl.BlockSpec((N_HEADS, seq, LANES), lambda b: (0, b, 0)),
            pl.BlockSpec((N_HEADS, BLK, LANES), lambda b: (0, 0, 0)),
        ],
        out_shape=[
            jax.ShapeDtypeStruct((N_HEADS, n_batch * seq, LANES), BF16),
            jax.ShapeDtypeStruct((N_HEADS, n_batch * seq, LANES), BF16),
            jax.ShapeDtypeStruct((N_HEADS, BLK, LANES), BF16),
        ],
        compiler_params=_cparams(("arbitrary",)),
        name="cumgate",
    )(lf_meta, lf_real)


def _bdot_nt(a, b):
    return lax.dot_general(a, b, (((2,), (2,)), ((0,), (0,))), preferred_element_type=F32)


def _bdot_nn(a, b):
    return lax.dot_general(a, b, (((2,), (1,)), ((0,), (0,))), preferred_element_type=F32)


def _fox_kernel(bound_ref, q_ref, qx_ref, k_ref, kx_ref, v_ref, g_ref, km_ref, kxm_ref, vm_ref, gain_ref,
                o_ref, m_scr, l_scr, acc_scr):
    i = pl.program_id(1)

    def sweep(kb, kxb, vb, mask, first):
        qa = jnp.concatenate([q_ref[...], qx_ref[...]], axis=2)
        ka = jnp.concatenate([kb, kxb], axis=2)
        s = _bdot_nt(qa, ka)
        if mask is not None:
            s = jnp.where(mask[None], s, -jnp.inf)
        m_cur = jnp.max(s, axis=2, keepdims=True)
        v1 = jnp.concatenate([vb, jnp.ones_like(vb)], axis=2)
        if first:
            m_col = m_cur
            pv = _bdot_nn(jnp.exp(s - m_col).astype(BF16), v1)
            l_scr[...] = pv[:, :, BLK:]
            acc_scr[...] = pv[:, :, :BLK]
            m_scr[...] = jnp.broadcast_to(m_col, m_scr.shape)
        else:
            m_prev = m_scr[...]
            m_new = jnp.maximum(m_prev, m_cur)
            m_col = m_new[:, :, 0:1]
            alpha = jnp.exp(m_prev - m_new)
            pv = _bdot_nn(jnp.exp(s - m_new).astype(BF16), v1)
            l_scr[...] = alpha * l_scr[...] + pv[:, :, BLK:]
            acc_scr[...] = alpha * acc_scr[...] + pv[:, :, :BLK]
            m_scr[...] = m_new
        return jnp.max(s[:, :, 0:1] - m_col)

    def real_block(start, width):
        start = pl.multiple_of(start, BLK)
        return (k_ref[:, pl.ds(start, width), :], kx_ref[:, pl.ds(start, width), :],
                v_ref[:, pl.ds(start, width), :])

    bound = bound_ref[0]
    first_start = jnp.maximum(i - (FOX_FIRST_BLOCKS - 1), 0) * BLK
    row2 = lax.broadcasted_iota(jnp.int32, (BLK, FOX_FIRST_BLOCKS * BLK), 0)
    col2 = lax.broadcasted_iota(jnp.int32, (BLK, FOX_FIRST_BLOCKS * BLK), 1)
    causal = col2 + (first_start - i * BLK) <= row2
    gap0 = sweep(*real_block(first_start, FOX_FIRST_BLOCKS * BLK), causal, True)

    def cond(c):
        j, done = c
        return jnp.logical_and(j >= 0, done == 0)

    def body(c):
        j, _ = c
        gap = sweep(*real_block(j * BLK, BLK), None, False)
        return j - 1, (gap + bound < -SKIP_LOG).astype(jnp.int32)

    _, done = lax.while_loop(cond, body, (i - FOX_FIRST_BLOCKS, (gap0 + bound < -SKIP_LOG).astype(jnp.int32)))

    @pl.when(done == 0)
    def _():
        col = lax.broadcasted_iota(jnp.int32, (BLK, BLK), 1)
        sweep(km_ref[...], kxm_ref[...], vm_ref[...], col < N_META, False)

    o = acc_scr[...] / l_scr[...]
    ms = jnp.mean(o * o, axis=-1, keepdims=True)
    gate = 1.0 / (1.0 + jnp.exp(-g_ref[...].astype(F32)))
    o_ref[...] = (o * lax.rsqrt(ms + EPS) * gain_ref[...] * gate).astype(BF16)


def _sb_kernel(q_ref, k_ref, v_ref, km_ref, vm_ref, gain_ref, o_ref, carry_scr, acc_scr):
    i = pl.program_id(1)

    def suffix_operator(width):
        r = lax.broadcasted_iota(jnp.int32, (width, width + BLK), 0)
        c = lax.broadcasted_iota(jnp.int32, (width, width + BLK), 1)
        return jnp.logical_or(c >= width, r > c).astype(BF16)

    def sweep(kb, vb, mask, first):
        width = kb.shape[1]
        suffix = suffix_operator(width)
        z = _bdot_nt(q_ref[...], kb)
        sp = jnp.maximum(z, 0.0) + jnp.log(1.0 + jnp.exp(-jnp.abs(z)))
        lk = -sp
        if mask is not None:
            lk = jnp.where(mask[None], lk, 0.0)
        hi = lk.astype(BF16)
        lo = (lk - hi.astype(F32)).astype(BF16)
        t = (jnp.dot(hi.reshape(N_HEADS * BLK, width), suffix, preferred_element_type=F32)
             + jnp.dot(lo.reshape(N_HEADS * BLK, width), suffix, preferred_element_type=F32))
        t = t.reshape(N_HEADS, BLK, width + BLK)
        later = t[:, :, :width]
        rowsum = t[:, :, width:]
        if not first:
            later = later + carry_scr[...]
        a = jnp.exp(z - sp + later)
        if mask is not None:
            a = jnp.where(mask[None], a, 0.0)
        pv = _bdot_nn(a.astype(BF16), vb)
        if first:
            acc_scr[...] = pv
            c_new = rowsum
        else:
            acc_scr[...] = acc_scr[...] + pv
            c_new = carry_scr[...] + rowsum
        carry_scr[...] = c_new
        return jnp.max(c_new[:, :, 0:1])

    def real_block(start, width):
        start = pl.multiple_of(start, BLK)
        return k_ref[:, pl.ds(start, width), :], v_ref[:, pl.ds(start, width), :]

    first_start = jnp.maximum(i - 1, 0) * BLK
    row2 = lax.broadcasted_iota(jnp.int32, (BLK, 2 * BLK), 0)
    col2 = lax.broadcasted_iota(jnp.int32, (BLK, 2 * BLK), 1)
    strict = col2 + (first_start - i * BLK) < row2
    top0 = sweep(*real_block(first_start, 2 * BLK), strict, True)

    def cond(c):
        j, done = c
        return jnp.logical_and(j >= 0, done == 0)

    def body(c):
        j, _ = c
        top = sweep(*real_block(j * BLK, BLK), None, False)
        return j - 1, (top < -SKIP_LOG).astype(jnp.int32)

    _, done = lax.while_loop(cond, body, (i - 2, (top0 < -SKIP_LOG).astype(jnp.int32)))

    @pl.when(done == 0)
    def _():
        col = lax.broadcasted_iota(jnp.int32, (BLK, BLK), 1)
        sweep(km_ref[...], vm_ref[...], col < N_META, False)

    o = acc_scr[...]
    ms = jnp.mean(o * o, axis=-1, keepdims=True)
    o_ref[...] = (o * lax.rsqrt(ms + EPS) * gain_ref[...]).astype(BF16)


def _head_spec_q(group, nq):
    return pl.BlockSpec((N_HEADS, BLK, HEAD_DIM), lambda b, i: (group, b * nq + i, 0))


def _head_spec_kv(group, seq):
    return pl.BlockSpec((N_HEADS, seq, HEAD_DIM), lambda b, i: (group, b, 0))


def _head_spec_meta(group):
    return pl.BlockSpec((N_HEADS, BLK, HEAD_DIM), lambda b, i: (group, 0, 0))


def _fox_attention(bound, proj, proj_m, qx, kx, kxm, out_gain, n_batch, seq):
    nq = seq // BLK
    rows = n_batch * seq
    return pl.pallas_call(
        _fox_kernel,
        grid=(n_batch, nq),
        in_specs=[
            pl.BlockSpec(memory_space=pltpu.SMEM),
            _head_spec_q(0, nq),
            _head_spec_q(0, nq),
            _head_spec_kv(1, seq),
            _head_spec_kv(0, seq),
            _head_spec_kv(2, seq),
            _head_spec_q(3, nq),
            _head_spec_meta(1),
            _head_spec_meta(0),
            _head_spec_meta(2),
            pl.BlockSpec((N_HEADS, 1, HEAD_DIM), lambda b, i: (0, 0, 0)),
        ],
        out_specs=pl.BlockSpec((N_HEADS, BLK, HEAD_DIM), lambda b, i: (0, b * nq + i, 0)),
        out_shape=jax.ShapeDtypeStruct((N_HEADS, rows, HEAD_DIM), BF16),
        scratch_shapes=[pltpu.VMEM((N_HEADS, BLK, LANES), F32)] * 3,
        compiler_params=_cparams(("parallel", "arbitrary")),
        name="fox_attention",
    )(bound, proj, qx, proj, kx, proj, proj, proj_m, kxm, proj_m, out_gain)


def _sb_attention(proj, proj_m, out_gain, n_batch, seq):
    nq = seq // BLK
    rows = n_batch * seq
    return pl.pallas_call(
        _sb_kernel,
        grid=(n_batch, nq),
        in_specs=[
            _head_spec_q(4, nq),
            _head_spec_kv(5, seq),
            _head_spec_kv(6, seq),
            _head_spec_meta(5),
            _head_spec_meta(6),
            pl.BlockSpec((N_HEADS, 1, HEAD_DIM), lambda b, i: (0, 0, 0)),
        ],
        out_specs=pl.BlockSpec((N_HEADS, BLK, HEAD_DIM), lambda b, i: (0, b * nq + i, 0)),
        out_shape=jax.ShapeDtypeStruct((N_HEADS, rows, HEAD_DIM), BF16),
        scratch_shapes=[pltpu.VMEM((N_HEADS, BLK, LANES), F32)] * 2,
        compiler_params=_cparams(("parallel", "arbitrary")),
        name="sb_attention",
    )(proj, proj, proj, proj_m, proj_m, out_gain)


ROUTE_TM = 256


def _router_kernel(oa_ref, ob_ref, wo_ref, x_ref, g_ref, whl_ref, b_ref,
                   h_ref, u_ref, route_ref, w0_ref, w1_ref, cnt_ref, carry_scr):
    step = pl.program_id(0)
    lhs = jnp.concatenate([oa_ref[h] for h in range(N_HEADS)] + [ob_ref[h] for h in range(N_HEADS)], axis=1)
    x = x_ref[...] + jnp.dot(lhs, wo_ref[...], preferred_element_type=F32)
    h_ref[...] = x
    tm = x.shape[0]
    ms = jnp.mean(x * x, axis=-1, keepdims=True)
    u = x * lax.rsqrt(ms + EPS) * g_ref[...]
    u_ref[...] = _pack_bf16_pairs(u)
    uhi = u.astype(BF16)
    ulo = (u - uhi.astype(F32)).astype(BF16)
    both = jnp.dot(uhi, whl_ref[...], preferred_element_type=F32)
    logits = (both[:, :LANES] + both[:, LANES:]
              + jnp.dot(ulo, whl_ref[:, :LANES], preferred_element_type=F32)) + b_ref[...]
    lane_i = lax.broadcasted_iota(jnp.int32, (tm, LANES), 1)
    lane = lane_i.astype(F32)
    big = float(4 * LANES)
    neg = -jnp.inf
    c = jnp.where(jnp.logical_and(lane_i >= N_EXPERTS, lane_i < N_EXPERTS + N_GROUPS), logits, neg)
    cmax = jnp.max(c, axis=1, keepdims=True)
    g_sel = jnp.min(jnp.where(c == cmax, lane, big), axis=1, keepdims=True) - N_EXPERTS
    g_gate = 1.0 / jnp.sum(jnp.exp(c - cmax), axis=1, keepdims=True)
    lo = g_sel * EXPERTS_PER_GROUP
    in_group = jnp.logical_and(lane >= lo, lane < lo + EXPERTS_PER_GROUP)
    f = jnp.where(in_group, logits, neg)
    t1 = jnp.max(f, axis=1, keepdims=True)
    i1 = jnp.min(jnp.where(f == t1, lane, big), axis=1, keepdims=True)
    f2 = jnp.where(lane == i1, neg, f)
    t2 = jnp.max(f2, axis=1, keepdims=True)
    i2 = jnp.min(jnp.where(f2 == t2, lane, big), axis=1, keepdims=True)
    d = jnp.exp(t2 - t1)
    w_first = g_gate / (1.0 + d)
    w0_ref[...] = jnp.broadcast_to(w_first, (tm, LANES))
    w1_ref[...] = jnp.broadcast_to(w_first * d, (tm, LANES))

    @pl.when(step == 0)
    def _():
        carry_scr[...] = jnp.zeros_like(carry_scr)

    oh0 = (lane == i1).astype(F32)
    oh1 = (lane == i2).astype(F32)
    oh = oh0 + oh1
    r = lax.broadcasted_iota(jnp.int32, (tm, tm), 0)
    cc = lax.broadcasted_iota(jnp.int32, (tm, tm), 1)
    before = (cc < r).astype(BF16)
    seen = jnp.dot(before, oh.astype(BF16), preferred_element_type=F32) + carry_scr[0:1, :]
    rank0 = jnp.sum(oh0 * seen, axis=1, keepdims=True)
    rank1 = jnp.sum(oh1 * seen, axis=1, keepdims=True)
    total = seen[tm - 1:tm, :] + oh[tm - 1:tm, :]
    carry_scr[...] = jnp.broadcast_to(total, carry_scr.shape)
    cnt_ref[...] = jnp.broadcast_to(total, cnt_ref.shape).astype(jnp.int32)
    vals = jnp.where(lane_i == 0, i1, jnp.where(lane_i == 1, i2, jnp.where(lane_i == 2, rank0,
                     jnp.where(lane_i == 3, rank1, 0.0))))
    route_ref[...] = vals.astype(jnp.int32)


def _outproj_router(oa, ob, wo_bf, x2d, norm_g, whl, b_pad):
    rows = x2d.shape[0]
    tm = ROUTE_TM
    return pl.pallas_call(
        _router_kernel,
        grid=(rows // tm,),
        in_specs=[
            pl.BlockSpec((N_HEADS, tm, HEAD_DIM), lambda i: (0, i, 0)),
            pl.BlockSpec((N_HEADS, tm, HEAD_DIM), lambda i: (0, i, 0)),
            pl.BlockSpec((2 * D_GROUP, D_MODEL), lambda i: (0, 0)),
            pl.BlockSpec((tm, D_MODEL), lambda i: (i, 0)),
            pl.BlockSpec((1, D_MODEL), lambda i: (0, 0)),
            pl.BlockSpec((D_MODEL, 2 * LANES), lambda i: (0, 0)),
            pl.BlockSpec((1, LANES), lambda i: (0, 0)),
        ],
        out_specs=[
            pl.BlockSpec((tm, D_MODEL), lambda i: (i, 0)),
            pl.BlockSpec((tm, D_PACKED), lambda i: (i, 0)),
            pl.BlockSpec((tm, LANES), lambda i: (i, 0)),
            pl.BlockSpec((tm, LANES), lambda i: (i, 0)),
            pl.BlockSpec((tm, LANES), lambda i: (i, 0)),
            pl.BlockSpec((8, LANES), lambda i: (0, 0)),
        ],
        out_shape=[
            jax.ShapeDtypeStruct((rows, D_MODEL), F32),
            jax.ShapeDtypeStruct((rows, D_PACKED), jnp.uint32),
            jax.ShapeDtypeStruct((rows, LANES), jnp.int32),
            jax.ShapeDtypeStruct((rows, LANES), F32),
            jax.ShapeDtypeStruct((rows, LANES), F32),
            jax.ShapeDtypeStruct((8, LANES), jnp.int32),
        ],
        scratch_shapes=[pltpu.VMEM((8, LANES), F32)],
        compiler_params=_cparams(("arbitrary",)),
        name="outproj_router",
    )(oa, ob, wo_bf, x2d, norm_g, whl, b_pad)


MOVE_TM = 256


def _scatter_kernel(pos_ref, u_ref, xs_ref, sem):
    base = pl.program_id(0) * MOVE_TM

    for t in range(MOVE_TM):
        for k in range(TOP_K):
            dst = pos_ref[(base + t) * TOP_K + k]
            pltpu.make_async_copy(u_ref.at[pl.ds(t, 1)], xs_ref.at[pl.ds(dst, 1)], sem).start(priority=k)
    for _ in range(TOP_K):
        pltpu.make_async_copy(u_ref, xs_ref.at[pl.ds(0, MOVE_TM)], sem).wait()


def _scatter_rows(pos_flat, u2):
    n_tok = u2.shape[0]
    return pl.pallas_call(
        _scatter_kernel,
        grid_spec=pltpu.PrefetchScalarGridSpec(
            num_scalar_prefetch=1,
            grid=(n_tok // MOVE_TM,),
            in_specs=[pl.BlockSpec((MOVE_TM, D_PACKED), lambda i, pos: (i, 0))],
            out_specs=pl.BlockSpec(memory_space=pl.ANY),
            scratch_shapes=[pltpu.SemaphoreType.DMA(())],
        ),
        out_shape=jax.ShapeDtypeStruct((n_tok * TOP_K, D_PACKED), u2.dtype),
        compiler_params=_cparams(("arbitrary",)),
        name="scatter_rows",
    )(pos_flat, u2)


FFN_TM = 128
WEIGHT_SLOTS = 3


def _ffn_kernel(tile_ref, exp_ref, lo_ref, hi_ref, first_ref, last_ref, new_ref, slot_ref, next_ref, next2_ref,
                x_ref, w1_hbm, w3_hbm, w2_hbm, o_ref,
                xb_scr, acc_scr, w1_buf, w3_buf, w2_buf, w1_bf, w3_bf, w2_bf, sem):
    it = pl.program_id(0)
    lo = lo_ref[it]
    hi = hi_ref[it]
    slot = slot_ref[it]

    def weight_copies(expert, s):
        half = D_EXPERT // 2
        return ((pltpu.make_async_copy(w1_hbm.at[expert], w1_buf.at[s], sem.at[s, 0]), 0),
                (pltpu.make_async_copy(w3_hbm.at[expert], w3_buf.at[s], sem.at[s, 1]), 1),
                (pltpu.make_async_copy(w2_hbm.at[expert, pl.ds(0, half)], w2_buf.at[s, pl.ds(0, half)],
                                       sem.at[s, 2]), 0),
                (pltpu.make_async_copy(w2_hbm.at[expert, pl.ds(half, half)], w2_buf.at[s, pl.ds(half, half)],
                                       sem.at[s, 3]), 1))

    def next_slot(s, ahead):
        s = s + ahead
        return jnp.where(s >= WEIGHT_SLOTS, s - WEIGHT_SLOTS, s)

    @pl.when(it == 0)
    def _():
        for cp, queue in weight_copies(exp_ref[0], slot):
            cp.start(priority=queue)

        @pl.when(next_ref[0] >= 0)
        def _():
            for cp, queue in weight_copies(next_ref[0], next_slot(slot, 1)):
                cp.start(priority=queue)

    @pl.when(new_ref[it] == 1)
    def _():
        @pl.when(next2_ref[it] >= 0)
        def _():
            for cp, queue in weight_copies(next2_ref[it], next_slot(slot, 2)):
                cp.start(priority=queue)

        for cp, _ in weight_copies(exp_ref[it], slot):
            cp.wait()

        w1_bf[...] = w1_buf[slot].astype(BF16)
        w3_bf[...] = w3_buf[slot].astype(BF16)
        w2_bf[...] = w2_buf[slot].astype(BF16)

    @pl.when(first_ref[it] == 1)
    def _():
        xb_scr[...] = _unpack_bf16_pairs(x_ref[...]).astype(BF16)
        acc_scr[...] = jnp.zeros_like(acc_scr)

    @pl.when(hi > lo)
    def _():
        x = xb_scr[...]
        a = jnp.dot(x, w1_bf[...], preferred_element_type=F32)
        b = jnp.dot(x, w3_bf[...], preferred_element_type=F32)
        mid = a / (1.0 + jnp.exp(-a)) * b
        rows = tile_ref[it] * FFN_TM + lax.broadcasted_iota(jnp.int32, mid.shape, 0)
        mid = jnp.where(jnp.logical_and(rows >= lo, rows < hi), mid, 0.0).astype(BF16)
        acc_scr[...] += jnp.dot(mid, w2_bf[...], preferred_element_type=F32)

    @pl.when(last_ref[it] == 1)
    def _():
        o_ref[...] = _pack_bf16_pairs(acc_scr[...])


def _grouped_ffn(items, xs, w1, w3, w2):
    n_items = items[0].shape[0]
    n_rows = xs.shape[0]

    def tile_map(i, t, *_):
        return (t[i], 0)

    return pl.pallas_call(
        _ffn_kernel,
        grid_spec=pltpu.PrefetchScalarGridSpec(
            num_scalar_prefetch=len(items),
            grid=(n_items,),
            in_specs=[
                pl.BlockSpec((FFN_TM, D_PACKED), tile_map),
                pl.BlockSpec(memory_space=pl.ANY),
                pl.BlockSpec(memory_space=pl.ANY),
                pl.BlockSpec(memory_space=pl.ANY),
            ],
            out_specs=pl.BlockSpec((FFN_TM, D_PACKED), tile_map),
            scratch_shapes=[
                pltpu.VMEM((FFN_TM, D_MODEL), BF16),
                pltpu.VMEM((FFN_TM, D_MODEL), F32),
                pltpu.VMEM((WEIGHT_SLOTS, D_MODEL, D_EXPERT), F32),
                pltpu.VMEM((WEIGHT_SLOTS, D_MODEL, D_EXPERT), F32),
                pltpu.VMEM((WEIGHT_SLOTS, D_EXPERT, D_MODEL), F32),
                pltpu.VMEM((D_MODEL, D_EXPERT), BF16),
                pltpu.VMEM((D_MODEL, D_EXPERT), BF16),
                pltpu.VMEM((D_EXPERT, D_MODEL), BF16),
                pltpu.SemaphoreType.DMA((WEIGHT_SLOTS, 4)),
            ],
        ),
        out_shape=jax.ShapeDtypeStruct((n_rows, D_PACKED), jnp.uint32),
        compiler_params=_cparams(("arbitrary",)),
        name="grouped_ffn",
    )(*items, xs, w1, w3, w2)


def _combine_kernel(pos_ref, h_ref, w0_ref, w1_ref, ys_ref, o_ref, ybuf, sem):
    step = pl.program_id(0)
    n_steps = pl.num_programs(0)

    def fetch(s, slot):
        for t in range(MOVE_TM):
            for k in range(TOP_K):
                src = pos_ref[(s * MOVE_TM + t) * TOP_K + k]
                pltpu.make_async_copy(ys_ref.at[pl.ds(src, 1)], ybuf.at[slot, k, pl.ds(t, 1)],
                                      sem.at[slot]).start(priority=k)

    @pl.when(step == 0)
    def _():
        fetch(0, 0)

    slot = step % 2

    @pl.when(step + 1 < n_steps)
    def _():
        fetch(step + 1, 1 - slot)

    for k in range(TOP_K):
        pltpu.make_async_copy(ys_ref.at[pl.ds(0, MOVE_TM)], ybuf.at[slot, k], sem.at[slot]).wait()
    reps = D_MODEL // LANES
    w0 = jnp.concatenate([w0_ref[...]] * reps, axis=1)
    w1 = jnp.concatenate([w1_ref[...]] * reps, axis=1)
    o_ref[...] = (h_ref[...] + w0 * _unpack_bf16_pairs(ybuf[slot, 0])
                  + w1 * _unpack_bf16_pairs(ybuf[slot, 1]))


def _combine(pos_flat, h1, w0b, w1b, ys):
    rows = h1.shape[0]
    tm = MOVE_TM
    return pl.pallas_call(
        _combine_kernel,
        grid_spec=pltpu.PrefetchScalarGridSpec(
            num_scalar_prefetch=1,
            grid=(rows // tm,),
            in_specs=[
                pl.BlockSpec((tm, D_MODEL), lambda i, pos: (i, 0)),
                pl.BlockSpec((tm, LANES), lambda i, pos: (i, 0)),
                pl.BlockSpec((tm, LANES), lambda i, pos: (i, 0)),
                pl.BlockSpec(memory_space=pl.ANY),
            ],
            out_specs=pl.BlockSpec((tm, D_MODEL), lambda i, pos: (i, 0)),
            scratch_shapes=[pltpu.VMEM((2, TOP_K, tm, D_PACKED), jnp.uint32), pltpu.SemaphoreType.DMA((2,))],
        ),
        out_shape=jax.ShapeDtypeStruct((rows, D_MODEL), F32),
        compiler_params=_cparams(("arbitrary",)),
        name="combine",
    )(pos_flat, h1, w0b, w1b, ys)


def _work_items(counts, n_assign):
    ends = jnp.cumsum(counts)
    starts = (ends - counts).astype(jnp.int32)
    n_tiles = n_assign // FFN_TM
    tile_starts = jnp.arange(n_tiles, dtype=jnp.int32) * FFN_TM
    seg_lo = jnp.sort(jnp.concatenate([tile_starts, starts]))
    seg_hi = jnp.concatenate([seg_lo[1:], jnp.array([n_assign], jnp.int32)])
    item_tile = jnp.minimum(seg_lo // FFN_TM, n_tiles - 1).astype(jnp.int32)
    item_exp = jnp.minimum(jnp.sum((ends[None, :] <= seg_lo[:, None]).astype(jnp.int32), axis=1), N_EXPERTS - 1)
    item_exp = item_exp.astype(jnp.int32)
    item_first = (seg_lo == item_tile * FFN_TM).astype(jnp.int32)
    n_items = seg_lo.shape[0]
    item_new = jnp.concatenate([jnp.ones((1,), jnp.int32), (item_exp[1:] != item_exp[:-1]).astype(jnp.int32)])
    ordinal = jnp.cumsum(item_new) - 1
    run_exp = jnp.full((n_items + 2,), -1, jnp.int32).at[ordinal].set(item_exp)
    item_next = run_exp[ordinal + 1]
    item_next2 = run_exp[ordinal + 2]
    item_slot = (ordinal % WEIGHT_SLOTS).astype(jnp.int32)
    item_last = (seg_hi == (item_tile + 1) * FFN_TM).astype(jnp.int32)
    items = (item_tile, item_exp, seg_lo, seg_hi, item_first, item_last, item_new, item_slot, item_next,
             item_next2)
    return starts, items


def kernel(x, meta_tokens, norm1_g, w_in, b_f, q_gain, k_gain, fox_out_gain, sb_out_gain, w_out, norm2_g,
           w_coarse, b_coarse, w_fine, b_fine, w1, w3, w2):
    assert norm1_g.shape[0] == 1, "single-layer block"
    n_batch, seq, _ = x.shape
    n_tok = n_batch * seq
    n_assign = n_tok * TOP_K
    scale = HEAD_DIM ** -0.5
    x2d = x.reshape(n_tok, D_MODEL)

    w_in_t = w_in[0].T
    w_out_bf = _cast_bf16(w_out[0], 2 * D_GROUP)
    wft_bf = jnp.pad(w_in_t[D_PROJ:], ((0, LANES - N_HEADS), (0, 0))).astype(BF16)
    bf_pad = jnp.pad(b_f[0], (0, LANES - N_HEADS)).reshape(1, LANES)
    ones = jnp.ones((HEAD_DIM,), F32)
    zeros = jnp.zeros((HEAD_DIM,), F32)
    gains = jnp.stack([jnp.stack([q_gain[0] * scale, ones]), jnp.stack([k_gain[0], ones]),
                       jnp.stack([ones, zeros]), jnp.stack([ones, zeros]), jnp.stack([ones * scale, zeros]),
                       jnp.stack([ones, zeros]), jnp.stack([ones, zeros])])
    meta_pad = jnp.pad(meta_tokens.astype(F32), ((0, BLK - N_META), (0, 0)))
    n1 = norm1_g[0].reshape(1, D_MODEL)

    proj_m, lf_m, w_in_bf = _inproj(meta_pad, n1, w_in_t, wft_bf, bf_pad, gains, tm=BLK)
    proj, lf = _inproj(x2d, n1, w_in_bf, wft_bf, bf_pad, gains, tm=1024)
    qx, kx, kxm = _cumgate(lf_m, lf, n_batch, seq)
    qk_bound = 1.02 * HEAD_DIM * scale * jnp.max(jnp.abs(q_gain[0])) * jnp.max(jnp.abs(k_gain[0]))
    bound = (2.0 * qk_bound).reshape(1).astype(F32)
    oa = _fox_attention(bound, proj, proj_m, qx, kx, kxm, fox_out_gain[0].reshape(N_HEADS, 1, HEAD_DIM),
                        n_batch, seq)
    ob = _sb_attention(proj, proj_m, sb_out_gain[0].reshape(N_HEADS, 1, HEAD_DIM), n_batch, seq)

    wr = jnp.pad(jnp.concatenate([w_fine[0], w_coarse[0]], axis=1),
                 ((0, 0), (0, LANES - N_GROUPS - N_EXPERTS)))
    wr_hi = wr.astype(BF16)
    wr_hl = jnp.concatenate([wr_hi, (wr - wr_hi.astype(F32)).astype(BF16)], axis=1)
    br = jnp.pad(jnp.concatenate([b_fine[0], b_coarse[0]]), (0, LANES - N_GROUPS - N_EXPERTS)).reshape(1, LANES)
    h1, u2, route, w0b, w1b, cnt = _outproj_router(oa, ob, w_out_bf, x2d, norm2_g[0].reshape(1, D_MODEL),
                                                   wr_hl, br)
    starts, items = _work_items(cnt[0, :N_EXPERTS], n_assign)
    eid = route[:, 0:TOP_K]
    start_of = jnp.sum(jnp.where(eid[..., None] == jnp.arange(N_EXPERTS, dtype=jnp.int32), starts, 0), axis=-1)
    pos = (start_of + route[:, TOP_K:2 * TOP_K]).reshape(-1)
    xs = _scatter_rows(pos, u2)
    ys = _grouped_ffn(items, xs, w1[0], w3[0], w2[0])
    out = _combine(pos, h1, w0b, w1b, ys)
    return out.reshape(n_batch, seq, D_MODEL)
```

```python
import functools

import jax
import jax.numpy as jnp
from jax import lax
from jax.experimental import pallas as pl
from jax.experimental.pallas import tpu as pltpu

F32 = jnp.float32
BF16 = jnp.bfloat16

D_MODEL = 2048
N_META = 16
HEAD_DIM = 128
N_HEADS = 8
D_GROUP = N_HEADS * HEAD_DIM
N_PROJ_GROUPS = 7
D_PROJ = N_PROJ_GROUPS * D_GROUP
N_GROUPS = 8
EXPERTS_PER_GROUP = 8
N_EXPERTS = 64
TOP_K = 2
D_EXPERT = 512
EPS = 1e-6
LANES = 128
D_PACKED = D_MODEL // 2
MXU_HEADS = 2
INPROJ_TN = 14 * HEAD_DIM
BLK = 128
FOX_FIRST_BLOCKS = 3
SKIP_LOG = 88.0
VMEM_LIMIT = 56 * 1024 * 1024


def _cparams(sem, vmem=VMEM_LIMIT):
    return pltpu.CompilerParams(dimension_semantics=sem, vmem_limit_bytes=vmem)


def _log_sigmoid(x):
    return jnp.minimum(x, 0.0) - jnp.log(1.0 + jnp.exp(-jnp.abs(x)))


def _split3(x):
    hi = x.astype(BF16)
    r1 = x - hi.astype(F32)
    mid = r1.astype(BF16)
    lo = (r1 - mid.astype(F32)).astype(BF16)
    return hi, mid, lo


def _dot_nt(a, b):
    return lax.dot_general(a, b, (((1,), (1,)), ((), ())), preferred_element_type=F32)


def _pack_bf16_pairs(x):
    half = x.shape[1] // 2
    xr = x.astype(BF16).astype(F32)
    hi = lax.bitcast_convert_type(xr[:, :half], jnp.uint32)
    lo = lax.bitcast_convert_type(xr[:, half:], jnp.uint32)
    return hi | lax.shift_right_logical(lo, jnp.uint32(16))


def _unpack_bf16_pairs(p):
    hi = lax.bitcast_convert_type(p & jnp.uint32(0xFFFF0000), F32)
    lo = lax.bitcast_convert_type(lax.shift_left(p, jnp.uint32(16)), F32)
    return jnp.concatenate([hi, lo], axis=1)


def _cast_kernel(x_ref, o_ref):
    o_ref[...] = x_ref[...].astype(o_ref.dtype)


def _cast_bf16(w, n_rows, tr=512):
    n_cols = w.shape[1]
    return pl.pallas_call(
        _cast_kernel,
        grid=(n_rows // tr,),
        in_specs=[pl.BlockSpec((tr, n_cols), lambda i: (i, 0))],
        out_specs=pl.BlockSpec((tr, n_cols), lambda i: (i, 0)),
        out_shape=jax.ShapeDtypeStruct((n_rows, n_cols), BF16),
        compiler_params=_cparams(("parallel",)),
        name="cast_bf16",
    )(w)


def _inproj_kernel(x_ref, g_ref, wt_ref, wft_ref, bf_ref, gain_ref, o_ref, lf_ref, *rest, heads_per_tile):
    wbf_ref = rest[0] if len(rest) == 2 else None
    u_ref = rest[-1]
    j = pl.program_id(1)

    @pl.when(j == 0)
    def _():
        x = x_ref[...]
        ms = jnp.mean(x * x, axis=-1, keepdims=True)
        u = (x * lax.rsqrt(ms + EPS) * g_ref[...]).astype(BF16)
        u_ref[...] = u
        f = _dot_nt(u, wft_ref[...]) + bf_ref[...]
        lf_ref[...] = _log_sigmoid(f)

    u = u_ref[...]
    for s in range(heads_per_tile // MXU_HEADS):
        gain = gain_ref[s * MXU_HEADS, 0:1, :]
        normed = gain_ref[s * MXU_HEADS, 1:2, :] > 0.5
        cols = pl.ds(s * MXU_HEADS * HEAD_DIM, MXU_HEADS * HEAD_DIM)
        w = wt_ref[cols, :]
        if wbf_ref is not None:
            w = w.astype(BF16)
            wbf_ref[cols, :] = w
        acc = _dot_nt(u, w)
        for hh in range(MXU_HEADS):
            y = acc[:, hh * HEAD_DIM:(hh + 1) * HEAD_DIM]
            ms = jnp.mean(y * y, axis=-1, keepdims=True)
            scale = jnp.where(normed, lax.rsqrt(ms + EPS), 1.0)
            o_ref[s * MXU_HEADS + hh] = (y * scale * gain).astype(BF16)


def _inproj(x2d, norm_g, wt, wft_bf, bf_pad, gains, tm, tn=D_GROUP):
    rows = x2d.shape[0]
    hpt = tn // HEAD_DIM
    kern = functools.partial(_inproj_kernel, heads_per_tile=hpt)
    out_specs = [
        pl.BlockSpec((hpt, tm, HEAD_DIM), lambda i, j: (j, i, 0)),
        pl.BlockSpec((tm, LANES), lambda i, j: (i, 0)),
    ]
    out_shape = [
        jax.ShapeDtypeStruct((D_PROJ // HEAD_DIM, rows, HEAD_DIM), BF16),
        jax.ShapeDtypeStruct((rows, LANES), F32),
    ]
    if wt.dtype != BF16:
        assert rows == tm, "the bf16 weight copy is written once per column tile"
        out_specs.append(pl.BlockSpec((tn, D_MODEL), lambda i, j: (j, 0)))
        out_shape.append(jax.ShapeDtypeStruct((D_PROJ, D_MODEL), BF16))
    return pl.pallas_call(
        kern,
        grid=(rows // tm, D_PROJ // tn),
        in_specs=[
            pl.BlockSpec((tm, D_MODEL), lambda i, j: (i, 0)),
            pl.BlockSpec((1, D_MODEL), lambda i, j: (0, 0)),
            pl.BlockSpec((tn, D_MODEL), lambda i, j: (j, 0)),
            pl.BlockSpec((LANES, D_MODEL), lambda i, j: (0, 0)),
            pl.BlockSpec((1, LANES), lambda i, j: (0, 0)),
            pl.BlockSpec((hpt, 2, HEAD_DIM), lambda i, j: (j, 0, 0)),
        ],
        out_specs=out_specs,
        out_shape=out_shape,
        scratch_shapes=[pltpu.VMEM((tm, D_MODEL), BF16)],
        compiler_params=_cparams(("parallel", "arbitrary")),
        name="inproj",
    )(x2d, norm_g, wt, wft_bf, bf_pad, gains)


def _cumgate_kernel(lfm_ref, lf_ref, qx_ref, kx_ref, kxm_ref, *, n_blk):
    row = lax.broadcasted_iota(jnp.int32, (BLK, BLK), 0)
    col = lax.broadcasted_iota(jnp.int32, (BLK, BLK), 1)
    tri = (col <= row).astype(BF16)

    def prefix(x):
        hi, mid, lo = _split3(x)
        return (jnp.dot(tri, hi, preferred_element_type=F32)
                + jnp.dot(tri, mid, preferred_element_type=F32)
                + jnp.dot(tri, lo, preferred_element_type=F32))

    one = jnp.ones((BLK, LANES), F32)
    zero = jnp.zeros((BLK, LANES), F32)

    def ext(cum, h):
        c = jnp.broadcast_to(cum[:, h:h + 1], (BLK, LANES))
        hi, mid, lo = (t.astype(F32) for t in _split3(c))
        qx = jnp.where(col == 0, hi, jnp.where(col == 1, mid, jnp.where(col == 2, lo,
                       jnp.where(col < 6, one, zero))))
        kx = jnp.where(col < 3, one, jnp.where(col == 3, -hi, jnp.where(col == 4, -mid,
                       jnp.where(col == 5, -lo, zero))))
        return qx.astype(BF16), kx.astype(BF16)

    lfm = jnp.where(row < N_META, lfm_ref[...], 0.0)
    cum_m = prefix(lfm)
    for h in range(N_HEADS):
        _, kx = ext(cum_m, h)
        kxm_ref[h] = kx
    carry = cum_m[BLK - 1:BLK, :]
    for b in range(n_blk):
        cum = prefix(lf_ref[b * BLK:(b + 1) * BLK, :]) + carry
        carry = cum[BLK - 1:BLK, :]
        for h in range(N_HEADS):
            qx, kx = ext(cum, h)
            qx_ref[h, b * BLK:(b + 1) * BLK, :] = qx
            kx_ref[h, b * BLK:(b + 1) * BLK, :] = kx


def _cumgate(lf_meta, lf_real, n_batch, seq):
    kern = functools.partial(_cumgate_kernel, n_blk=seq // BLK)
    return pl.pallas_call(
        kern,
        grid=(n_batch,),
        in_specs=[
            pl.BlockSpec((BLK, LANES), lambda b: (0, 0)),
            pl.BlockSpec((seq, LANES), lambda b: (b, 0)),
        ],
        out_specs=[
            pl.BlockSpec((N_HEADS, seq, LANES), lambda b: (0, b, 0)),
            pl.BlockSpec((N_HEADS, seq, LANES), lambda b: (0, b, 0)),
            pl.BlockSpec((N_HEADS, BLK, LANES), lambda b: (0, 0, 0)),
        ],
        out_shape=[
            jax.ShapeDtypeStruct((N_HEADS, n_batch * seq, LANES), BF16),
            jax.ShapeDtypeStruct((N_HEADS, n_batch * seq, LANES), BF16),
            jax.ShapeDtypeStruct((N_HEADS, BLK, LANES), BF16),
        ],
        compiler_params=_cparams(("arbitrary",)),
        name="cumgate",
    )(lf_meta, lf_real)


def _bdot_nt(a, b):
    return lax.dot_general(a, b, (((2,), (2,)), ((0,), (0,))), preferred_element_type=F32)


def _bdot_nn(a, b):
    return lax.dot_general(a, b, (((2,), (1,)), ((0,), (0,))), preferred_element_type=F32)


def _fox_kernel(bound_ref, q_ref, qx_ref, k_ref, kx_ref, v_ref, g_ref, km_ref, kxm_ref, vm_ref, gain_ref,
                o_ref, m_scr, l_scr, acc_scr):
    i = pl.program_id(1)

    def sweep(kb, kxb, vb, mask, first):
        qa = jnp.concatenate([q_ref[...], qx_ref[...]], axis=2)
        ka = jnp.concatenate([kb, kxb], axis=2)
        s = _bdot_nt(qa, ka)
        if mask is not None:
            s = jnp.where(mask[None], s, -jnp.inf)
        m_cur = jnp.max(s, axis=2, keepdims=True)
        v1 = jnp.concatenate([vb, jnp.ones_like(vb)], axis=2)
        if first:
            m_col = m_cur
            pv = _bdot_nn(jnp.exp(s - m_col).astype(BF16), v1)
            l_scr[...] = pv[:, :, BLK:]
            acc_scr[...] = pv[:, :, :BLK]
            m_scr[...] = jnp.broadcast_to(m_col, m_scr.shape)
        else:
            m_prev = m_scr[...]
            m_new = jnp.maximum(m_prev, m_cur)
            m_col = m_new[:, :, 0:1]
            alpha = jnp.exp(m_prev - m_new)
            pv = _bdot_nn(jnp.exp(s - m_new).astype(BF16), v1)
            l_scr[...] = alpha * l_scr[...] + pv[:, :, BLK:]
            acc_scr[...] = alpha * acc_scr[...] + pv[:, :, :BLK]
            m_scr[...] = m_new
        return jnp.max(s[:, :, 0:1] - m_col)

    def real_block(start, width):
        start = pl.multiple_of(start, BLK)
        return (k_ref[:, pl.ds(start, width), :], kx_ref[:, pl.ds(start, width), :],
                v_ref[:, pl.ds(start, width), :])

    bound = bound_ref[0]
    first_start = jnp.maximum(i - (FOX_FIRST_BLOCKS - 1), 0) * BLK
    row2 = lax.broadcasted_iota(jnp.int32, (BLK, FOX_FIRST_BLOCKS * BLK), 0)
    col2 = lax.broadcasted_iota(jnp.int32, (BLK, FOX_FIRST_BLOCKS * BLK), 1)
    causal = col2 + (first_start - i * BLK) <= row2
    gap0 = sweep(*real_block(first_start, FOX_FIRST_BLOCKS * BLK), causal, True)

    def cond(c):
        j, done = c
        return jnp.logical_and(j >= 0, done == 0)

    def body(c):
        j, _ = c
        gap = sweep(*real_block(j * BLK, BLK), None, False)
        return j - 1, (gap + bound < -SKIP_LOG).astype(jnp.int32)

    _, done = lax.while_loop(cond, body, (i - FOX_FIRST_BLOCKS, (gap0 + bound < -SKIP_LOG).astype(jnp.int32)))

    @pl.when(done == 0)
    def _():
        col = lax.broadcasted_iota(jnp.int32, (BLK, BLK), 1)
        sweep(km_ref[...], kxm_ref[...], vm_ref[...], col < N_META, False)

    o = acc_scr[...] / l_scr[...]
    ms = jnp.mean(o * o, axis=-1, keepdims=True)
    gate = 1.0 / (1.0 + jnp.exp(-g_ref[...].astype(F32)))
    o_ref[...] = (o * lax.rsqrt(ms + EPS) * gain_ref[...] * gate).astype(BF16)


def _sb_kernel(q_ref, k_ref, v_ref, km_ref, vm_ref, gain_ref, o_ref, carry_scr, acc_scr):
    i = pl.program_id(1)

    def suffix_operator(width):
        r = lax.broadcasted_iota(jnp.int32, (width, width + BLK), 0)
        c = lax.broadcasted_iota(jnp.int32, (width, width + BLK), 1)
        return jnp.logical_or(c >= width, r > c).astype(BF16)

    def sweep(kb, vb, mask, first):
        width = kb.shape[1]
        suffix = suffix_operator(width)
        z = _bdot_nt(q_ref[...], kb)
        sp = jnp.maximum(z, 0.0) + jnp.log(1.0 + jnp.exp(-jnp.abs(z)))
        lk = -sp
        if mask is not None:
            lk = jnp.where(mask[None], lk, 0.0)
        hi = lk.astype(BF16)
        lo = (lk - hi.astype(F32)).astype(BF16)
        t = (jnp.dot(hi.reshape(N_HEADS * BLK, width), suffix, preferred_element_type=F32)
             + jnp.dot(lo.reshape(N_HEADS * BLK, width), suffix, preferred_element_type=F32))
        t = t.reshape(N_HEADS, BLK, width + BLK)
        later = t[:, :, :width]
        rowsum = t[:, :, width:]
        if not first:
            later = later + carry_scr[...]
        a = jnp.exp(z - sp + later)
        if mask is not None:
            a = jnp.where(mask[None], a, 0.0)
        pv = _bdot_nn(a.astype(BF16), vb)
        if first:
            acc_scr[...] = pv
            c_new = rowsum
        else:
            acc_scr[...] = acc_scr[...] + pv
            c_new = carry_scr[...] + rowsum
        carry_scr[...] = c_new
        return jnp.max(c_new[:, :, 0:1])

    def real_block(start, width):
        start = pl.multiple_of(start, BLK)
        return k_ref[:, pl.ds(start, width), :], v_ref[:, pl.ds(start, width), :]

    first_start = jnp.maximum(i - 1, 0) * BLK
    row2 = lax.broadcasted_iota(jnp.int32, (BLK, 2 * BLK), 0)
    col2 = lax.broadcasted_iota(jnp.int32, (BLK, 2 * BLK), 1)
    strict = col2 + (first_start - i * BLK) < row2
    top0 = sweep(*real_block(first_start, 2 * BLK), strict, True)

    def cond(c):
        j, done = c
        return jnp.logical_and(j >= 0, done == 0)

    def body(c):
        j, _ = c
        top = sweep(*real_block(j * BLK, BLK), None, False)
        return j - 1, (top < -SKIP_LOG).astype(jnp.int32)

    _, done = lax.while_loop(cond, body, (i - 2, (top0 < -SKIP_LOG).astype(jnp.int32)))

    @pl.when(done == 0)
    def _():
        col = lax.broadcasted_iota(jnp.int32, (BLK, BLK), 1)
        sweep(km_ref[...], vm_ref[...], col < N_META, False)

    o = acc_scr[...]
    ms = jnp.mean(o * o, axis=-1, keepdims=True)
    o_ref[...] = (o * lax.rsqrt(ms + EPS) * gain_ref[...]).astype(BF16)


def _head_spec_q(group, nq):
    return pl.BlockSpec((N_HEADS, BLK, HEAD_DIM), lambda b, i: (group, b * nq + i, 0))


def _head_spec_kv(group, seq):
    return pl.BlockSpec((N_HEADS, seq, HEAD_DIM), lambda b, i: (group, b, 0))


def _head_spec_meta(group):
    return pl.BlockSpec((N_HEADS, BLK, HEAD_DIM), lambda b, i: (group, 0, 0))


def _fox_attention(bound, proj, proj_m, qx, kx, kxm, out_gain, n_batch, seq):
    nq = seq // BLK
    rows = n_batch * seq
    return pl.pallas_call(
        _fox_kernel,
        grid=(n_batch, nq),
        in_specs=[
            pl.BlockSpec(memory_space=pltpu.SMEM),
            _head_spec_q(0, nq),
            _head_spec_q(0, nq),
            _head_spec_kv(1, seq),
            _head_spec_kv(0, seq),
            _head_spec_kv(2, seq),
            _head_spec_q(3, nq),
            _head_spec_meta(1),
            _head_spec_meta(0),
            _head_spec_meta(2),
            pl.BlockSpec((N_HEADS, 1, HEAD_DIM), lambda b, i: (0, 0, 0)),
        ],
        out_specs=pl.BlockSpec((N_HEADS, BLK, HEAD_DIM), lambda b, i: (0, b * nq + i, 0)),
        out_shape=jax.ShapeDtypeStruct((N_HEADS, rows, HEAD_DIM), BF16),
        scratch_shapes=[pltpu.VMEM((N_HEADS, BLK, LANES), F32)] * 3,
        compiler_params=_cparams(("parallel", "arbitrary")),
        name="fox_attention",
    )(bound, proj, qx, proj, kx, proj, proj, proj_m, kxm, proj_m, out_gain)


def _sb_attention(proj, proj_m, out_gain, n_batch, seq):
    nq = seq // BLK
    rows = n_batch * seq
    return pl.pallas_call(
        _sb_kernel,
        grid=(n_batch, nq),
        in_specs=[
            _head_spec_q(4, nq),
            _head_spec_kv(5, seq),
            _head_spec_kv(6, seq),
            _head_spec_meta(5),
            _head_spec_meta(6),
            pl.BlockSpec((N_HEADS, 1, HEAD_DIM), lambda b, i: (0, 0, 0)),
        ],
        out_specs=pl.BlockSpec((N_HEADS, BLK, HEAD_DIM), lambda b, i: (0, b * nq + i, 0)),
        out_shape=jax.ShapeDtypeStruct((N_HEADS, rows, HEAD_DIM), BF16),
        scratch_shapes=[pltpu.VMEM((N_HEADS, BLK, LANES), F32)] * 2,
        compiler_params=_cparams(("parallel", "arbitrary")),
        name="sb_attention",
    )(proj, proj, proj, proj_m, proj_m, out_gain)


ROUTE_TM = 512


def _router_kernel(oa_ref, ob_ref, wo_ref, x_ref, g_ref, whl_ref, b_ref,
                   h_ref, u_ref, route_ref, w0_ref, w1_ref, cnt_ref, carry_scr):
    step = pl.program_id(0)
    lhs = jnp.concatenate([oa_ref[h] for h in range(N_HEADS)] + [ob_ref[h] for h in range(N_HEADS)], axis=1)
    x = x_ref[...] + jnp.dot(lhs, wo_ref[...], preferred_element_type=F32)
    h_ref[...] = x
    tm = x.shape[0]
    ms = jnp.mean(x * x, axis=-1, keepdims=True)
    u = x * lax.rsqrt(ms + EPS) * g_ref[...]
    u_ref[...] = _pack_bf16_pairs(u)
    uhi = u.astype(BF16)
    ulo = (u - uhi.astype(F32)).astype(BF16)
    both = jnp.dot(uhi, whl_ref[...], preferred_element_type=F32)
    logits = (both[:, :LANES] + both[:, LANES:]
              + jnp.dot(ulo, whl_ref[:, :LANES], preferred_element_type=F32)) + b_ref[...]
    lane_i = lax.broadcasted_iota(jnp.int32, (tm, LANES), 1)
    lane = lane_i.astype(F32)
    big = float(4 * LANES)
    neg = -jnp.inf
    c = jnp.where(jnp.logical_and(lane_i >= N_EXPERTS, lane_i < N_EXPERTS + N_GROUPS), logits, neg)
    cmax = jnp.max(c, axis=1, keepdims=True)
    g_sel = jnp.min(jnp.where(c == cmax, lane, big), axis=1, keepdims=True) - N_EXPERTS
    g_gate = 1.0 / jnp.sum(jnp.exp(c - cmax), axis=1, keepdims=True)
    lo = g_sel * EXPERTS_PER_GROUP
    in_group = jnp.logical_and(lane >= lo, lane < lo + EXPERTS_PER_GROUP)
    f = jnp.where(in_group, logits, neg)
    t1 = jnp.max(f, axis=1, keepdims=True)
    i1 = jnp.min(jnp.where(f == t1, lane, big), axis=1, keepdims=True)
    f2 = jnp.where(lane == i1, neg, f)
    t2 = jnp.max(f2, axis=1, keepdims=True)
    i2 = jnp.min(jnp.where(f2 == t2, lane, big), axis=1, keepdims=True)
    d = jnp.exp(t2 - t1)
    w_first = g_gate / (1.0 + d)
    w0_ref[...] = jnp.broadcast_to(w_first, (tm, LANES))
    w1_ref[...] = jnp.broadcast_to(w_first * d, (tm, LANES))

    @pl.when(step == 0)
    def _():
        carry_scr[...] = jnp.zeros_like(carry_scr)

    oh0 = (lane == i1).astype(F32)
    oh1 = (lane == i2).astype(F32)
    oh = oh0 + oh1
    r = lax.broadcasted_iota(jnp.int32, (tm, tm), 0)
    cc = lax.broadcasted_iota(jnp.int32, (tm, tm), 1)
    before = (cc < r).astype(BF16)
    seen = jnp.dot(before, oh.astype(BF16), preferred_element_type=F32) + carry_scr[0:1, :]
    rank0 = jnp.sum(oh0 * seen, axis=1, keepdims=True)
    rank1 = jnp.sum(oh1 * seen, axis=1, keepdims=True)
    total = seen[tm - 1:tm, :] + oh[tm - 1:tm, :]
    carry_scr[...] = jnp.broadcast_to(total, carry_scr.shape)
    cnt_ref[...] = jnp.broadcast_to(total, cnt_ref.shape).astype(jnp.int32)
    vals = jnp.where(lane_i == 0, i1, jnp.where(lane_i == 1, i2, jnp.where(lane_i == 2, rank0,
                     jnp.where(lane_i == 3, rank1, 0.0))))
    route_ref[...] = vals.astype(jnp.int32)


def _outproj_router(oa, ob, wo_bf, x2d, norm_g, whl, b_pad):
    rows = x2d.shape[0]
    tm = ROUTE_TM
    return pl.pallas_call(
        _router_kernel,
        grid=(rows // tm,),
        in_specs=[
            pl.BlockSpec((N_HEADS, tm, HEAD_DIM), lambda i: (0, i, 0)),
            pl.BlockSpec((N_HEADS, tm, HEAD_DIM), lambda i: (0, i, 0)),
            pl.BlockSpec((2 * D_GROUP, D_MODEL), lambda i: (0, 0)),
            pl.BlockSpec((tm, D_MODEL), lambda i: (i, 0)),
            pl.BlockSpec((1, D_MODEL), lambda i: (0, 0)),
            pl.BlockSpec((D_MODEL, 2 * LANES), lambda i: (0, 0)),
            pl.BlockSpec((1, LANES), lambda i: (0, 0)),
        ],
        out_specs=[
            pl.BlockSpec((tm, D_MODEL), lambda i: (i, 0)),
            pl.BlockSpec((tm, D_PACKED), lambda i: (i, 0)),
            pl.BlockSpec((tm, LANES), lambda i: (i, 0)),
            pl.BlockSpec((tm, LANES), lambda i: (i, 0)),
            pl.BlockSpec((tm, LANES), lambda i: (i, 0)),
            pl.BlockSpec((8, LANES), lambda i: (0, 0)),
        ],
        out_shape=[
            jax.ShapeDtypeStruct((rows, D_MODEL), F32),
            jax.ShapeDtypeStruct((rows, D_PACKED), jnp.uint32),
            jax.ShapeDtypeStruct((rows, LANES), jnp.int32),
            jax.ShapeDtypeStruct((rows, LANES), F32),
            jax.ShapeDtypeStruct((rows, LANES), F32),
            jax.ShapeDtypeStruct((8, LANES), jnp.int32),
        ],
        scratch_shapes=[pltpu.VMEM((8, LANES), F32)],
        compiler_params=_cparams(("arbitrary",)),
        name="outproj_router",
    )(oa, ob, wo_bf, x2d, norm_g, whl, b_pad)


MOVE_TM = 256


def _scatter_kernel(pos_ref, u_ref, xs_ref, sem):
    base = pl.program_id(0) * MOVE_TM

    for t in range(MOVE_TM):
        for k in range(TOP_K):
            dst = pos_ref[(base + t) * TOP_K + k]
            pltpu.make_async_copy(u_ref.at[pl.ds(t, 1)], xs_ref.at[pl.ds(dst, 1)], sem).start(priority=k)
    for _ in range(TOP_K):
        pltpu.make_async_copy(u_ref, xs_ref.at[pl.ds(0, MOVE_TM)], sem).wait()


def _scatter_rows(pos_flat, u2):
    n_tok = u2.shape[0]
    return pl.pallas_call(
        _scatter_kernel,
        grid_spec=pltpu.PrefetchScalarGridSpec(
            num_scalar_prefetch=1,
            grid=(n_tok // MOVE_TM,),
            in_specs=[pl.BlockSpec((MOVE_TM, D_PACKED), lambda i, pos: (i, 0))],
            out_specs=pl.BlockSpec(memory_space=pl.ANY),
            scratch_shapes=[pltpu.SemaphoreType.DMA(())],
        ),
        out_shape=jax.ShapeDtypeStruct((n_tok * TOP_K, D_PACKED), u2.dtype),
        compiler_params=_cparams(("arbitrary",)),
        name="scatter_rows",
    )(pos_flat, u2)


FFN_TM = 128
WEIGHT_SLOTS = 3


def _ffn_kernel(tile_ref, exp_ref, lo_ref, hi_ref, first_ref, last_ref, new_ref, slot_ref, next_ref, next2_ref,
                x_ref, w1_hbm, w3_hbm, w2_hbm, o_ref,
                xb_scr, acc_scr, w1_buf, w3_buf, w2_buf, w1_bf, w3_bf, w2_bf, sem):
    it = pl.program_id(0)
    lo = lo_ref[it]
    hi = hi_ref[it]
    slot = slot_ref[it]

    def weight_copies(expert, s):
        half = D_EXPERT // 2
        return ((pltpu.make_async_copy(w1_hbm.at[expert], w1_buf.at[s], sem.at[s, 0]), 0),
                (pltpu.make_async_copy(w3_hbm.at[expert], w3_buf.at[s], sem.at[s, 1]), 1),
                (pltpu.make_async_copy(w2_hbm.at[expert, pl.ds(0, half)], w2_buf.at[s, pl.ds(0, half)],
                                       sem.at[s, 2]), 0),
                (pltpu.make_async_copy(w2_hbm.at[expert, pl.ds(half, half)], w2_buf.at[s, pl.ds(half, half)],
                                       sem.at[s, 3]), 1))

    def next_slot(s, ahead):
        s = s + ahead
        return jnp.where(s >= WEIGHT_SLOTS, s - WEIGHT_SLOTS, s)

    @pl.when(it == 0)
    def _():
        for cp, queue in weight_copies(exp_ref[0], slot):
            cp.start(priority=queue)

        @pl.when(next_ref[0] >= 0)
        def _():
            for cp, queue in weight_copies(next_ref[0], next_slot(slot, 1)):
                cp.start(priority=queue)

    @pl.when(new_ref[it] == 1)
    def _():
        @pl.when(next2_ref[it] >= 0)
        def _():
            for cp, queue in weight_copies(next2_ref[it], next_slot(slot, 2)):
                cp.start(priority=queue)

        for cp, _ in weight_copies(exp_ref[it], slot):
            cp.wait()

        w1_bf[...] = w1_buf[slot].astype(BF16)
        w3_bf[...] = w3_buf[slot].astype(BF16)
        w2_bf[...] = w2_buf[slot].astype(BF16)

    @pl.when(first_ref[it] == 1)
    def _():
        xb_scr[...] = _unpack_bf16_pairs(x_ref[...]).astype(BF16)
        acc_scr[...] = jnp.zeros_like(acc_scr)

    @pl.when(hi > lo)
    def _():
        x = xb_scr[...]
        a = jnp.dot(x, w1_bf[...], preferred_element_type=F32)
        b = jnp.dot(x, w3_bf[...], preferred_element_type=F32)
        mid = a / (1.0 + jnp.exp(-a)) * b
        rows = tile_ref[it] * FFN_TM + lax.broadcasted_iota(jnp.int32, mid.shape, 0)
        mid = jnp.where(jnp.logical_and(rows >= lo, rows < hi), mid, 0.0).astype(BF16)
        acc_scr[...] += jnp.dot(mid, w2_bf[...], preferred_element_type=F32)

    @pl.when(last_ref[it] == 1)
    def _():
        o_ref[...] = _pack_bf16_pairs(acc_scr[...])


def _grouped_ffn(items, xs, w1, w3, w2):
    n_items = items[0].shape[0]
    n_rows = xs.shape[0]

    def tile_map(i, t, *_):
        return (t[i], 0)

    return pl.pallas_call(
        _ffn_kernel,
        grid_spec=pltpu.PrefetchScalarGridSpec(
            num_scalar_prefetch=len(items),
            grid=(n_items,),
            in_specs=[
                pl.BlockSpec((FFN_TM, D_PACKED), tile_map),
                pl.BlockSpec(memory_space=pl.ANY),
                pl.BlockSpec(memory_space=pl.ANY),
                pl.BlockSpec(memory_space=pl.ANY),
            ],
            out_specs=pl.BlockSpec((FFN_TM, D_PACKED), tile_map),
            scratch_shapes=[
                pltpu.VMEM((FFN_TM, D_MODEL), BF16),
                pltpu.VMEM((FFN_TM, D_MODEL), F32),
                pltpu.VMEM((WEIGHT_SLOTS, D_MODEL, D_EXPERT), F32),
                pltpu.VMEM((WEIGHT_SLOTS, D_MODEL, D_EXPERT), F32),
                pltpu.VMEM((WEIGHT_SLOTS, D_EXPERT, D_MODEL), F32),
                pltpu.VMEM((D_MODEL, D_EXPERT), BF16),
                pltpu.VMEM((D_MODEL, D_EXPERT), BF16),
                pltpu.VMEM((D_EXPERT, D_MODEL), BF16),
                pltpu.SemaphoreType.DMA((WEIGHT_SLOTS, 4)),
            ],
        ),
        out_shape=jax.ShapeDtypeStruct((n_rows, D_PACKED), jnp.uint32),
        compiler_params=_cparams(("arbitrary",)),
        name="grouped_ffn",
    )(*items, xs, w1, w3, w2)


def _combine_kernel(pos_ref, h_ref, w0_ref, w1_ref, ys_ref, o_ref, ybuf, sem):
    step = pl.program_id(0)
    n_steps = pl.num_programs(0)

    def fetch(s, slot):
        for t in range(MOVE_TM):
            for k in range(TOP_K):
                src = pos_ref[(s * MOVE_TM + t) * TOP_K + k]
                pltpu.make_async_copy(ys_ref.at[pl.ds(src, 1)], ybuf.at[slot, k, pl.ds(t, 1)],
                                      sem.at[slot]).start(priority=k)

    @pl.when(step == 0)
    def _():
        fetch(0, 0)

    slot = step % 2

    @pl.when(step + 1 < n_steps)
    def _():
        fetch(step + 1, 1 - slot)

    for k in range(TOP_K):
        pltpu.make_async_copy(ys_ref.at[pl.ds(0, MOVE_TM)], ybuf.at[slot, k], sem.at[slot]).wait()
    reps = D_MODEL // LANES
    w0 = jnp.concatenate([w0_ref[...]] * reps, axis=1)
    w1 = jnp.concatenate([w1_ref[...]] * reps, axis=1)
    o_ref[...] = (h_ref[...] + w0 * _unpack_bf16_pairs(ybuf[slot, 0])
                  + w1 * _unpack_bf16_pairs(ybuf[slot, 1]))


def _combine(pos_flat, h1, w0b, w1b, ys):
    rows = h1.shape[0]
    tm = MOVE_TM
    return pl.pallas_call(
        _combine_kernel,
        grid_spec=pltpu.PrefetchScalarGridSpec(
            num_scalar_prefetch=1,
            grid=(rows // tm,),
            in_specs=[
                pl.BlockSpec((tm, D_MODEL), lambda i, pos: (i, 0)),
                pl.BlockSpec((tm, LANES), lambda i, pos: (i, 0)),
                pl.BlockSpec((tm, LANES), lambda i, pos: (i, 0)),
                pl.BlockSpec(memory_space=pl.ANY),
            ],
            out_specs=pl.BlockSpec((tm, D_MODEL), lambda i, pos: (i, 0)),
            scratch_shapes=[pltpu.VMEM((2, TOP_K, tm, D_PACKED), jnp.uint32), pltpu.SemaphoreType.DMA((2,))],
        ),
        out_shape=jax.ShapeDtypeStruct((rows, D_MODEL), F32),
        compiler_params=_cparams(("arbitrary",)),
        name="combine",
    )(pos_flat, h1, w0b, w1b, ys)


def _work_items(counts, n_assign):
    ends = jnp.cumsum(counts)
    starts = (ends - counts).astype(jnp.int32)
    n_tiles = n_assign // FFN_TM
    tile_starts = jnp.arange(n_tiles, dtype=jnp.int32) * FFN_TM
    seg_lo = jnp.sort(jnp.concatenate([tile_starts, starts]))
    seg_hi = jnp.concatenate([seg_lo[1:], jnp.array([n_assign], jnp.int32)])
    item_tile = jnp.minimum(seg_lo // FFN_TM, n_tiles - 1).astype(jnp.int32)
    item_exp = jnp.minimum(jnp.sum((ends[None, :] <= seg_lo[:, None]).astype(jnp.int32), axis=1), N_EXPERTS - 1)
    item_exp = item_exp.astype(jnp.int32)
    item_first = (seg_lo == item_tile * FFN_TM).astype(jnp.int32)
    n_items = seg_lo.shape[0]
    item_new = jnp.concatenate([jnp.ones((1,), jnp.int32), (item_exp[1:] != item_exp[:-1]).astype(jnp.int32)])
    ordinal = jnp.cumsum(item_new) - 1
    run_exp = jnp.full((n_items + 2,), -1, jnp.int32).at[ordinal].set(item_exp)
    item_next = run_exp[ordinal + 1]
    item_next2 = run_exp[ordinal + 2]
    item_slot = (ordinal % WEIGHT_SLOTS).astype(jnp.int32)
    item_last = (seg_hi == (item_tile + 1) * FFN_TM).astype(jnp.int32)
    items = (item_tile, item_exp, seg_lo, seg_hi, item_first, item_last, item_new, item_slot, item_next,
             item_next2)
    return starts, items


def kernel(x, meta_tokens, norm1_g, w_in, b_f, q_gain, k_gain, fox_out_gain, sb_out_gain, w_out, norm2_g,
           w_coarse, b_coarse, w_fine, b_fine, w1, w3, w2):
    assert norm1_g.shape[0] == 1, "single-layer block"
    n_batch, seq, _ = x.shape
    n_tok = n_batch * seq
    n_assign = n_tok * TOP_K
    scale = HEAD_DIM ** -0.5
    x2d = x.reshape(n_tok, D_MODEL)

    w_in_t = w_in[0].T
    w_out_bf = _cast_bf16(w_out[0], 2 * D_GROUP)
    wft_bf = jnp.pad(w_in_t[D_PROJ:], ((0, LANES - N_HEADS), (0, 0))).astype(BF16)
    bf_pad = jnp.pad(b_f[0], (0, LANES - N_HEADS)).reshape(1, LANES)
    ones = jnp.ones((HEAD_DIM,), F32)
    zeros = jnp.zeros((HEAD_DIM,), F32)
    gains = jnp.stack([jnp.stack([q_gain[0] * scale, ones]), jnp.stack([k_gain[0], ones]),
                       jnp.stack([ones, zeros]), jnp.stack([ones, zeros]), jnp.stack([ones * scale, zeros]),
                       jnp.stack([ones, zeros]), jnp.stack([ones, zeros])])
    gains = jnp.repeat(gains, N_HEADS, axis=0)
    meta_pad = jnp.pad(meta_tokens.astype(F32), ((0, BLK - N_META), (0, 0)))
    n1 = norm1_g[0].reshape(1, D_MODEL)

    proj_m, lf_m, w_in_bf = _inproj(meta_pad, n1, w_in_t, wft_bf, bf_pad, gains, tm=BLK)
    proj, lf = _inproj(x2d, n1, w_in_bf, wft_bf, bf_pad, gains, tm=1024, tn=INPROJ_TN)
    qx, kx, kxm = _cumgate(lf_m, lf, n_batch, seq)
    qk_bound = 1.02 * HEAD_DIM * scale * jnp.max(jnp.abs(q_gain[0])) * jnp.max(jnp.abs(k_gain[0]))
    bound = (2.0 * qk_bound).reshape(1).astype(F32)
    oa = _fox_attention(bound, proj, proj_m, qx, kx, kxm, fox_out_gain[0].reshape(N_HEADS, 1, HEAD_DIM),
                        n_batch, seq)
    ob = _sb_attention(proj, proj_m, sb_out_gain[0].reshape(N_HEADS, 1, HEAD_DIM), n_batch, seq)

    wr = jnp.pad(jnp.concatenate([w_fine[0], w_coarse[0]], axis=1),
                 ((0, 0), (0, LANES - N_GROUPS - N_EXPERTS)))
    wr_hi = wr.astype(BF16)
    wr_hl = jnp.concatenate([wr_hi, (wr - wr_hi.astype(F32)).astype(BF16)], axis=1)
    br = jnp.pad(jnp.concatenate([b_fine[0], b_coarse[0]]), (0, LANES - N_GROUPS - N_EXPERTS)).reshape(1, LANES)
    h1, u2, route, w0b, w1b, cnt = _outproj_router(oa, ob, w_out_bf, x2d, norm2_g[0].reshape(1, D_MODEL),
                                                   wr_hl, br)
    starts, items = _work_items(cnt[0, :N_EXPERTS], n_assign)
    eid = route[:, 0:TOP_K]
    start_of = jnp.sum(jnp.where(eid[..., None] == jnp.arange(N_EXPERTS, dtype=jnp.int32), starts, 0), axis=-1)
    pos = (start_of + route[:, TOP_K:2 * TOP_K]).reshape(-1)
    xs = _scatter_rows(pos, u2)
    ys = _grouped_ffn(items, xs, w1[0], w3[0], w2[0])
    out = _combine(pos, h1, w0b, w1b, ys)
    return out.reshape(n_batch, seq, D_MODEL)
```

```python
import functools

import jax
import jax.numpy as jnp
from jax import lax
from jax.experimental import pallas as pl
from jax.experimental.pallas import tpu as pltpu

F32 = jnp.float32
BF16 = jnp.bfloat16

D_MODEL = 2048
N_META = 16
HEAD_DIM = 128
N_HEADS = 8
D_GROUP = N_HEADS * HEAD_DIM
N_PROJ_GROUPS = 7
D_PROJ = N_PROJ_GROUPS * D_GROUP
N_GROUPS = 8
EXPERTS_PER_GROUP = 8
N_EXPERTS = 64
TOP_K = 2
D_EXPERT = 512
EPS = 1e-6
LANES = 128
D_PACKED = D_MODEL // 2
MXU_HEADS = 2
INPROJ_TN = 14 * HEAD_DIM
BLK = 128
FOX_FIRST_BLOCKS = 3
SKIP_LOG = 88.0
VMEM_LIMIT = 56 * 1024 * 1024


def _cparams(sem, vmem=VMEM_LIMIT):
    return pltpu.CompilerParams(dimension_semantics=sem, vmem_limit_bytes=vmem)


def _log_sigmoid(x):
    return jnp.minimum(x, 0.0) - jnp.log(1.0 + jnp.exp(-jnp.abs(x)))


def _split3(x):
    hi = x.astype(BF16)
    r1 = x - hi.astype(F32)
    mid = r1.astype(BF16)
    lo = (r1 - mid.astype(F32)).astype(BF16)
    return hi, mid, lo


def _dot_nt(a, b):
    return lax.dot_general(a, b, (((1,), (1,)), ((), ())), preferred_element_type=F32)


def _pack_bf16_pairs(x):
    half = x.shape[1] // 2
    xr = x.astype(BF16).astype(F32)
    hi = lax.bitcast_convert_type(xr[:, :half], jnp.uint32)
    lo = lax.bitcast_convert_type(xr[:, half:], jnp.uint32)
    return hi | lax.shift_right_logical(lo, jnp.uint32(16))


def _unpack_bf16_pairs(p):
    hi = lax.bitcast_convert_type(p & jnp.uint32(0xFFFF0000), F32)
    lo = lax.bitcast_convert_type(lax.shift_left(p, jnp.uint32(16)), F32)
    return jnp.concatenate([hi, lo], axis=1)


def _cast_kernel(x_ref, o_ref):
    o_ref[...] = x_ref[...].astype(o_ref.dtype)


def _cast_bf16(w, n_rows, tr=512):
    n_cols = w.shape[1]
    return pl.pallas_call(
        _cast_kernel,
        grid=(n_rows // tr,),
        in_specs=[pl.BlockSpec((tr, n_cols), lambda i: (i, 0))],
        out_specs=pl.BlockSpec((tr, n_cols), lambda i: (i, 0)),
        out_shape=jax.ShapeDtypeStruct((n_rows, n_cols), BF16),
        compiler_params=_cparams(("parallel",)),
        name="cast_bf16",
    )(w)


def _inproj_kernel(x_ref, g_ref, wt_ref, wft_ref, bf_ref, gain_ref, o_ref, lf_ref, *rest, heads_per_tile):
    wbf_ref = rest[0] if len(rest) == 2 else None
    u_ref = rest[-1]
    j = pl.program_id(1)

    @pl.when(j == 0)
    def _():
        x = x_ref[...]
        ms = jnp.mean(x * x, axis=-1, keepdims=True)
        u = (x * lax.rsqrt(ms + EPS) * g_ref[...]).astype(BF16)
        u_ref[...] = u
        f = _dot_nt(u, wft_ref[...]) + bf_ref[...]
        lf_ref[...] = _log_sigmoid(f)

    u = u_ref[...]
    for s in range(heads_per_tile // MXU_HEADS):
        gain = gain_ref[s * MXU_HEADS, 0:1, :]
        normed = gain_ref[s * MXU_HEADS, 1:2, :] > 0.5
        cols = pl.ds(s * MXU_HEADS * HEAD_DIM, MXU_HEADS * HEAD_DIM)
        w = wt_ref[cols, :]
        if wbf_ref is not None:
            w = w.astype(BF16)
            wbf_ref[cols, :] = w
        acc = _dot_nt(u, w)
        for hh in range(MXU_HEADS):
            y = acc[:, hh * HEAD_DIM:(hh + 1) * HEAD_DIM]
            ms = jnp.mean(y * y, axis=-1, keepdims=True)
            scale = jnp.where(normed, lax.rsqrt(ms + EPS), 1.0)
            o_ref[s * MXU_HEADS + hh] = (y * scale * gain).astype(BF16)


def _inproj(x2d, norm_g, wt, wft_bf, bf_pad, gains, tm, tn=D_GROUP):
    rows = x2d.shape[0]
    hpt = tn // HEAD_DIM
    kern = functools.partial(_inproj_kernel, heads_per_tile=hpt)
    out_specs = [
        pl.BlockSpec((hpt, tm, HEAD_DIM), lambda i, j: (j, i, 0)),
        pl.BlockSpec((tm, LANES), lambda i, j: (i, 0)),
    ]
    out_shape = [
        jax.ShapeDtypeStruct((D_PROJ // HEAD_DIM, rows, HEAD_DIM), BF16),
        jax.ShapeDtypeStruct((rows, LANES), F32),
    ]
    if wt.dtype != BF16:
        assert rows == tm, "the bf16 weight copy is written once per column tile"
        out_specs.append(pl.BlockSpec((tn, D_MODEL), lambda i, j: (j, 0)))
        out_shape.append(jax.ShapeDtypeStruct((D_PROJ, D_MODEL), BF16))
    return pl.pallas_call(
        kern,
        grid=(rows // tm, D_PROJ // tn),
        in_specs=[
            pl.BlockSpec((tm, D_MODEL), lambda i, j: (i, 0)),
            pl.BlockSpec((1, D_MODEL), lambda i, j: (0, 0)),
            pl.BlockSpec((tn, D_MODEL), lambda i, j: (j, 0)),
            pl.BlockSpec((LANES, D_MODEL), lambda i, j: (0, 0)),
            pl.BlockSpec((1, LANES), lambda i, j: (0, 0)),
            pl.BlockSpec((hpt, 2, HEAD_DIM), lambda i, j: (j, 0, 0)),
        ],
        out_specs=out_specs,
        out_shape=out_shape,
        scratch_shapes=[pltpu.VMEM((tm, D_MODEL), BF16)],
        compiler_params=_cparams(("parallel", "arbitrary")),
        name="inproj",
    )(x2d, norm_g, wt, wft_bf, bf_pad, gains)


def _cumgate_kernel(lfm_ref, lf_ref, qx_ref, kx_ref, kxm_ref, *, n_blk):
    row = lax.broadcasted_iota(jnp.int32, (BLK, BLK), 0)
    col = lax.broadcasted_iota(jnp.int32, (BLK, BLK), 1)
    tri = (col <= row).astype(BF16)

    def prefix(x):
        hi, mid, lo = _split3(x)
        return (jnp.dot(tri, hi, preferred_element_type=F32)
                + jnp.dot(tri, mid, preferred_element_type=F32)
                + jnp.dot(tri, lo, preferred_element_type=F32))

    one = jnp.ones((BLK, LANES), F32)
    zero = jnp.zeros((BLK, LANES), F32)

    def ext(cum, h):
        c = jnp.broadcast_to(cum[:, h:h + 1], (BLK, LANES))
        hi, mid, lo = (t.astype(F32) for t in _split3(c))
        qx = jnp.where(col == 0, hi, jnp.where(col == 1, mid, jnp.where(col == 2, lo,
                       jnp.where(col < 6, one, zero))))
        kx = jnp.where(col < 3, one, jnp.where(col == 3, -hi, jnp.where(col == 4, -mid,
                       jnp.where(col == 5, -lo, zero))))
        return qx.astype(BF16), kx.astype(BF16)

    lfm = jnp.where(row < N_META, lfm_ref[...], 0.0)
    cum_m = prefix(lfm)
    for h in range(N_HEADS):
        _, kx = ext(cum_m, h)
        kxm_ref[h] = kx
    carry = cum_m[BLK - 1:BLK, :]
    for b in range(n_blk):
        cum = prefix(lf_ref[b * BLK:(b + 1) * BLK, :]) + carry
        carry = cum[BLK - 1:BLK, :]
        for h in range(N_HEADS):
            qx, kx = ext(cum, h)
            qx_ref[h, b * BLK:(b + 1) * BLK, :] = qx
            kx_ref[h, b * BLK:(b + 1) * BLK, :] = kx


def _cumgate(lf_meta, lf_real, n_batch, seq):
    kern = functools.partial(_cumgate_kernel, n_blk=seq // BLK)
    return pl.pallas_call(
        kern,
        grid=(n_batch,),
        in_specs=[
            pl.BlockSpec((BLK, LANES), lambda b: (0, 0)),
            pl.BlockSpec((seq, LANES), lambda b: (b, 0)),
        ],
        out_specs=[
            pl.BlockSpec((N_HEADS, seq, LANES), lambda b: (0, b, 0)),
            pl.BlockSpec((N_HEADS, seq, LANES), lambda b: (0, b, 0)),
            pl.BlockSpec((N_HEADS, BLK, LANES), lambda b: (0, 0, 0)),
        ],
        out_shape=[
            jax.ShapeDtypeStruct((N_HEADS, n_batch * seq, LANES), BF16),
            jax.ShapeDtypeStruct((N_HEADS, n_batch * seq, LANES), BF16),
            jax.ShapeDtypeStruct((N_HEADS, BLK, LANES), BF16),
        ],
        compiler_params=_cparams(("arbitrary",)),
        name="cumgate",
    )(lf_meta, lf_real)


def _bdot_nt(a, b):
    return lax.dot_general(a, b, (((2,), (2,)), ((0,), (0,))), preferred_element_type=F32)


def _bdot_nn(a, b):
    return lax.dot_general(a, b, (((2,), (1,)), ((0,), (0,))), preferred_element_type=F32)


def _fox_kernel(bound_ref, q_ref, qx_ref, k_ref, kx_ref, v_ref, g_ref, km_ref, kxm_ref, vm_ref, gain_ref,
                o_ref, m_scr, l_scr, acc_scr):
    i = pl.program_id(1)

    def sweep(kb, kxb, vb, mask, first):
        qa = jnp.concatenate([q_ref[...], qx_ref[...]], axis=2)
        ka = jnp.concatenate([kb, kxb], axis=2)
        s = _bdot_nt(qa, ka)
        if mask is not None:
            s = jnp.where(mask[None], s, -jnp.inf)
        m_cur = jnp.max(s, axis=2, keepdims=True)
        v1 = jnp.concatenate([vb, jnp.ones_like(vb)], axis=2)
        if first:
            m_col = m_cur
            pv = _bdot_nn(jnp.exp(s - m_col).astype(BF16), v1)
            l_scr[...] = pv[:, :, BLK:]
            acc_scr[...] = pv[:, :, :BLK]
            m_scr[...] = jnp.broadcast_to(m_col, m_scr.shape)
        else:
            m_prev = m_scr[...]
            m_new = jnp.maximum(m_prev, m_cur)
            m_col = m_new[:, :, 0:1]
            alpha = jnp.exp(m_prev - m_new)
            pv = _bdot_nn(jnp.exp(s - m_new).astype(BF16), v1)
            l_scr[...] = alpha * l_scr[...] + pv[:, :, BLK:]
            acc_scr[...] = alpha * acc_scr[...] + pv[:, :, :BLK]
            m_scr[...] = m_new
        return jnp.max(s[:, :, 0:1] - m_col)

    def real_block(start, width):
        start = pl.multiple_of(start, BLK)
        return (k_ref[:, pl.ds(start, width), :], kx_ref[:, pl.ds(start, width), :],
                v_ref[:, pl.ds(start, width), :])

    bound = bound_ref[0]
    first_start = jnp.maximum(i - (FOX_FIRST_BLOCKS - 1), 0) * BLK
    row2 = lax.broadcasted_iota(jnp.int32, (BLK, FOX_FIRST_BLOCKS * BLK), 0)
    col2 = lax.broadcasted_iota(jnp.int32, (BLK, FOX_FIRST_BLOCKS * BLK), 1)
    causal = col2 + (first_start - i * BLK) <= row2
    gap0 = sweep(*real_block(first_start, FOX_FIRST_BLOCKS * BLK), causal, True)

    def cond(c):
        j, done = c
        return jnp.logical_and(j >= 0, done == 0)

    def body(c):
        j, _ = c
        gap = sweep(*real_block(j * BLK, BLK), None, False)
        return j - 1, (gap + bound < -SKIP_LOG).astype(jnp.int32)

    _, done = lax.while_loop(cond, body, (i - FOX_FIRST_BLOCKS, (gap0 + bound < -SKIP_LOG).astype(jnp.int32)))

    @pl.when(done == 0)
    def _():
        col = lax.broadcasted_iota(jnp.int32, (BLK, BLK), 1)
        sweep(km_ref[...], kxm_ref[...], vm_ref[...], col < N_META, False)

    o = acc_scr[...] / l_scr[...]
    ms = jnp.mean(o * o, axis=-1, keepdims=True)
    gate = 1.0 / (1.0 + jnp.exp(-g_ref[...].astype(F32)))
    o_ref[...] = (o * lax.rsqrt(ms + EPS) * gain_ref[...] * gate).astype(BF16)


def _sb_kernel(q_ref, k_ref, v_ref, km_ref, vm_ref, gain_ref, o_ref, carry_scr, acc_scr):
    i = pl.program_id(1)

    def suffix_operator(width):
        r = lax.broadcasted_iota(jnp.int32, (width, width + BLK), 0)
        c = lax.broadcasted_iota(jnp.int32, (width, width + BLK), 1)
        return jnp.logical_or(c >= width, r > c).astype(BF16)

    def sweep(kb, vb, mask, first):
        width = kb.shape[1]
        suffix = suffix_operator(width)
        z = _bdot_nt(q_ref[...], kb)
        sp = jnp.maximum(z, 0.0) + jnp.log(1.0 + jnp.exp(-jnp.abs(z)))
        lk = -sp
        if mask is not None:
            lk = jnp.where(mask[None], lk, 0.0)
        hi = lk.astype(BF16)
        lo = (lk - hi.astype(F32)).astype(BF16)
        t = (jnp.dot(hi.reshape(N_HEADS * BLK, width), suffix, preferred_element_type=F32)
             + jnp.dot(lo.reshape(N_HEADS * BLK, width), suffix, preferred_element_type=F32))
        t = t.reshape(N_HEADS, BLK, width + BLK)
        later = t[:, :, :width]
        rowsum = t[:, :, width:]
        if not first:
            later = later + carry_scr[...]
        a = jnp.exp(z - sp + later)
        if mask is not None:
            a = jnp.where(mask[None], a, 0.0)
        pv = _bdot_nn(a.astype(BF16), vb)
        if first:
            acc_scr[...] = pv
            c_new = rowsum
        else:
            acc_scr[...] = acc_scr[...] + pv
            c_new = carry_scr[...] + rowsum
        carry_scr[...] = c_new
        return jnp.max(c_new[:, :, 0:1])

    def real_block(start, width):
        start = pl.multiple_of(start, BLK)
        return k_ref[:, pl.ds(start, width), :], v_ref[:, pl.ds(start, width), :]

    first_start = jnp.maximum(i - 1, 0) * BLK
    row2 = lax.broadcasted_iota(jnp.int32, (BLK, 2 * BLK), 0)
    col2 = lax.broadcasted_iota(jnp.int32, (BLK, 2 * BLK), 1)
    strict = col2 + (first_start - i * BLK) < row2
    top0 = sweep(*real_block(first_start, 2 * BLK), strict, True)

    def cond(c):
        j, done = c
        return jnp.logical_and(j >= 0, done == 0)

    def body(c):
        j, _ = c
        top = sweep(*real_block(j * BLK, BLK), None, False)
        return j - 1, (top < -SKIP_LOG).astype(jnp.int32)

    _, done = lax.while_loop(cond, body, (i - 2, (top0 < -SKIP_LOG).astype(jnp.int32)))

    @pl.when(done == 0)
    def _():
        col = lax.broadcasted_iota(jnp.int32, (BLK, BLK), 1)
        sweep(km_ref[...], vm_ref[...], col < N_META, False)

    o = acc_scr[...]
    ms = jnp.mean(o * o, axis=-1, keepdims=True)
    o_ref[...] = (o * lax.rsqrt(ms + EPS) * gain_ref[...]).astype(BF16)


def _head_spec_q(group, nq):
    return pl.BlockSpec((N_HEADS, BLK, HEAD_DIM), lambda b, i: (group, b * nq + i, 0))


def _head_spec_kv(group, seq):
    return pl.BlockSpec((N_HEADS, seq, HEAD_DIM), lambda b, i: (group, b, 0))


def _head_spec_meta(group):
    return pl.BlockSpec((N_HEADS, BLK, HEAD_DIM), lambda b, i: (group, 0, 0))


def _fox_attention(bound, proj, proj_m, qx, kx, kxm, out_gain, n_batch, seq):
    nq = seq // BLK
    rows = n_batch * seq
    return pl.pallas_call(
        _fox_kernel,
        grid=(n_batch, nq),
        in_specs=[
            pl.BlockSpec(memory_space=pltpu.SMEM),
            _head_spec_q(0, nq),
            _head_spec_q(0, nq),
            _head_spec_kv(1, seq),
            _head_spec_kv(0, seq),
            _head_spec_kv(2, seq),
            _head_spec_q(3, nq),
            _head_spec_meta(1),
            _head_spec_meta(0),
            _head_spec_meta(2),
            pl.BlockSpec((N_HEADS, 1, HEAD_DIM), lambda b, i: (0, 0, 0)),
        ],
        out_specs=pl.BlockSpec((N_HEADS, BLK, HEAD_DIM), lambda b, i: (0, b * nq + i, 0)),
        out_shape=jax.ShapeDtypeStruct((N_HEADS, rows, HEAD_DIM), BF16),
        scratch_shapes=[pltpu.VMEM((N_HEADS, BLK, LANES), F32)] * 3,
        compiler_params=_cparams(("parallel", "arbitrary")),
        name="fox_attention",
    )(bound, proj, qx, proj, kx, proj, proj, proj_m, kxm, proj_m, out_gain)


def _sb_attention(proj, proj_m, out_gain, n_batch, seq):
    nq = seq // BLK
    rows = n_batch * seq
    return pl.pallas_call(
        _sb_kernel,
        grid=(n_batch, nq),
        in_specs=[
            _head_spec_q(4, nq),
            _head_spec_kv(5, seq),
            _head_spec_kv(6, seq),
            _head_spec_meta(5),
            _head_spec_meta(6),
            pl.BlockSpec((N_HEADS, 1, HEAD_DIM), lambda b, i: (0, 0, 0)),
        ],
        out_specs=pl.BlockSpec((N_HEADS, BLK, HEAD_DIM), lambda b, i: (0, b * nq + i, 0)),
        out_shape=jax.ShapeDtypeStruct((N_HEADS, rows, HEAD_DIM), BF16),
        scratch_shapes=[pltpu.VMEM((N_HEADS, BLK, LANES), F32)] * 2,
        compiler_params=_cparams(("parallel", "arbitrary")),
        name="sb_attention",
    )(proj, proj, proj, proj_m, proj_m, out_gain)


ROUTE_TM = 512


def _router_kernel(oa_ref, ob_ref, wo_ref, x_ref, g_ref, whl_ref, b_ref,
                   h_ref, u_ref, route_ref, w0_ref, w1_ref, cnt_ref, carry_scr):
    step = pl.program_id(0)
    lhs = jnp.concatenate([oa_ref[h] for h in range(N_HEADS)] + [ob_ref[h] for h in range(N_HEADS)], axis=1)
    x = x_ref[...] + jnp.dot(lhs, wo_ref[...], preferred_element_type=F32)
    h_ref[...] = x
    tm = x.shape[0]
    ms = jnp.mean(x * x, axis=-1, keepdims=True)
    u = x * lax.rsqrt(ms + EPS) * g_ref[...]
    u_ref[...] = _pack_bf16_pairs(u)
    uhi = u.astype(BF16)
    ulo = (u - uhi.astype(F32)).astype(BF16)
    both = jnp.dot(uhi, whl_ref[...], preferred_element_type=F32)
    logits = (both[:, :LANES] + both[:, LANES:]
              + jnp.dot(ulo, whl_ref[:, :LANES], preferred_element_type=F32)) + b_ref[...]
    lane_i = lax.broadcasted_iota(jnp.int32, (tm, LANES), 1)
    lane = lane_i.astype(F32)
    big = float(4 * LANES)
    neg = -jnp.inf
    c = jnp.where(jnp.logical_and(lane_i >= N_EXPERTS, lane_i < N_EXPERTS + N_GROUPS), logits, neg)
    cmax = jnp.max(c, axis=1, keepdims=True)
    g_sel = jnp.min(jnp.where(c == cmax, lane, big), axis=1, keepdims=True) - N_EXPERTS
    g_gate = 1.0 / jnp.sum(jnp.exp(c - cmax), axis=1, keepdims=True)
    lo = g_sel * EXPERTS_PER_GROUP
    in_group = jnp.logical_and(lane >= lo, lane < lo + EXPERTS_PER_GROUP)
    f = jnp.where(in_group, logits, neg)
    t1 = jnp.max(f, axis=1, keepdims=True)
    i1 = jnp.min(jnp.where(f == t1, lane, big), axis=1, keepdims=True)
    f2 = jnp.where(lane == i1, neg, f)
    t2 = jnp.max(f2, axis=1, keepdims=True)
    i2 = jnp.min(jnp.where(f2 == t2, lane, big), axis=1, keepdims=True)
    d = jnp.exp(t2 - t1)
    w_first = g_gate / (1.0 + d)
    w0_ref[...] = jnp.broadcast_to(w_first, (tm, LANES))
    w1_ref[...] = jnp.broadcast_to(w_first * d, (tm, LANES))

    @pl.when(step == 0)
    def _():
        carry_scr[...] = jnp.zeros_like(carry_scr)

    oh0 = (lane == i1).astype(F32)
    oh1 = (lane == i2).astype(F32)
    oh = oh0 + oh1
    r = lax.broadcasted_iota(jnp.int32, (tm, tm), 0)
    cc = lax.broadcasted_iota(jnp.int32, (tm, tm), 1)
    before = (cc < r).astype(BF16)
    seen = jnp.dot(before, oh.astype(BF16), preferred_element_type=F32) + carry_scr[0:1, :]
    rank0 = jnp.sum(oh0 * seen, axis=1, keepdims=True)
    rank1 = jnp.sum(oh1 * seen, axis=1, keepdims=True)
    total = seen[tm - 1:tm, :] + oh[tm - 1:tm, :]
    carry_scr[...] = jnp.broadcast_to(total, carry_scr.shape)
    cnt_ref[...] = jnp.broadcast_to(total, cnt_ref.shape).astype(jnp.int32)
    vals = jnp.where(lane_i == 0, i1, jnp.where(lane_i == 1, i2, jnp.where(lane_i == 2, rank0,
                     jnp.where(lane_i == 3, rank1, 0.0))))
    route_ref[...] = vals.astype(jnp.int32)


def _outproj_router(oa, ob, wo_bf, x2d, norm_g, whl, b_pad):
    rows = x2d.shape[0]
    tm = ROUTE_TM
    return pl.pallas_call(
        _router_kernel,
        grid=(rows // tm,),
        in_specs=[
            pl.BlockSpec((N_HEADS, tm, HEAD_DIM), lambda i: (0, i, 0)),
            pl.BlockSpec((N_HEADS, tm, HEAD_DIM), lambda i: (0, i, 0)),
            pl.BlockSpec((2 * D_GROUP, D_MODEL), lambda i: (0, 0)),
            pl.BlockSpec((tm, D_MODEL), lambda i: (i, 0)),
            pl.BlockSpec((1, D_MODEL), lambda i: (0, 0)),
            pl.BlockSpec((D_MODEL, 2 * LANES), lambda i: (0, 0)),
            pl.BlockSpec((1, LANES), lambda i: (0, 0)),
        ],
        out_specs=[
            pl.BlockSpec((tm, D_MODEL), lambda i: (i, 0)),
            pl.BlockSpec((tm, D_PACKED), lambda i: (i, 0)),
            pl.BlockSpec((tm, LANES), lambda i: (i, 0)),
            pl.BlockSpec((tm, LANES), lambda i: (i, 0)),
            pl.BlockSpec((tm, LANES), lambda i: (i, 0)),
            pl.BlockSpec((8, LANES), lambda i: (0, 0)),
        ],
        out_shape=[
            jax.ShapeDtypeStruct((rows, D_MODEL), F32),
            jax.ShapeDtypeStruct((rows, D_PACKED), jnp.uint32),
            jax.ShapeDtypeStruct((rows, LANES), jnp.int32),
            jax.ShapeDtypeStruct((rows, LANES), F32),
            jax.ShapeDtypeStruct((rows, LANES), F32),
            jax.ShapeDtypeStruct((8, LANES), jnp.int32),
        ],
        scratch_shapes=[pltpu.VMEM((8, LANES), F32)],
        compiler_params=_cparams(("arbitrary",)),
        name="outproj_router",
    )(oa, ob, wo_bf, x2d, norm_g, whl, b_pad)


MOVE_TM = 256


def _scatter_kernel(pos_ref, u_ref, xs_ref, zero_scr, sem, pad_sem, *, n_rows):
    base = pl.program_id(0) * MOVE_TM

    @pl.when(pl.program_id(0) == 0)
    def _():
        zero_scr[...] = jnp.zeros_like(zero_scr)
        pad = pltpu.make_async_copy(zero_scr, xs_ref.at[pl.ds(n_rows, FFN_WIN)], pad_sem)
        pad.start()
        pad.wait()

    for t in range(MOVE_TM):
        for k in range(TOP_K):
            dst = pos_ref[(base + t) * TOP_K + k]
            pltpu.make_async_copy(u_ref.at[pl.ds(t, 1)], xs_ref.at[pl.ds(dst, 1)], sem).start(priority=k)
    for _ in range(TOP_K):
        pltpu.make_async_copy(u_ref, xs_ref.at[pl.ds(0, MOVE_TM)], sem).wait()


def _scatter_rows(pos_flat, u2):
    n_tok = u2.shape[0]
    n_rows = n_tok * TOP_K
    return pl.pallas_call(
        functools.partial(_scatter_kernel, n_rows=n_rows),
        grid_spec=pltpu.PrefetchScalarGridSpec(
            num_scalar_prefetch=1,
            grid=(n_tok // MOVE_TM,),
            in_specs=[pl.BlockSpec((MOVE_TM, D_PACKED), lambda i, pos: (i, 0))],
            out_specs=pl.BlockSpec(memory_space=pl.ANY),
            scratch_shapes=[pltpu.VMEM((FFN_WIN, D_PACKED), u2.dtype), pltpu.SemaphoreType.DMA(()),
                            pltpu.SemaphoreType.DMA(())],
        ),
        out_shape=jax.ShapeDtypeStruct((n_rows + FFN_WIN, D_PACKED), u2.dtype),
        compiler_params=_cparams(("arbitrary",)),
        name="scatter_rows",
    )(pos_flat, u2)


FFN_TM = 128
SUBLANES = 8
FFN_WIN = FFN_TM + SUBLANES
WEIGHT_SLOTS = 3


def _ffn_kernel(row_ref, tile_ref, exp_ref, valid_ref, new_ref, slot_ref, next_ref, next2_ref,
                x_ref, w1_hbm, w3_hbm, w2_hbm, o_ref,
                w1_buf, w3_buf, w2_buf, w1_bf, w3_bf, w2_bf, sem):
    it = pl.program_id(0)
    slot = slot_ref[it]

    def weight_copies(expert, s):
        half = D_EXPERT // 2
        return ((pltpu.make_async_copy(w1_hbm.at[expert], w1_buf.at[s], sem.at[s, 0]), 0),
                (pltpu.make_async_copy(w3_hbm.at[expert], w3_buf.at[s], sem.at[s, 1]), 1),
                (pltpu.make_async_copy(w2_hbm.at[expert, pl.ds(0, half)], w2_buf.at[s, pl.ds(0, half)],
                                       sem.at[s, 2]), 0),
                (pltpu.make_async_copy(w2_hbm.at[expert, pl.ds(half, half)], w2_buf.at[s, pl.ds(half, half)],
                                       sem.at[s, 3]), 1))

    def next_slot(s, ahead):
        s = s + ahead
        return jnp.where(s >= WEIGHT_SLOTS, s - WEIGHT_SLOTS, s)

    @pl.when(it == 0)
    def _():
        for cp, queue in weight_copies(exp_ref[0], slot):
            cp.start(priority=queue)

        @pl.when(next_ref[0] >= 0)
        def _():
            for cp, queue in weight_copies(next_ref[0], next_slot(slot, 1)):
                cp.start(priority=queue)

    @pl.when(new_ref[it] == 1)
    def _():
        @pl.when(next2_ref[it] >= 0)
        def _():
            for cp, queue in weight_copies(next2_ref[it], next_slot(slot, 2)):
                cp.start(priority=queue)

        for cp, _ in weight_copies(exp_ref[it], slot):
            cp.wait()

        w1_bf[...] = w1_buf[slot].astype(BF16)
        w3_bf[...] = w3_buf[slot].astype(BF16)
        w2_bf[...] = w2_buf[slot].astype(BF16)

    @pl.when(valid_ref[it] == 1)
    def _():
        x = _unpack_bf16_pairs(x_ref[...]).astype(BF16)
        a = jnp.dot(x, w1_bf[...], preferred_element_type=F32)
        b = jnp.dot(x, w3_bf[...], preferred_element_type=F32)
        mid = (a / (1.0 + jnp.exp(-a)) * b).astype(BF16)
        o_ref[...] = _pack_bf16_pairs(jnp.dot(mid, w2_bf[...], preferred_element_type=F32))

    @pl.when(valid_ref[it] == 0)
    def _():
        o_ref[...] = jnp.zeros_like(o_ref)


def _grouped_ffn(items, xs, w1, w3, w2):
    n_items = items[0].shape[0]

    def row_map(i, row_group, *_):
        return (row_group[i] * SUBLANES, 0)

    def tile_map(i, row, tile, *_):
        return (tile[i], 0)

    return pl.pallas_call(
        _ffn_kernel,
        grid_spec=pltpu.PrefetchScalarGridSpec(
            num_scalar_prefetch=len(items),
            grid=(n_items,),
            in_specs=[
                pl.BlockSpec((pl.Element(FFN_WIN), pl.Element(D_PACKED)), row_map),
                pl.BlockSpec(memory_space=pl.ANY),
                pl.BlockSpec(memory_space=pl.ANY),
                pl.BlockSpec(memory_space=pl.ANY),
            ],
            out_specs=pl.BlockSpec((FFN_WIN, D_PACKED), tile_map),
            scratch_shapes=[
                pltpu.VMEM((WEIGHT_SLOTS, D_MODEL, D_EXPERT), F32),
                pltpu.VMEM((WEIGHT_SLOTS, D_MODEL, D_EXPERT), F32),
                pltpu.VMEM((WEIGHT_SLOTS, D_EXPERT, D_MODEL), F32),
                pltpu.VMEM((D_MODEL, D_EXPERT), BF16),
                pltpu.VMEM((D_MODEL, D_EXPERT), BF16),
                pltpu.VMEM((D_EXPERT, D_MODEL), BF16),
                pltpu.SemaphoreType.DMA((WEIGHT_SLOTS, 4)),
            ],
        ),
        out_shape=jax.ShapeDtypeStruct((n_items * FFN_WIN, D_PACKED), jnp.uint32),
        compiler_params=_cparams(("arbitrary",)),
        name="grouped_ffn",
    )(*items, xs, w1, w3, w2)


def _combine_kernel(pos_ref, h_ref, w0_ref, w1_ref, ys_ref, o_ref, ybuf, sem):
    step = pl.program_id(0)
    n_steps = pl.num_programs(0)

    def fetch(s, slot):
        for t in range(MOVE_TM):
            for k in range(TOP_K):
                src = pos_ref[(s * MOVE_TM + t) * TOP_K + k]
                pltpu.make_async_copy(ys_ref.at[pl.ds(src, 1)], ybuf.at[slot, k, pl.ds(t, 1)],
                                      sem.at[slot]).start(priority=k)

    @pl.when(step == 0)
    def _():
        fetch(0, 0)

    slot = step % 2

    @pl.when(step + 1 < n_steps)
    def _():
        fetch(step + 1, 1 - slot)

    for k in range(TOP_K):
        pltpu.make_async_copy(ys_ref.at[pl.ds(0, MOVE_TM)], ybuf.at[slot, k], sem.at[slot]).wait()
    reps = D_MODEL // LANES
    w0 = jnp.concatenate([w0_ref[...]] * reps, axis=1)
    w1 = jnp.concatenate([w1_ref[...]] * reps, axis=1)
    o_ref[...] = (h_ref[...] + w0 * _unpack_bf16_pairs(ybuf[slot, 0])
                  + w1 * _unpack_bf16_pairs(ybuf[slot, 1]))


def _combine(pos_flat, h1, w0b, w1b, ys):
    rows = h1.shape[0]
    tm = MOVE_TM
    return pl.pallas_call(
        _combine_kernel,
        grid_spec=pltpu.PrefetchScalarGridSpec(
            num_scalar_prefetch=1,
            grid=(rows // tm,),
            in_specs=[
                pl.BlockSpec((tm, D_MODEL), lambda i, pos: (i, 0)),
                pl.BlockSpec((tm, LANES), lambda i, pos: (i, 0)),
                pl.BlockSpec((tm, LANES), lambda i, pos: (i, 0)),
                pl.BlockSpec(memory_space=pl.ANY),
            ],
            out_specs=pl.BlockSpec((tm, D_MODEL), lambda i, pos: (i, 0)),
            scratch_shapes=[pltpu.VMEM((2, TOP_K, tm, D_PACKED), jnp.uint32), pltpu.SemaphoreType.DMA((2,))],
        ),
        out_shape=jax.ShapeDtypeStruct((rows, D_MODEL), F32),
        compiler_params=_cparams(("arbitrary",)),
        name="combine",
    )(pos_flat, h1, w0b, w1b, ys)


def _work_items(counts, n_assign):
    counts = counts.astype(jnp.int32)
    starts = jnp.cumsum(counts) - counts
    chunks = (counts + FFN_TM - 1) // FFN_TM
    chunk_end = jnp.cumsum(chunks)
    chunk_start = chunk_end - chunks
    n_items = n_assign // FFN_TM + N_EXPERTS
    n_real = chunk_end[-1]
    idx = jnp.minimum(jnp.arange(n_items, dtype=jnp.int32), n_real - 1)
    item_valid = (jnp.arange(n_items, dtype=jnp.int32) < n_real).astype(jnp.int32)
    item_exp = jnp.sum((chunk_end[None, :] <= idx[:, None]).astype(jnp.int32), axis=1).astype(jnp.int32)
    onehot = item_exp[:, None] == jnp.arange(N_EXPERTS, dtype=jnp.int32)[None, :]
    exp_start = jnp.sum(jnp.where(onehot, starts[None, :], 0), axis=1)
    exp_chunk0 = jnp.sum(jnp.where(onehot, chunk_start[None, :], 0), axis=1)
    item_row = ((exp_start + (idx - exp_chunk0) * FFN_TM) // SUBLANES).astype(jnp.int32)
    item_new = jnp.concatenate([jnp.ones((1,), jnp.int32), (item_exp[1:] != item_exp[:-1]).astype(jnp.int32)])
    ordinal = jnp.cumsum(item_new) - 1
    run_exp = jnp.full((n_items + 2,), -1, jnp.int32).at[ordinal].set(item_exp)
    item_next = run_exp[ordinal + 1]
    item_next2 = run_exp[ordinal + 2]
    item_slot = (ordinal % WEIGHT_SLOTS).astype(jnp.int32)
    item_tile = jnp.arange(n_items, dtype=jnp.int32)
    items = (item_row, item_tile, item_exp, item_valid, item_new, item_slot, item_next, item_next2)
    return starts.astype(jnp.int32), chunk_start.astype(jnp.int32), items


def kernel(x, meta_tokens, norm1_g, w_in, b_f, q_gain, k_gain, fox_out_gain, sb_out_gain, w_out, norm2_g,
           w_coarse, b_coarse, w_fine, b_fine, w1, w3, w2):
    assert norm1_g.shape[0] == 1, "single-layer block"
    n_batch, seq, _ = x.shape
    n_tok = n_batch * seq
    n_assign = n_tok * TOP_K
    scale = HEAD_DIM ** -0.5
    x2d = x.reshape(n_tok, D_MODEL)

    w_in_t = w_in[0].T
    w_out_bf = _cast_bf16(w_out[0], 2 * D_GROUP)
    wft_bf = jnp.pad(w_in_t[D_PROJ:], ((0, LANES - N_HEADS), (0, 0))).astype(BF16)
    bf_pad = jnp.pad(b_f[0], (0, LANES - N_HEADS)).reshape(1, LANES)
    ones = jnp.ones((HEAD_DIM,), F32)
    zeros = jnp.zeros((HEAD_DIM,), F32)
    gains = jnp.stack([jnp.stack([q_gain[0] * scale, ones]), jnp.stack([k_gain[0], ones]),
                       jnp.stack([ones, zeros]), jnp.stack([ones, zeros]), jnp.stack([ones * scale, zeros]),
                       jnp.stack([ones, zeros]), jnp.stack([ones, zeros])])
    gains = jnp.repeat(gains, N_HEADS, axis=0)
    meta_pad = jnp.pad(meta_tokens.astype(F32), ((0, BLK - N_META), (0, 0)))
    n1 = norm1_g[0].reshape(1, D_MODEL)

    proj_m, lf_m, w_in_bf = _inproj(meta_pad, n1, w_in_t, wft_bf, bf_pad, gains, tm=BLK)
    proj, lf = _inproj(x2d, n1, w_in_bf, wft_bf, bf_pad, gains, tm=1024, tn=INPROJ_TN)
    qx, kx, kxm = _cumgate(lf_m, lf, n_batch, seq)
    qk_bound = 1.02 * HEAD_DIM * scale * jnp.max(jnp.abs(q_gain[0])) * jnp.max(jnp.abs(k_gain[0]))
    bound = (2.0 * qk_bound).reshape(1).astype(F32)
    oa = _fox_attention(bound, proj, proj_m, qx, kx, kxm, fox_out_gain[0].reshape(N_HEADS, 1, HEAD_DIM),
                        n_batch, seq)
    ob = _sb_attention(proj, proj_m, sb_out_gain[0].reshape(N_HEADS, 1, HEAD_DIM), n_batch, seq)

    wr = jnp.pad(jnp.concatenate([w_fine[0], w_coarse[0]], axis=1),
                 ((0, 0), (0, LANES - N_GROUPS - N_EXPERTS)))
    wr_hi = wr.astype(BF16)
    wr_hl = jnp.concatenate([wr_hi, (wr - wr_hi.astype(F32)).astype(BF16)], axis=1)
    br = jnp.pad(jnp.concatenate([b_fine[0], b_coarse[0]]), (0, LANES - N_GROUPS - N_EXPERTS)).reshape(1, LANES)
    h1, u2, route, w0b, w1b, cnt = _outproj_router(oa, ob, w_out_bf, x2d, norm2_g[0].reshape(1, D_MODEL),
                                                   wr_hl, br)
    starts, chunk_starts, items = _work_items(cnt[0, :N_EXPERTS], n_assign)
    eid = route[:, 0:TOP_K]
    rank = route[:, TOP_K:2 * TOP_K]
    is_exp = eid[..., None] == jnp.arange(N_EXPERTS, dtype=jnp.int32)
    start_of = jnp.sum(jnp.where(is_exp, starts, 0), axis=-1)
    chunk = rank // FFN_TM
    item_of = jnp.sum(jnp.where(is_exp, chunk_starts, 0), axis=-1) + chunk
    pos_in = start_of + rank
    window_row = (start_of + chunk * FFN_TM) // SUBLANES * SUBLANES
    pos_out = (item_of * FFN_WIN + (pos_in - window_row)).reshape(-1)
    pos_in = pos_in.reshape(-1)
    xs = _scatter_rows(pos_in, u2)
    ys = _grouped_ffn(items, xs, w1[0], w3[0], w2[0])
    out = _combine(pos_out, h1, w0b, w1b, ys)
    return out.reshape(n_batch, seq, D_MODEL)
```

```python
import functools

import jax
import jax.numpy as jnp
from jax import lax
from jax.experimental import pallas as pl
from jax.experimental.pallas import tpu as pltpu

F32 = jnp.float32
BF16 = jnp.bfloat16

D_MODEL = 2048
N_META = 16
HEAD_DIM = 128
N_HEADS = 8
D_GROUP = N_HEADS * HEAD_DIM
N_PROJ_GROUPS = 7
D_PROJ = N_PROJ_GROUPS * D_GROUP
N_GROUPS = 8
EXPERTS_PER_GROUP = 8
N_EXPERTS = 64
TOP_K = 2
D_EXPERT = 512
EPS = 1e-6
LANES = 128
D_PACKED = D_MODEL // 2
MXU_HEADS = 2
INPROJ_TN = 14 * HEAD_DIM
BLK = 128
FOX_FIRST_BLOCKS = 3
SKIP_LOG = 88.0
VMEM_LIMIT = 56 * 1024 * 1024


def _cparams(sem, vmem=VMEM_LIMIT):
    return pltpu.CompilerParams(dimension_semantics=sem, vmem_limit_bytes=vmem)


def _log_sigmoid(x):
    return jnp.minimum(x, 0.0) - jnp.log(1.0 + jnp.exp(-jnp.abs(x)))


def _split3(x):
    hi = x.astype(BF16)
    r1 = x - hi.astype(F32)
    mid = r1.astype(BF16)
    lo = (r1 - mid.astype(F32)).astype(BF16)
    return hi, mid, lo


def _dot_nt(a, b):
    return lax.dot_general(a, b, (((1,), (1,)), ((), ())), preferred_element_type=F32)


def _pack_bf16_pairs(x):
    half = x.shape[1] // 2
    xr = x.astype(BF16).astype(F32)
    hi = lax.bitcast_convert_type(xr[:, :half], jnp.uint32)
    lo = lax.bitcast_convert_type(xr[:, half:], jnp.uint32)
    return hi | lax.shift_right_logical(lo, jnp.uint32(16))


def _unpack_bf16_pairs(p):
    hi = lax.bitcast_convert_type(p & jnp.uint32(0xFFFF0000), F32)
    lo = lax.bitcast_convert_type(lax.shift_left(p, jnp.uint32(16)), F32)
    return jnp.concatenate([hi, lo], axis=1)


def _cast_kernel(x_ref, o_ref):
    o_ref[...] = x_ref[...].astype(o_ref.dtype)


def _cast_bf16(w, n_rows, tr=512):
    n_cols = w.shape[1]
    return pl.pallas_call(
        _cast_kernel,
        grid=(n_rows // tr,),
        in_specs=[pl.BlockSpec((tr, n_cols), lambda i: (i, 0))],
        out_specs=pl.BlockSpec((tr, n_cols), lambda i: (i, 0)),
        out_shape=jax.ShapeDtypeStruct((n_rows, n_cols), BF16),
        compiler_params=_cparams(("parallel",)),
        name="cast_bf16",
    )(w)


def _inproj_kernel(x_ref, g_ref, wt_ref, wft_ref, bf_ref, gain_ref, o_ref, lf_ref, *rest, heads_per_tile):
    wbf_ref = rest[0] if len(rest) == 2 else None
    u_ref = rest[-1]
    j = pl.program_id(1)

    @pl.when(j == 0)
    def _():
        x = x_ref[...]
        ms = jnp.mean(x * x, axis=-1, keepdims=True)
        u = (x * lax.rsqrt(ms + EPS) * g_ref[...]).astype(BF16)
        u_ref[...] = u
        f = _dot_nt(u, wft_ref[...]) + bf_ref[...]
        lf_ref[...] = _log_sigmoid(f)

    u = u_ref[...]
    for s in range(heads_per_tile // MXU_HEADS):
        gain = gain_ref[s * MXU_HEADS, 0:1, :]
        normed = gain_ref[s * MXU_HEADS, 1:2, :] > 0.5
        cols = pl.ds(s * MXU_HEADS * HEAD_DIM, MXU_HEADS * HEAD_DIM)
        w = wt_ref[cols, :]
        if wbf_ref is not None:
            w = w.astype(BF16)
            wbf_ref[cols, :] = w
        acc = _dot_nt(u, w)
        for hh in range(MXU_HEADS):
            y = acc[:, hh * HEAD_DIM:(hh + 1) * HEAD_DIM]
            ms = jnp.mean(y * y, axis=-1, keepdims=True)
            scale = jnp.where(normed, lax.rsqrt(ms + EPS), 1.0)
            o_ref[s * MXU_HEADS + hh] = (y * scale * gain).astype(BF16)


def _inproj(x2d, norm_g, wt, wft_bf, bf_pad, gains, tm, tn=D_GROUP):
    rows = x2d.shape[0]
    hpt = tn // HEAD_DIM
    kern = functools.partial(_inproj_kernel, heads_per_tile=hpt)
    out_specs = [
        pl.BlockSpec((hpt, tm, HEAD_DIM), lambda i, j: (j, i, 0)),
        pl.BlockSpec((tm, LANES), lambda i, j: (i, 0)),
    ]
    out_shape = [
        jax.ShapeDtypeStruct((D_PROJ // HEAD_DIM, rows, HEAD_DIM), BF16),
        jax.ShapeDtypeStruct((rows, LANES), F32),
    ]
    if wt.dtype != BF16:
        assert rows == tm, "the bf16 weight copy is written once per column tile"
        out_specs.append(pl.BlockSpec((tn, D_MODEL), lambda i, j: (j, 0)))
        out_shape.append(jax.ShapeDtypeStruct((D_PROJ, D_MODEL), BF16))
    return pl.pallas_call(
        kern,
        grid=(rows // tm, D_PROJ // tn),
        in_specs=[
            pl.BlockSpec((tm, D_MODEL), lambda i, j: (i, 0)),
            pl.BlockSpec((1, D_MODEL), lambda i, j: (0, 0)),
            pl.BlockSpec((tn, D_MODEL), lambda i, j: (j, 0)),
            pl.BlockSpec((LANES, D_MODEL), lambda i, j: (0, 0)),
            pl.BlockSpec((1, LANES), lambda i, j: (0, 0)),
            pl.BlockSpec((hpt, 2, HEAD_DIM), lambda i, j: (j, 0, 0)),
        ],
        out_specs=out_specs,
        out_shape=out_shape,
        scratch_shapes=[pltpu.VMEM((tm, D_MODEL), BF16)],
        compiler_params=_cparams(("parallel", "arbitrary")),
        name="inproj",
    )(x2d, norm_g, wt, wft_bf, bf_pad, gains)


def _cumgate_kernel(lfm_ref, lf_ref, qx_ref, kx_ref, kxm_ref, *, n_blk):
    row = lax.broadcasted_iota(jnp.int32, (BLK, BLK), 0)
    col = lax.broadcasted_iota(jnp.int32, (BLK, BLK), 1)
    tri = (col <= row).astype(BF16)

    def prefix(x):
        hi, mid, lo = _split3(x)
        return (jnp.dot(tri, hi, preferred_element_type=F32)
                + jnp.dot(tri, mid, preferred_element_type=F32)
                + jnp.dot(tri, lo, preferred_element_type=F32))

    one = jnp.ones((BLK, LANES), F32)
    zero = jnp.zeros((BLK, LANES), F32)

    def ext(cum, h):
        c = jnp.broadcast_to(cum[:, h:h + 1], (BLK, LANES))
        hi, mid, lo = (t.astype(F32) for t in _split3(c))
        qx = jnp.where(col == 0, hi, jnp.where(col == 1, mid, jnp.where(col == 2, lo,
                       jnp.where(col < 6, one, zero))))
        kx = jnp.where(col < 3, one, jnp.where(col == 3, -hi, jnp.where(col == 4, -mid,
                       jnp.where(col == 5, -lo, zero))))
        return qx.astype(BF16), kx.astype(BF16)

    lfm = jnp.where(row < N_META, lfm_ref[...], 0.0)
    cum_m = prefix(lfm)
    for h in range(N_HEADS):
        _, kx = ext(cum_m, h)
        kxm_ref[h] = kx
    carry = cum_m[BLK - 1:BLK, :]
    for b in range(n_blk):
        cum = prefix(lf_ref[b * BLK:(b + 1) * BLK, :]) + carry
        carry = cum[BLK - 1:BLK, :]
        for h in range(N_HEADS):
            qx, kx = ext(cum, h)
            qx_ref[h, b * BLK:(b + 1) * BLK, :] = qx
            kx_ref[h, b * BLK:(b + 1) * BLK, :] = kx


def _cumgate(lf_meta, lf_real, n_batch, seq):
    kern = functools.partial(_cumgate_kernel, n_blk=seq // BLK)
    return pl.pallas_call(
        kern,
        grid=(n_batch,),
        in_specs=[
            pl.BlockSpec((BLK, LANES), lambda b: (0, 0)),
            pl.BlockSpec((seq, LANES), lambda b: (b, 0)),
        ],
        out_specs=[
            pl.BlockSpec((N_HEADS, seq, LANES), lambda b: (0, b, 0)),
            pl.BlockSpec((N_HEADS, seq, LANES), lambda b: (0, b, 0)),
            pl.BlockSpec((N_HEADS, BLK, LANES), lambda b: (0, 0, 0)),
        ],
        out_shape=[
            jax.ShapeDtypeStruct((N_HEADS, n_batch * seq, LANES), BF16),
            jax.ShapeDtypeStruct((N_HEADS, n_batch * seq, LANES), BF16),
            jax.ShapeDtypeStruct((N_HEADS, BLK, LANES), BF16),
        ],
        compiler_params=_cparams(("arbitrary",)),
        name="cumgate",
    )(lf_meta, lf_real)


def _bdot_nt(a, b):
    return lax.dot_general(a, b, (((2,), (2,)), ((0,), (0,))), preferred_element_type=F32)


def _bdot_nn(a, b):
    return lax.dot_general(a, b, (((2,), (1,)), ((0,), (0,))), preferred_element_type=F32)


def _fox_kernel(bound_ref, q_ref, qx_ref, k_ref, kx_ref, v_ref, g_ref, km_ref, kxm_ref, vm_ref, gain_ref,
                o_ref, m_scr, l_scr, acc_scr):
    i = pl.program_id(1)

    def sweep(kb, kxb, vb, mask, first):
        qa = jnp.concatenate([q_ref[...], qx_ref[...]], axis=2)
        ka = jnp.concatenate([kb, kxb], axis=2)
        s = _bdot_nt(qa, ka)
        if mask is not None:
            s = jnp.where(mask[None], s, -jnp.inf)
        m_cur = jnp.max(s, axis=2, keepdims=True)
        v1 = jnp.concatenate([vb, jnp.ones_like(vb)], axis=2)
        if first:
            m_col = m_cur
            pv = _bdot_nn(jnp.exp(s - m_col).astype(BF16), v1)
            l_scr[...] = pv[:, :, BLK:]
            acc_scr[...] = pv[:, :, :BLK]
            m_scr[...] = jnp.broadcast_to(m_col, m_scr.shape)
        else:
            m_prev = m_scr[...]
            m_new = jnp.maximum(m_prev, m_cur)
            m_col = m_new[:, :, 0:1]
            alpha = jnp.exp(m_prev - m_new)
            pv = _bdot_nn(jnp.exp(s - m_new).astype(BF16), v1)
            l_scr[...] = alpha * l_scr[...] + pv[:, :, BLK:]
            acc_scr[...] = alpha * acc_scr[...] + pv[:, :, :BLK]
            m_scr[...] = m_new
        return jnp.max(s[:, :, 0:1] - m_col)

    def real_block(start, width):
        start = pl.multiple_of(start, BLK)
        return (k_ref[:, pl.ds(start, width), :], kx_ref[:, pl.ds(start, width), :],
                v_ref[:, pl.ds(start, width), :])

    bound = bound_ref[0]
    first_start = jnp.maximum(i - (FOX_FIRST_BLOCKS - 1), 0) * BLK
    row2 = lax.broadcasted_iota(jnp.int32, (BLK, FOX_FIRST_BLOCKS * BLK), 0)
    col2 = lax.broadcasted_iota(jnp.int32, (BLK, FOX_FIRST_BLOCKS * BLK), 1)
    causal = col2 + (first_start - i * BLK) <= row2
    gap0 = sweep(*real_block(first_start, FOX_FIRST_BLOCKS * BLK), causal, True)

    def cond(c):
        j, done = c
        return jnp.logical_and(j >= 0, done == 0)

    def body(c):
        j, _ = c
        gap = sweep(*real_block(j * BLK, BLK), None, False)
        return j - 1, (gap + bound < -SKIP_LOG).astype(jnp.int32)

    _, done = lax.while_loop(cond, body, (i - FOX_FIRST_BLOCKS, (gap0 + bound < -SKIP_LOG).astype(jnp.int32)))

    @pl.when(done == 0)
    def _():
        col = lax.broadcasted_iota(jnp.int32, (BLK, BLK), 1)
        sweep(km_ref[...], kxm_ref[...], vm_ref[...], col < N_META, False)

    o = acc_scr[...] / l_scr[...]
    ms = jnp.mean(o * o, axis=-1, keepdims=True)
    gate = 1.0 / (1.0 + jnp.exp(-g_ref[...].astype(F32)))
    o_ref[...] = (o * lax.rsqrt(ms + EPS) * gain_ref[...] * gate).astype(BF16)


def _sb_kernel(q_ref, k_ref, v_ref, km_ref, vm_ref, gain_ref, o_ref, carry_scr, acc_scr):
    i = pl.program_id(1)

    def suffix_operator(width):
        r = lax.broadcasted_iota(jnp.int32, (width, width + BLK), 0)
        c = lax.broadcasted_iota(jnp.int32, (width, width + BLK), 1)
        return jnp.logical_or(c >= width, r > c).astype(BF16)

    def sweep(kb, vb, mask, first):
        width = kb.shape[1]
        suffix = suffix_operator(width)
        z = _bdot_nt(q_ref[...], kb)
        sp = jnp.maximum(z, 0.0) + jnp.log(1.0 + jnp.exp(-jnp.abs(z)))
        lk = -sp
        if mask is not None:
            lk = jnp.where(mask[None], lk, 0.0)
        hi = lk.astype(BF16)
        lo = (lk - hi.astype(F32)).astype(BF16)
        t = (jnp.dot(hi.reshape(N_HEADS * BLK, width), suffix, preferred_element_type=F32)
             + jnp.dot(lo.reshape(N_HEADS * BLK, width), suffix, preferred_element_type=F32))
        t = t.reshape(N_HEADS, BLK, width + BLK)
        later = t[:, :, :width]
        rowsum = t[:, :, width:]
        if not first:
            later = later + carry_scr[...]
        a = jnp.exp(z - sp + later)
        if mask is not None:
            a = jnp.where(mask[None], a, 0.0)
        pv = _bdot_nn(a.astype(BF16), vb)
        if first:
            acc_scr[...] = pv
            c_new = rowsum
        else:
            acc_scr[...] = acc_scr[...] + pv
            c_new = carry_scr[...] + rowsum
        carry_scr[...] = c_new
        return jnp.max(c_new[:, :, 0:1])

    def real_block(start, width):
        start = pl.multiple_of(start, BLK)
        return k_ref[:, pl.ds(start, width), :], v_ref[:, pl.ds(start, width), :]

    first_start = jnp.maximum(i - 1, 0) * BLK
    row2 = lax.broadcasted_iota(jnp.int32, (BLK, 2 * BLK), 0)
    col2 = lax.broadcasted_iota(jnp.int32, (BLK, 2 * BLK), 1)
    strict = col2 + (first_start - i * BLK) < row2
    top0 = sweep(*real_block(first_start, 2 * BLK), strict, True)

    def cond(c):
        j, done = c
        return jnp.logical_and(j >= 0, done == 0)

    def body(c):
        j, _ = c
        top = sweep(*real_block(j * BLK, BLK), None, False)
        return j - 1, (top < -SKIP_LOG).astype(jnp.int32)

    _, done = lax.while_loop(cond, body, (i - 2, (top0 < -SKIP_LOG).astype(jnp.int32)))

    @pl.when(done == 0)
    def _():
        col = lax.broadcasted_iota(jnp.int32, (BLK, BLK), 1)
        sweep(km_ref[...], vm_ref[...], col < N_META, False)

    o = acc_scr[...]
    ms = jnp.mean(o * o, axis=-1, keepdims=True)
    o_ref[...] = (o * lax.rsqrt(ms + EPS) * gain_ref[...]).astype(BF16)


def _head_spec_q(group, nq):
    return pl.BlockSpec((N_HEADS, BLK, HEAD_DIM), lambda b, i: (group, b * nq + i, 0))


def _head_spec_kv(group, seq):
    return pl.BlockSpec((N_HEADS, seq, HEAD_DIM), lambda b, i: (group, b, 0))


def _head_spec_meta(group):
    return pl.BlockSpec((N_HEADS, BLK, HEAD_DIM), lambda b, i: (group, 0, 0))


def _fox_attention(bound, proj, proj_m, qx, kx, kxm, out_gain, n_batch, seq):
    nq = seq // BLK
    rows = n_batch * seq
    return pl.pallas_call(
        _fox_kernel,
        grid=(n_batch, nq),
        in_specs=[
            pl.BlockSpec(memory_space=pltpu.SMEM),
            _head_spec_q(0, nq),
            _head_spec_q(0, nq),
            _head_spec_kv(1, seq),
            _head_spec_kv(0, seq),
            _head_spec_kv(2, seq),
            _head_spec_q(3, nq),
            _head_spec_meta(1),
            _head_spec_meta(0),
            _head_spec_meta(2),
            pl.BlockSpec((N_HEADS, 1, HEAD_DIM), lambda b, i: (0, 0, 0)),
        ],
        out_specs=pl.BlockSpec((N_HEADS, BLK, HEAD_DIM), lambda b, i: (0, b * nq + i, 0)),
        out_shape=jax.ShapeDtypeStruct((N_HEADS, rows, HEAD_DIM), BF16),
        scratch_shapes=[pltpu.VMEM((N_HEADS, BLK, LANES), F32)] * 3,
        compiler_params=_cparams(("parallel", "arbitrary")),
        name="fox_attention",
    )(bound, proj, qx, proj, kx, proj, proj, proj_m, kxm, proj_m, out_gain)


def _sb_attention(proj, proj_m, out_gain, n_batch, seq):
    nq = seq // BLK
    rows = n_batch * seq
    return pl.pallas_call(
        _sb_kernel,
        grid=(n_batch, nq),
        in_specs=[
            _head_spec_q(4, nq),
            _head_spec_kv(5, seq),
            _head_spec_kv(6, seq),
            _head_spec_meta(5),
            _head_spec_meta(6),
            pl.BlockSpec((N_HEADS, 1, HEAD_DIM), lambda b, i: (0, 0, 0)),
        ],
        out_specs=pl.BlockSpec((N_HEADS, BLK, HEAD_DIM), lambda b, i: (0, b * nq + i, 0)),
        out_shape=jax.ShapeDtypeStruct((N_HEADS, rows, HEAD_DIM), BF16),
        scratch_shapes=[pltpu.VMEM((N_HEADS, BLK, LANES), F32)] * 2,
        compiler_params=_cparams(("parallel", "arbitrary")),
        name="sb_attention",
    )(proj, proj, proj, proj_m, proj_m, out_gain)


ROUTE_TM = 512


def _router_kernel(oa_ref, ob_ref, wo_ref, x_ref, g_ref, whl_ref, b_ref,
                   h_ref, u_ref, route_ref, w0_ref, w1_ref, cnt_ref, carry_scr):
    step = pl.program_id(0)
    lhs = jnp.concatenate([oa_ref[h] for h in range(N_HEADS)] + [ob_ref[h] for h in range(N_HEADS)], axis=1)
    x = x_ref[...] + jnp.dot(lhs, wo_ref[...], preferred_element_type=F32)
    h_ref[...] = x
    tm = x.shape[0]
    ms = jnp.mean(x * x, axis=-1, keepdims=True)
    u = x * lax.rsqrt(ms + EPS) * g_ref[...]
    u_ref[...] = _pack_bf16_pairs(u)
    uhi = u.astype(BF16)
    ulo = (u - uhi.astype(F32)).astype(BF16)
    both = jnp.dot(uhi, whl_ref[...], preferred_element_type=F32)
    logits = (both[:, :LANES] + both[:, LANES:]
              + jnp.dot(ulo, whl_ref[:, :LANES], preferred_element_type=F32)) + b_ref[...]
    lane_i = lax.broadcasted_iota(jnp.int32, (tm, LANES), 1)
    lane = lane_i.astype(F32)
    big = float(4 * LANES)
    neg = -jnp.inf
    c = jnp.where(jnp.logical_and(lane_i >= N_EXPERTS, lane_i < N_EXPERTS + N_GROUPS), logits, neg)
    cmax = jnp.max(c, axis=1, keepdims=True)
    g_sel = jnp.min(jnp.where(c == cmax, lane, big), axis=1, keepdims=True) - N_EXPERTS
    g_gate = 1.0 / jnp.sum(jnp.exp(c - cmax), axis=1, keepdims=True)
    lo = g_sel * EXPERTS_PER_GROUP
    in_group = jnp.logical_and(lane >= lo, lane < lo + EXPERTS_PER_GROUP)
    f = jnp.where(in_group, logits, neg)
    t1 = jnp.max(f, axis=1, keepdims=True)
    i1 = jnp.min(jnp.where(f == t1, lane, big), axis=1, keepdims=True)
    f2 = jnp.where(lane == i1, neg, f)
    t2 = jnp.max(f2, axis=1, keepdims=True)
    i2 = jnp.min(jnp.where(f2 == t2, lane, big), axis=1, keepdims=True)
    d = jnp.exp(t2 - t1)
    w_first = g_gate / (1.0 + d)
    w0_ref[...] = jnp.broadcast_to(w_first, (tm, LANES))
    w1_ref[...] = jnp.broadcast_to(w_first * d, (tm, LANES))

    @pl.when(step == 0)
    def _():
        carry_scr[...] = jnp.zeros_like(carry_scr)

    oh0 = (lane == i1).astype(F32)
    oh1 = (lane == i2).astype(F32)
    oh = oh0 + oh1
    r = lax.broadcasted_iota(jnp.int32, (tm, tm), 0)
    cc = lax.broadcasted_iota(jnp.int32, (tm, tm), 1)
    before = (cc < r).astype(BF16)
    seen = jnp.dot(before, oh.astype(BF16), preferred_element_type=F32) + carry_scr[0:1, :]
    rank0 = jnp.sum(oh0 * seen, axis=1, keepdims=True)
    rank1 = jnp.sum(oh1 * seen, axis=1, keepdims=True)
    total = seen[tm - 1:tm, :] + oh[tm - 1:tm, :]
    carry_scr[...] = jnp.broadcast_to(total, carry_scr.shape)
    cnt_ref[...] = jnp.broadcast_to(total, cnt_ref.shape).astype(jnp.int32)
    vals = jnp.where(lane_i == 0, i1, jnp.where(lane_i == 1, i2, jnp.where(lane_i == 2, rank0,
                     jnp.where(lane_i == 3, rank1, 0.0))))
    route_ref[...] = vals.astype(jnp.int32)


def _outproj_router(oa, ob, wo_bf, x2d, norm_g, whl, b_pad):
    rows = x2d.shape[0]
    tm = ROUTE_TM
    return pl.pallas_call(
        _router_kernel,
        grid=(rows // tm,),
        in_specs=[
            pl.BlockSpec((N_HEADS, tm, HEAD_DIM), lambda i: (0, i, 0)),
            pl.BlockSpec((N_HEADS, tm, HEAD_DIM), lambda i: (0, i, 0)),
            pl.BlockSpec((2 * D_GROUP, D_MODEL), lambda i: (0, 0)),
            pl.BlockSpec((tm, D_MODEL), lambda i: (i, 0)),
            pl.BlockSpec((1, D_MODEL), lambda i: (0, 0)),
            pl.BlockSpec((D_MODEL, 2 * LANES), lambda i: (0, 0)),
            pl.BlockSpec((1, LANES), lambda i: (0, 0)),
        ],
        out_specs=[
            pl.BlockSpec((tm, D_MODEL), lambda i: (i, 0)),
            pl.BlockSpec((tm, D_PACKED), lambda i: (i, 0)),
            pl.BlockSpec((tm, LANES), lambda i: (i, 0)),
            pl.BlockSpec((tm, LANES), lambda i: (i, 0)),
            pl.BlockSpec((tm, LANES), lambda i: (i, 0)),
            pl.BlockSpec((8, LANES), lambda i: (0, 0)),
        ],
        out_shape=[
            jax.ShapeDtypeStruct((rows, D_MODEL), F32),
            jax.ShapeDtypeStruct((rows, D_PACKED), jnp.uint32),
            jax.ShapeDtypeStruct((rows, LANES), jnp.int32),
            jax.ShapeDtypeStruct((rows, LANES), F32),
            jax.ShapeDtypeStruct((rows, LANES), F32),
            jax.ShapeDtypeStruct((8, LANES), jnp.int32),
        ],
        scratch_shapes=[pltpu.VMEM((8, LANES), F32)],
        compiler_params=_cparams(("arbitrary",)),
        name="outproj_router",
    )(oa, ob, wo_bf, x2d, norm_g, whl, b_pad)


MOVE_TM = 256


def _scatter_kernel(pos_ref, u_ref, xs_ref, zero_scr, sem, pad_sem, *, n_rows):
    base = pl.program_id(0) * MOVE_TM

    @pl.when(pl.program_id(0) == 0)
    def _():
        zero_scr[...] = jnp.zeros_like(zero_scr)
        pad = pltpu.make_async_copy(zero_scr, xs_ref.at[pl.ds(n_rows, FFN_WIN)], pad_sem)
        pad.start()
        pad.wait()

    for t in range(MOVE_TM):
        for k in range(TOP_K):
            dst = pos_ref[(base + t) * TOP_K + k]
            pltpu.make_async_copy(u_ref.at[pl.ds(t, 1)], xs_ref.at[pl.ds(dst, 1)], sem).start(priority=k)
    for _ in range(TOP_K):
        pltpu.make_async_copy(u_ref, xs_ref.at[pl.ds(0, MOVE_TM)], sem).wait()


def _scatter_rows(pos_flat, u2):
    n_tok = u2.shape[0]
    n_rows = n_tok * TOP_K
    return pl.pallas_call(
        functools.partial(_scatter_kernel, n_rows=n_rows),
        grid_spec=pltpu.PrefetchScalarGridSpec(
            num_scalar_prefetch=1,
            grid=(n_tok // MOVE_TM,),
            in_specs=[pl.BlockSpec((MOVE_TM, D_PACKED), lambda i, pos: (i, 0))],
            out_specs=pl.BlockSpec(memory_space=pl.ANY),
            scratch_shapes=[pltpu.VMEM((FFN_WIN, D_PACKED), u2.dtype), pltpu.SemaphoreType.DMA(()),
                            pltpu.SemaphoreType.DMA(())],
        ),
        out_shape=jax.ShapeDtypeStruct((n_rows + FFN_WIN, D_PACKED), u2.dtype),
        compiler_params=_cparams(("arbitrary",)),
        name="scatter_rows",
    )(pos_flat, u2)


FFN_TM = 152
SUBLANES = 8
FFN_WIN = FFN_TM + SUBLANES
WEIGHT_SLOTS = 3


def _ffn_kernel(row_ref, tile_ref, exp_ref, valid_ref, new_ref, slot_ref, next_ref, next2_ref,
                x_ref, w1_hbm, w3_hbm, w2_hbm, o_ref,
                w1_buf, w3_buf, w2_buf, w1_bf, w3_bf, w2_bf, sem):
    it = pl.program_id(0)
    slot = slot_ref[it]

    def weight_copies(expert, s):
        half = D_EXPERT // 2
        return ((pltpu.make_async_copy(w1_hbm.at[expert], w1_buf.at[s], sem.at[s, 0]), 0),
                (pltpu.make_async_copy(w3_hbm.at[expert], w3_buf.at[s], sem.at[s, 1]), 1),
                (pltpu.make_async_copy(w2_hbm.at[expert, pl.ds(0, half)], w2_buf.at[s, pl.ds(0, half)],
                                       sem.at[s, 2]), 0),
                (pltpu.make_async_copy(w2_hbm.at[expert, pl.ds(half, half)], w2_buf.at[s, pl.ds(half, half)],
                                       sem.at[s, 3]), 1))

    def next_slot(s, ahead):
        s = s + ahead
        return jnp.where(s >= WEIGHT_SLOTS, s - WEIGHT_SLOTS, s)

    @pl.when(it == 0)
    def _():
        for cp, queue in weight_copies(exp_ref[0], slot):
            cp.start(priority=queue)

        @pl.when(next_ref[0] >= 0)
        def _():
            for cp, queue in weight_copies(next_ref[0], next_slot(slot, 1)):
                cp.start(priority=queue)

    @pl.when(new_ref[it] == 1)
    def _():
        @pl.when(next2_ref[it] >= 0)
        def _():
            for cp, queue in weight_copies(next2_ref[it], next_slot(slot, 2)):
                cp.start(priority=queue)

        for cp, _ in weight_copies(exp_ref[it], slot):
            cp.wait()

        w1_bf[...] = w1_buf[slot].astype(BF16)
        w3_bf[...] = w3_buf[slot].astype(BF16)
        w2_bf[...] = w2_buf[slot].astype(BF16)

    @pl.when(valid_ref[it] == 1)
    def _():
        x = _unpack_bf16_pairs(x_ref[...]).astype(BF16)
        a = jnp.dot(x, w1_bf[...], preferred_element_type=F32)
        b = jnp.dot(x, w3_bf[...], preferred_element_type=F32)
        mid = (a / (1.0 + jnp.exp(-a)) * b).astype(BF16)
        o_ref[...] = _pack_bf16_pairs(jnp.dot(mid, w2_bf[...], preferred_element_type=F32))

    @pl.when(valid_ref[it] == 0)
    def _():
        o_ref[...] = jnp.zeros_like(o_ref)


def _grouped_ffn(items, xs, w1, w3, w2):
    n_items = items[0].shape[0]

    def row_map(i, row_group, *_):
        return (row_group[i] * SUBLANES, 0)

    def tile_map(i, row, tile, *_):
        return (tile[i], 0)

    return pl.pallas_call(
        _ffn_kernel,
        grid_spec=pltpu.PrefetchScalarGridSpec(
            num_scalar_prefetch=len(items),
            grid=(n_items,),
            in_specs=[
                pl.BlockSpec((pl.Element(FFN_WIN), pl.Element(D_PACKED)), row_map),
                pl.BlockSpec(memory_space=pl.ANY),
                pl.BlockSpec(memory_space=pl.ANY),
                pl.BlockSpec(memory_space=pl.ANY),
            ],
            out_specs=pl.BlockSpec((FFN_WIN, D_PACKED), tile_map),
            scratch_shapes=[
                pltpu.VMEM((WEIGHT_SLOTS, D_MODEL, D_EXPERT), F32),
                pltpu.VMEM((WEIGHT_SLOTS, D_MODEL, D_EXPERT), F32),
                pltpu.VMEM((WEIGHT_SLOTS, D_EXPERT, D_MODEL), F32),
                pltpu.VMEM((D_MODEL, D_EXPERT), BF16),
                pltpu.VMEM((D_MODEL, D_EXPERT), BF16),
                pltpu.VMEM((D_EXPERT, D_MODEL), BF16),
                pltpu.SemaphoreType.DMA((WEIGHT_SLOTS, 4)),
            ],
        ),
        out_shape=jax.ShapeDtypeStruct((n_items * FFN_WIN, D_PACKED), jnp.uint32),
        compiler_params=_cparams(("arbitrary",)),
        name="grouped_ffn",
    )(*items, xs, w1, w3, w2)


def _combine_kernel(pos_ref, h_ref, w0_ref, w1_ref, ys_ref, o_ref, ybuf, sem):
    step = pl.program_id(0)
    n_steps = pl.num_programs(0)

    def fetch(s, slot):
        for t in range(MOVE_TM):
            for k in range(TOP_K):
                src = pos_ref[(s * MOVE_TM + t) * TOP_K + k]
                pltpu.make_async_copy(ys_ref.at[pl.ds(src, 1)], ybuf.at[slot, k, pl.ds(t, 1)],
                                      sem.at[slot]).start(priority=k)

    @pl.when(step == 0)
    def _():
        fetch(0, 0)

    slot = step % 2

    @pl.when(step + 1 < n_steps)
    def _():
        fetch(step + 1, 1 - slot)

    for k in range(TOP_K):
        pltpu.make_async_copy(ys_ref.at[pl.ds(0, MOVE_TM)], ybuf.at[slot, k], sem.at[slot]).wait()
    reps = D_MODEL // LANES
    w0 = jnp.concatenate([w0_ref[...]] * reps, axis=1)
    w1 = jnp.concatenate([w1_ref[...]] * reps, axis=1)
    o_ref[...] = (h_ref[...] + w0 * _unpack_bf16_pairs(ybuf[slot, 0])
                  + w1 * _unpack_bf16_pairs(ybuf[slot, 1]))


def _combine(pos_flat, h1, w0b, w1b, ys):
    rows = h1.shape[0]
    tm = MOVE_TM
    return pl.pallas_call(
        _combine_kernel,
        grid_spec=pltpu.PrefetchScalarGridSpec(
            num_scalar_prefetch=1,
            grid=(rows // tm,),
            in_specs=[
                pl.BlockSpec((tm, D_MODEL), lambda i, pos: (i, 0)),
                pl.BlockSpec((tm, LANES), lambda i, pos: (i, 0)),
                pl.BlockSpec((tm, LANES), lambda i, pos: (i, 0)),
                pl.BlockSpec(memory_space=pl.ANY),
            ],
            out_specs=pl.BlockSpec((tm, D_MODEL), lambda i, pos: (i, 0)),
            scratch_shapes=[pltpu.VMEM((2, TOP_K, tm, D_PACKED), jnp.uint32), pltpu.SemaphoreType.DMA((2,))],
        ),
        out_shape=jax.ShapeDtypeStruct((rows, D_MODEL), F32),
        compiler_params=_cparams(("arbitrary",)),
        name="combine",
    )(pos_flat, h1, w0b, w1b, ys)


def _work_items(counts, n_assign):
    counts = counts.astype(jnp.int32)
    starts = jnp.cumsum(counts) - counts
    chunks = (counts + FFN_TM - 1) // FFN_TM
    chunk_end = jnp.cumsum(chunks)
    chunk_start = chunk_end - chunks
    n_items = n_assign // FFN_TM + N_EXPERTS
    n_real = chunk_end[-1]
    idx = jnp.minimum(jnp.arange(n_items, dtype=jnp.int32), n_real - 1)
    item_valid = (jnp.arange(n_items, dtype=jnp.int32) < n_real).astype(jnp.int32)
    item_exp = jnp.sum((chunk_end[None, :] <= idx[:, None]).astype(jnp.int32), axis=1).astype(jnp.int32)
    onehot = item_exp[:, None] == jnp.arange(N_EXPERTS, dtype=jnp.int32)[None, :]
    exp_start = jnp.sum(jnp.where(onehot, starts[None, :], 0), axis=1)
    exp_chunk0 = jnp.sum(jnp.where(onehot, chunk_start[None, :], 0), axis=1)
    item_row = ((exp_start + (idx - exp_chunk0) * FFN_TM) // SUBLANES).astype(jnp.int32)
    item_new = jnp.concatenate([jnp.ones((1,), jnp.int32), (item_exp[1:] != item_exp[:-1]).astype(jnp.int32)])
    ordinal = jnp.cumsum(item_new) - 1
    run_exp = jnp.full((n_items + 2,), -1, jnp.int32).at[ordinal].set(item_exp)
    item_next = run_exp[ordinal + 1]
    item_next2 = run_exp[ordinal + 2]
    item_slot = (ordinal % WEIGHT_SLOTS).astype(jnp.int32)
    item_tile = jnp.arange(n_items, dtype=jnp.int32)
    items = (item_row, item_tile, item_exp, item_valid, item_new, item_slot, item_next, item_next2)
    return starts.astype(jnp.int32), chunk_start.astype(jnp.int32), items


def kernel(x, meta_tokens, norm1_g, w_in, b_f, q_gain, k_gain, fox_out_gain, sb_out_gain, w_out, norm2_g,
           w_coarse, b_coarse, w_fine, b_fine, w1, w3, w2):
    assert norm1_g.shape[0] == 1, "single-layer block"
    n_batch, seq, _ = x.shape
    n_tok = n_batch * seq
    n_assign = n_tok * TOP_K
    scale = HEAD_DIM ** -0.5
    x2d = x.reshape(n_tok, D_MODEL)

    w_in_t = w_in[0].T
    w_out_bf = _cast_bf16(w_out[0], 2 * D_GROUP)
    wft_bf = jnp.pad(w_in_t[D_PROJ:], ((0, LANES - N_HEADS), (0, 0))).astype(BF16)
    bf_pad = jnp.pad(b_f[0], (0, LANES - N_HEADS)).reshape(1, LANES)
    ones = jnp.ones((HEAD_DIM,), F32)
    zeros = jnp.zeros((HEAD_DIM,), F32)
    gains = jnp.stack([jnp.stack([q_gain[0] * scale, ones]), jnp.stack([k_gain[0], ones]),
                       jnp.stack([ones, zeros]), jnp.stack([ones, zeros]), jnp.stack([ones * scale, zeros]),
                       jnp.stack([ones, zeros]), jnp.stack([ones, zeros])])
    gains = jnp.repeat(gains, N_HEADS, axis=0)
    meta_pad = jnp.pad(meta_tokens.astype(F32), ((0, BLK - N_META), (0, 0)))
    n1 = norm1_g[0].reshape(1, D_MODEL)

    proj_m, lf_m, w_in_bf = _inproj(meta_pad, n1, w_in_t, wft_bf, bf_pad, gains, tm=BLK)
    proj, lf = _inproj(x2d, n1, w_in_bf, wft_bf, bf_pad, gains, tm=1024, tn=INPROJ_TN)
    qx, kx, kxm = _cumgate(lf_m, lf, n_batch, seq)
    qk_bound = 1.02 * HEAD_DIM * scale * jnp.max(jnp.abs(q_gain[0])) * jnp.max(jnp.abs(k_gain[0]))
    bound = (2.0 * qk_bound).reshape(1).astype(F32)
    oa = _fox_attention(bound, proj, proj_m, qx, kx, kxm, fox_out_gain[0].reshape(N_HEADS, 1, HEAD_DIM),
                        n_batch, seq)
    ob = _sb_attention(proj, proj_m, sb_out_gain[0].reshape(N_HEADS, 1, HEAD_DIM), n_batch, seq)

    wr = jnp.pad(jnp.concatenate([w_fine[0], w_coarse[0]], axis=1),
                 ((0, 0), (0, LANES - N_GROUPS - N_EXPERTS)))
    wr_hi = wr.astype(BF16)
    wr_hl = jnp.concatenate([wr_hi, (wr - wr_hi.astype(F32)).astype(BF16)], axis=1)
    br = jnp.pad(jnp.concatenate([b_fine[0], b_coarse[0]]), (0, LANES - N_GROUPS - N_EXPERTS)).reshape(1, LANES)
    h1, u2, route, w0b, w1b, cnt = _outproj_router(oa, ob, w_out_bf, x2d, norm2_g[0].reshape(1, D_MODEL),
                                                   wr_hl, br)
    starts, chunk_starts, items = _work_items(cnt[0, :N_EXPERTS], n_assign)
    eid = route[:, 0:TOP_K]
    rank = route[:, TOP_K:2 * TOP_K]
    is_exp = eid[..., None] == jnp.arange(N_EXPERTS, dtype=jnp.int32)
    start_of = jnp.sum(jnp.where(is_exp, starts, 0), axis=-1)
    chunk = rank // FFN_TM
    item_of = jnp.sum(jnp.where(is_exp, chunk_starts, 0), axis=-1) + chunk
    pos_in = start_of + rank
    window_row = (start_of + chunk * FFN_TM) // SUBLANES * SUBLANES
    pos_out = (item_of * FFN_WIN + (pos_in - window_row)).reshape(-1)
    pos_in = pos_in.reshape(-1)
    xs = _scatter_rows(pos_in, u2)
    ys = _grouped_ffn(items, xs, w1[0], w3[0], w2[0])
    out = _combine(pos_out, h1, w0b, w1b, ys)
    return out.reshape(n_batch, seq, D_MODEL)
```

```python
import functools

import jax
import jax.numpy as jnp
from jax import lax
from jax.experimental import pallas as pl
from jax.experimental.pallas import tpu as pltpu

F32 = jnp.float32
BF16 = jnp.bfloat16

D_MODEL = 2048
N_META = 16
HEAD_DIM = 128
N_HEADS = 8
D_GROUP = N_HEADS * HEAD_DIM
N_PROJ_GROUPS = 7
D_PROJ = N_PROJ_GROUPS * D_GROUP
N_GROUPS = 8
EXPERTS_PER_GROUP = 8
N_EXPERTS = 64
TOP_K = 2
D_EXPERT = 512
EPS = 1e-6
LANES = 128
D_PACKED = D_MODEL // 2
MXU_HEADS = 2
INPROJ_TN = 14 * HEAD_DIM
BLK = 128
FOX_FIRST_BLOCKS = 3
SKIP_LOG = 88.0
VMEM_LIMIT = 56 * 1024 * 1024


def _cparams(sem, vmem=VMEM_LIMIT):
    return pltpu.CompilerParams(dimension_semantics=sem, vmem_limit_bytes=vmem)


def _log_sigmoid(x):
    return jnp.minimum(x, 0.0) - jnp.log(1.0 + jnp.exp(-jnp.abs(x)))


def _split3(x):
    hi = x.astype(BF16)
    r1 = x - hi.astype(F32)
    mid = r1.astype(BF16)
    lo = (r1 - mid.astype(F32)).astype(BF16)
    return hi, mid, lo


def _dot_nt(a, b):
    return lax.dot_general(a, b, (((1,), (1,)), ((), ())), preferred_element_type=F32)


def _pack_bf16_pairs(x):
    half = x.shape[1] // 2
    xr = x.astype(BF16).astype(F32)
    hi = lax.bitcast_convert_type(xr[:, :half], jnp.uint32)
    lo = lax.bitcast_convert_type(xr[:, half:], jnp.uint32)
    return hi | lax.shift_right_logical(lo, jnp.uint32(16))


def _unpack_bf16_pairs(p):
    hi = lax.bitcast_convert_type(p & jnp.uint32(0xFFFF0000), F32)
    lo = lax.bitcast_convert_type(lax.shift_left(p, jnp.uint32(16)), F32)
    return jnp.concatenate([hi, lo], axis=1)


def _cast_kernel(x_ref, o_ref):
    o_ref[...] = x_ref[...].astype(o_ref.dtype)


def _cast_bf16(w, n_rows, tr=512):
    n_cols = w.shape[1]
    return pl.pallas_call(
        _cast_kernel,
        grid=(n_rows // tr,),
        in_specs=[pl.BlockSpec((tr, n_cols), lambda i: (i, 0))],
        out_specs=pl.BlockSpec((tr, n_cols), lambda i: (i, 0)),
        out_shape=jax.ShapeDtypeStruct((n_rows, n_cols), BF16),
        compiler_params=_cparams(("parallel",)),
        name="cast_bf16",
    )(w)


def _inproj_kernel(x_ref, g_ref, wt_ref, wft_ref, bf_ref, gain_ref, o_ref, lf_ref, *rest, heads_per_tile):
    wbf_ref = rest[0] if len(rest) == 2 else None
    u_ref = rest[-1]
    j = pl.program_id(1)

    @pl.when(j == 0)
    def _():
        x = x_ref[...]
        ms = jnp.mean(x * x, axis=-1, keepdims=True)
        u = (x * lax.rsqrt(ms + EPS) * g_ref[...]).astype(BF16)
        u_ref[...] = u
        f = _dot_nt(u, wft_ref[...]) + bf_ref[...]
        lf_ref[...] = _log_sigmoid(f)

    u = u_ref[...]
    for s in range(heads_per_tile // MXU_HEADS):
        gain = gain_ref[s * MXU_HEADS, 0:1, :]
        normed = gain_ref[s * MXU_HEADS, 1:2, :] > 0.5
        cols = pl.ds(s * MXU_HEADS * HEAD_DIM, MXU_HEADS * HEAD_DIM)
        w = wt_ref[cols, :]
        if wbf_ref is not None:
            w = w.astype(BF16)
            wbf_ref[cols, :] = w
        acc = _dot_nt(u, w)
        for hh in range(MXU_HEADS):
            y = acc[:, hh * HEAD_DIM:(hh + 1) * HEAD_DIM]
            ms = jnp.mean(y * y, axis=-1, keepdims=True)
            scale = jnp.where(normed, lax.rsqrt(ms + EPS), 1.0)
            o_ref[s * MXU_HEADS + hh] = (y * scale * gain).astype(BF16)


def _inproj(x2d, norm_g, wt, wft_bf, bf_pad, gains, tm, tn=D_GROUP):
    rows = x2d.shape[0]
    hpt = tn // HEAD_DIM
    kern = functools.partial(_inproj_kernel, heads_per_tile=hpt)
    out_specs = [
        pl.BlockSpec((hpt, tm, HEAD_DIM), lambda i, j: (j, i, 0)),
        pl.BlockSpec((tm, LANES), lambda i, j: (i, 0)),
    ]
    out_shape = [
        jax.ShapeDtypeStruct((D_PROJ // HEAD_DIM, rows, HEAD_DIM), BF16),
        jax.ShapeDtypeStruct((rows, LANES), F32),
    ]
    if wt.dtype != BF16:
        assert rows == tm, "the bf16 weight copy is written once per column tile"
        out_specs.append(pl.BlockSpec((tn, D_MODEL), lambda i, j: (j, 0)))
        out_shape.append(jax.ShapeDtypeStruct((D_PROJ, D_MODEL), BF16))
    return pl.pallas_call(
        kern,
        grid=(rows // tm, D_PROJ // tn),
        in_specs=[
            pl.BlockSpec((tm, D_MODEL), lambda i, j: (i, 0)),
            pl.BlockSpec((1, D_MODEL), lambda i, j: (0, 0)),
            pl.BlockSpec((tn, D_MODEL), lambda i, j: (j, 0)),
            pl.BlockSpec((LANES, D_MODEL), lambda i, j: (0, 0)),
            pl.BlockSpec((1, LANES), lambda i, j: (0, 0)),
            pl.BlockSpec((hpt, 2, HEAD_DIM), lambda i, j: (j, 0, 0)),
        ],
        out_specs=out_specs,
        out_shape=out_shape,
        scratch_shapes=[pltpu.VMEM((tm, D_MODEL), BF16)],
        compiler_params=_cparams(("parallel", "arbitrary")),
        name="inproj",
    )(x2d, norm_g, wt, wft_bf, bf_pad, gains)


def _cumgate_kernel(lfm_ref, lf_ref, qx_ref, kx_ref, kxm_ref, *, n_blk):
    row = lax.broadcasted_iota(jnp.int32, (BLK, BLK), 0)
    col = lax.broadcasted_iota(jnp.int32, (BLK, BLK), 1)
    tri = (col <= row).astype(BF16)

    def prefix(x):
        hi, mid, lo = _split3(x)
        return (jnp.dot(tri, hi, preferred_element_type=F32)
                + jnp.dot(tri, mid, preferred_element_type=F32)
                + jnp.dot(tri, lo, preferred_element_type=F32))

    one = jnp.ones((BLK, LANES), F32)
    zero = jnp.zeros((BLK, LANES), F32)

    def ext(cum, h):
        c = jnp.broadcast_to(cum[:, h:h + 1], (BLK, LANES))
        hi, mid, lo = (t.astype(F32) for t in _split3(c))
        qx = jnp.where(col == 0, hi, jnp.where(col == 1, mid, jnp.where(col == 2, lo,
                       jnp.where(col < 6, one, zero))))
        kx = jnp.where(col < 3, one, jnp.where(col == 3, -hi, jnp.where(col == 4, -mid,
                       jnp.where(col == 5, -lo, zero))))
        return qx.astype(BF16), kx.astype(BF16)

    lfm = jnp.where(row < N_META, lfm_ref[...], 0.0)
    cum_m = prefix(lfm)
    for h in range(N_HEADS):
        _, kx = ext(cum_m, h)
        kxm_ref[h] = kx
    carry = cum_m[BLK - 1:BLK, :]
    for b in range(n_blk):
        cum = prefix(lf_ref[b * BLK:(b + 1) * BLK, :]) + carry
        carry = cum[BLK - 1:BLK, :]
        for h in range(N_HEADS):
            qx, kx = ext(cum, h)
            qx_ref[h, b * BLK:(b + 1) * BLK, :] = qx
            kx_ref[h, b * BLK:(b + 1) * BLK, :] = kx


def _cumgate(lf_meta, lf_real, n_batch, seq):
    kern = functools.partial(_cumgate_kernel, n_blk=seq // BLK)
    return pl.pallas_call(
        kern,
        grid=(n_batch,),
        in_specs=[
            pl.BlockSpec((BLK, LANES), lambda b: (0, 0)),
            pl.BlockSpec((seq, LANES), lambda b: (b, 0)),
        ],
        out_specs=[
            pl.BlockSpec((N_HEADS, seq, LANES), lambda b: (0, b, 0)),
            pl.BlockSpec((N_HEADS, seq, LANES), lambda b: (0, b, 0)),
            pl.BlockSpec((N_HEADS, BLK, LANES), lambda b: (0, 0, 0)),
        ],
        out_shape=[
            jax.ShapeDtypeStruct((N_HEADS, n_batch * seq, LANES), BF16),
            jax.ShapeDtypeStruct((N_HEADS, n_batch * seq, LANES), BF16),
            jax.ShapeDtypeStruct((N_HEADS, BLK, LANES), BF16),
        ],
        compiler_params=_cparams(("arbitrary",)),
        name="cumgate",
    )(lf_meta, lf_real)


def _bdot_nt(a, b):
    return lax.dot_general(a, b, (((2,), (2,)), ((0,), (0,))), preferred_element_type=F32)


def _bdot_nn(a, b):
    return lax.dot_general(a, b, (((2,), (1,)), ((0,), (0,))), preferred_element_type=F32)


def _fox_kernel(bound_ref, q_ref, qx_ref, k_ref, kx_ref, v_ref, g_ref, km_ref, kxm_ref, vm_ref, gain_ref,
                o_ref, m_scr, l_scr, acc_scr):
    i = pl.program_id(1)

    def sweep(kb, kxb, vb, mask, first):
        qa = jnp.concatenate([q_ref[...], qx_ref[...]], axis=2)
        ka = jnp.concatenate([kb, kxb], axis=2)
        s = _bdot_nt(qa, ka)
        if mask is not None:
            s = jnp.where(mask[None], s, -jnp.inf)
        m_cur = jnp.max(s, axis=2, keepdims=True)
        v1 = jnp.concatenate([vb, jnp.ones_like(vb)], axis=2)
        if first:
            m_col = m_cur
            pv = _bdot_nn(jnp.exp(s - m_col).astype(BF16), v1)
            l_scr[...] = pv[:, :, BLK:]
            acc_scr[...] = pv[:, :, :BLK]
            m_scr[...] = jnp.broadcast_to(m_col, m_scr.shape)
        else:
            m_prev = m_scr[...]
            m_new = jnp.maximum(m_prev, m_cur)
            m_col = m_new[:, :, 0:1]
            alpha = jnp.exp(m_prev - m_new)
            pv = _bdot_nn(jnp.exp(s - m_new).astype(BF16), v1)
            l_scr[...] = alpha * l_scr[...] + pv[:, :, BLK:]
            acc_scr[...] = alpha * acc_scr[...] + pv[:, :, :BLK]
            m_scr[...] = m_new
        return jnp.max(s[:, :, 0:1] - m_col)

    def real_block(start, width):
        start = pl.multiple_of(start, BLK)
        return (k_ref[:, pl.ds(start, width), :], kx_ref[:, pl.ds(start, width), :],
                v_ref[:, pl.ds(start, width), :])

    bound = bound_ref[0]
    first_start = jnp.maximum(i - (FOX_FIRST_BLOCKS - 1), 0) * BLK
    row2 = lax.broadcasted_iota(jnp.int32, (BLK, FOX_FIRST_BLOCKS * BLK), 0)
    col2 = lax.broadcasted_iota(jnp.int32, (BLK, FOX_FIRST_BLOCKS * BLK), 1)
    causal = col2 + (first_start - i * BLK) <= row2
    gap0 = sweep(*real_block(first_start, FOX_FIRST_BLOCKS * BLK), causal, True)

    def cond(c):
        j, done = c
        return jnp.logical_and(j >= 0, done == 0)

    def body(c):
        j, _ = c
        gap = sweep(*real_block(j * BLK, BLK), None, False)
        return j - 1, (gap + bound < -SKIP_LOG).astype(jnp.int32)

    _, done = lax.while_loop(cond, body, (i - FOX_FIRST_BLOCKS, (gap0 + bound < -SKIP_LOG).astype(jnp.int32)))

    @pl.when(done == 0)
    def _():
        col = lax.broadcasted_iota(jnp.int32, (BLK, BLK), 1)
        sweep(km_ref[...], kxm_ref[...], vm_ref[...], col < N_META, False)

    o = acc_scr[...] / l_scr[...]
    ms = jnp.mean(o * o, axis=-1, keepdims=True)
    gate = 1.0 / (1.0 + jnp.exp(-g_ref[...].astype(F32)))
    o_ref[...] = (o * lax.rsqrt(ms + EPS) * gain_ref[...] * gate).astype(BF16)


def _sb_kernel(q_ref, k_ref, v_ref, km_ref, vm_ref, gain_ref, o_ref, carry_scr, acc_scr):
    i = pl.program_id(1)

    def suffix_operator(width):
        r = lax.broadcasted_iota(jnp.int32, (width, width + BLK), 0)
        c = lax.broadcasted_iota(jnp.int32, (width, width + BLK), 1)
        return jnp.logical_or(c >= width, r > c).astype(BF16)

    def sweep(kb, vb, mask, first):
        width = kb.shape[1]
        suffix = suffix_operator(width)
        z = _bdot_nt(q_ref[...], kb)
        sp = jnp.maximum(z, 0.0) + jnp.log(1.0 + jnp.exp(-jnp.abs(z)))
        lk = -sp
        if mask is not None:
            lk = jnp.where(mask[None], lk, 0.0)
        hi = lk.astype(BF16)
        lo = (lk - hi.astype(F32)).astype(BF16)
        t = (jnp.dot(hi.reshape(N_HEADS * BLK, width), suffix, preferred_element_type=F32)
             + jnp.dot(lo.reshape(N_HEADS * BLK, width), suffix, preferred_element_type=F32))
        t = t.reshape(N_HEADS, BLK, width + BLK)
        later = t[:, :, :width]
        rowsum = t[:, :, width:]
        if not first:
            later = later + carry_scr[...]
        a = jnp.exp(z - sp + later)
        if mask is not None:
            a = jnp.where(mask[None], a, 0.0)
        pv = _bdot_nn(a.astype(BF16), vb)
        if first:
            acc_scr[...] = pv
            c_new = rowsum
        else:
            acc_scr[...] = acc_scr[...] + pv
            c_new = carry_scr[...] + rowsum
        carry_scr[...] = c_new
        return jnp.max(c_new[:, :, 0:1])

    def real_block(start, width):
        start = pl.multiple_of(start, BLK)
        return k_ref[:, pl.ds(start, width), :], v_ref[:, pl.ds(start, width), :]

    first_start = jnp.maximum(i - 1, 0) * BLK
    row2 = lax.broadcasted_iota(jnp.int32, (BLK, 2 * BLK), 0)
    col2 = lax.broadcasted_iota(jnp.int32, (BLK, 2 * BLK), 1)
    strict = col2 + (first_start - i * BLK) < row2
    top0 = sweep(*real_block(first_start, 2 * BLK), strict, True)

    def cond(c):
        j, done = c
        return jnp.logical_and(j >= 0, done == 0)

    def body(c):
        j, _ = c
        top = sweep(*real_block(j * BLK, BLK), None, False)
        return j - 1, (top < -SKIP_LOG).astype(jnp.int32)

    _, done = lax.while_loop(cond, body, (i - 2, (top0 < -SKIP_LOG).astype(jnp.int32)))

    @pl.when(done == 0)
    def _():
        col = lax.broadcasted_iota(jnp.int32, (BLK, BLK), 1)
        sweep(km_ref[...], vm_ref[...], col < N_META, False)

    o = acc_scr[...]
    ms = jnp.mean(o * o, axis=-1, keepdims=True)
    o_ref[...] = (o * lax.rsqrt(ms + EPS) * gain_ref[...]).astype(BF16)


def _head_spec_q(group, nq):
    return pl.BlockSpec((N_HEADS, BLK, HEAD_DIM), lambda b, i: (group, b * nq + i, 0))


def _head_spec_kv(group, seq):
    return pl.BlockSpec((N_HEADS, seq, HEAD_DIM), lambda b, i: (group, b, 0))


def _head_spec_meta(group):
    return pl.BlockSpec((N_HEADS, BLK, HEAD_DIM), lambda b, i: (group, 0, 0))


def _fox_attention(bound, proj, proj_m, qx, kx, kxm, out_gain, n_batch, seq):
    nq = seq // BLK
    rows = n_batch * seq
    return pl.pallas_call(
        _fox_kernel,
        grid=(n_batch, nq),
        in_specs=[
            pl.BlockSpec(memory_space=pltpu.SMEM),
            _head_spec_q(0, nq),
            _head_spec_q(0, nq),
            _head_spec_kv(1, seq),
            _head_spec_kv(0, seq),
            _head_spec_kv(2, seq),
            _head_spec_q(3, nq),
            _head_spec_meta(1),
            _head_spec_meta(0),
            _head_spec_meta(2),
            pl.BlockSpec((N_HEADS, 1, HEAD_DIM), lambda b, i: (0, 0, 0)),
        ],
        out_specs=pl.BlockSpec((N_HEADS, BLK, HEAD_DIM), lambda b, i: (0, b * nq + i, 0)),
        out_shape=jax.ShapeDtypeStruct((N_HEADS, rows, HEAD_DIM), BF16),
        scratch_shapes=[pltpu.VMEM((N_HEADS, BLK, LANES), F32)] * 3,
        compiler_params=_cparams(("parallel", "arbitrary")),
        name="fox_attention",
    )(bound, proj, qx, proj, kx, proj, proj, proj_m, kxm, proj_m, out_gain)


def _sb_attention(proj, proj_m, out_gain, n_batch, seq):
    nq = seq // BLK
    rows = n_batch * seq
    return pl.pallas_call(
        _sb_kernel,
        grid=(n_batch, nq),
        in_specs=[
            _head_spec_q(4, nq),
            _head_spec_kv(5, seq),
            _head_spec_kv(6, seq),
            _head_spec_meta(5),
            _head_spec_meta(6),
            pl.BlockSpec((N_HEADS, 1, HEAD_DIM), lambda b, i: (0, 0, 0)),
        ],
        out_specs=pl.BlockSpec((N_HEADS, BLK, HEAD_DIM), lambda b, i: (0, b * nq + i, 0)),
        out_shape=jax.ShapeDtypeStruct((N_HEADS, rows, HEAD_DIM), BF16),
        scratch_shapes=[pltpu.VMEM((N_HEADS, BLK, LANES), F32)] * 2,
        compiler_params=_cparams(("parallel", "arbitrary")),
        name="sb_attention",
    )(proj, proj, proj, proj_m, proj_m, out_gain)


ROUTE_TM = 512


def _router_kernel(oa_ref, ob_ref, wo_ref, x_ref, g_ref, whl_ref, b_ref,
                   h_ref, u_ref, route_ref, w0_ref, w1_ref, cnt_ref, carry_scr):
    step = pl.program_id(0)
    lhs = jnp.concatenate([oa_ref[h] for h in range(N_HEADS)] + [ob_ref[h] for h in range(N_HEADS)], axis=1)
    x = x_ref[...] + jnp.dot(lhs, wo_ref[...], preferred_element_type=F32)
    h_ref[...] = x
    tm = x.shape[0]
    ms = jnp.mean(x * x, axis=-1, keepdims=True)
    u = x * lax.rsqrt(ms + EPS) * g_ref[...]
    u_ref[...] = _pack_bf16_pairs(u)
    uhi = u.astype(BF16)
    ulo = (u - uhi.astype(F32)).astype(BF16)
    both = jnp.dot(uhi, whl_ref[...], preferred_element_type=F32)
    logits = (both[:, :LANES] + both[:, LANES:]
              + jnp.dot(ulo, whl_ref[:, :LANES], preferred_element_type=F32)) + b_ref[...]
    lane_i = lax.broadcasted_iota(jnp.int32, (tm, LANES), 1)
    lane = lane_i.astype(F32)
    big = float(4 * LANES)
    neg = -jnp.inf
    c = jnp.where(jnp.logical_and(lane_i >= N_EXPERTS, lane_i < N_EXPERTS + N_GROUPS), logits, neg)
    cmax = jnp.max(c, axis=1, keepdims=True)
    g_sel = jnp.min(jnp.where(c == cmax, lane, big), axis=1, keepdims=True) - N_EXPERTS
    g_gate = 1.0 / jnp.sum(jnp.exp(c - cmax), axis=1, keepdims=True)
    lo = g_sel * EXPERTS_PER_GROUP
    in_group = jnp.logical_and(lane >= lo, lane < lo + EXPERTS_PER_GROUP)
    f = jnp.where(in_group, logits, neg)
    t1 = jnp.max(f, axis=1, keepdims=True)
    i1 = jnp.min(jnp.where(f == t1, lane, big), axis=1, keepdims=True)
    f2 = jnp.where(lane == i1, neg, f)
    t2 = jnp.max(f2, axis=1, keepdims=True)
    i2 = jnp.min(jnp.where(f2 == t2, lane, big), axis=1, keepdims=True)
    d = jnp.exp(t2 - t1)
    w_first = g_gate / (1.0 + d)
    w0_ref[...] = jnp.broadcast_to(w_first, (tm, LANES))
    w1_ref[...] = jnp.broadcast_to(w_first * d, (tm, LANES))

    @pl.when(step == 0)
    def _():
        carry_scr[...] = jnp.zeros_like(carry_scr)

    oh0 = (lane == i1).astype(F32)
    oh1 = (lane == i2).astype(F32)
    oh = oh0 + oh1
    r = lax.broadcasted_iota(jnp.int32, (tm, tm), 0)
    cc = lax.broadcasted_iota(jnp.int32, (tm, tm), 1)
    before = (cc < r).astype(BF16)
    seen = jnp.dot(before, oh.astype(BF16), preferred_element_type=F32) + carry_scr[0:1, :]
    rank0 = jnp.sum(oh0 * seen, axis=1, keepdims=True)
    rank1 = jnp.sum(oh1 * seen, axis=1, keepdims=True)
    total = seen[tm - 1:tm, :] + oh[tm - 1:tm, :]
    carry_scr[...] = jnp.broadcast_to(total, carry_scr.shape)
    cnt_ref[...] = jnp.broadcast_to(total, cnt_ref.shape).astype(jnp.int32)
    vals = jnp.where(lane_i == 0, i1, jnp.where(lane_i == 1, i2, jnp.where(lane_i == 2, rank0,
                     jnp.where(lane_i == 3, rank1, 0.0))))
    route_ref[...] = vals.astype(jnp.int32)


def _outproj_router(oa, ob, wo_bf, x2d, norm_g, whl, b_pad):
    rows = x2d.shape[0]
    tm = ROUTE_TM
    return pl.pallas_call(
        _router_kernel,
        grid=(rows // tm,),
        in_specs=[
            pl.BlockSpec((N_HEADS, tm, HEAD_DIM), lambda i: (0, i, 0)),
            pl.BlockSpec((N_HEADS, tm, HEAD_DIM), lambda i: (0, i, 0)),
            pl.BlockSpec((2 * D_GROUP, D_MODEL), lambda i: (0, 0)),
            pl.BlockSpec((tm, D_MODEL), lambda i: (i, 0)),
            pl.BlockSpec((1, D_MODEL), lambda i: (0, 0)),
            pl.BlockSpec((D_MODEL, 2 * LANES), lambda i: (0, 0)),
            pl.BlockSpec((1, LANES), lambda i: (0, 0)),
        ],
        out_specs=[
            pl.BlockSpec((tm, D_MODEL), lambda i: (i, 0)),
            pl.BlockSpec((tm, D_PACKED), lambda i: (i, 0)),
            pl.BlockSpec((tm, LANES), lambda i: (i, 0)),
            pl.BlockSpec((tm, LANES), lambda i: (i, 0)),
            pl.BlockSpec((tm, LANES), lambda i: (i, 0)),
            pl.BlockSpec((8, LANES), lambda i: (0, 0)),
        ],
        out_shape=[
            jax.ShapeDtypeStruct((rows, D_MODEL), F32),
            jax.ShapeDtypeStruct((rows, D_PACKED), jnp.uint32),
            jax.ShapeDtypeStruct((rows, LANES), jnp.int32),
            jax.ShapeDtypeStruct((rows, LANES), F32),
            jax.ShapeDtypeStruct((rows, LANES), F32),
            jax.ShapeDtypeStruct((8, LANES), jnp.int32),
        ],
        scratch_shapes=[pltpu.VMEM((8, LANES), F32)],
        compiler_params=_cparams(("arbitrary",)),
        name="outproj_router",
    )(oa, ob, wo_bf, x2d, norm_g, whl, b_pad)


MOVE_TM = 512


def _scatter_kernel(pos_ref, u_ref, xs_ref, zero_scr, sem, pad_sem, *, n_rows):
    base = pl.program_id(0) * MOVE_TM

    @pl.when(pl.program_id(0) == 0)
    def _():
        zero_scr[...] = jnp.zeros_like(zero_scr)
        pad = pltpu.make_async_copy(zero_scr, xs_ref.at[pl.ds(n_rows, FFN_WIN)], pad_sem)
        pad.start()
        pad.wait()

    for t in range(MOVE_TM):
        for k in range(TOP_K):
            dst = pos_ref[(base + t) * TOP_K + k]
            pltpu.make_async_copy(u_ref.at[pl.ds(t, 1)], xs_ref.at[pl.ds(dst, 1)], sem).start(priority=k)
    for _ in range(TOP_K):
        pltpu.make_async_copy(u_ref, xs_ref.at[pl.ds(0, MOVE_TM)], sem).wait()


def _scatter_rows(pos_flat, u2):
    n_tok = u2.shape[0]
    n_rows = n_tok * TOP_K
    return pl.pallas_call(
        functools.partial(_scatter_kernel, n_rows=n_rows),
        grid_spec=pltpu.PrefetchScalarGridSpec(
            num_scalar_prefetch=1,
            grid=(n_tok // MOVE_TM,),
            in_specs=[pl.BlockSpec((MOVE_TM, D_PACKED), lambda i, pos: (i, 0))],
            out_specs=pl.BlockSpec(memory_space=pl.ANY),
            scratch_shapes=[pltpu.VMEM((FFN_WIN, D_PACKED), u2.dtype), pltpu.SemaphoreType.DMA(()),
                            pltpu.SemaphoreType.DMA(())],
        ),
        out_shape=jax.ShapeDtypeStruct((n_rows + FFN_WIN, D_PACKED), u2.dtype),
        compiler_params=_cparams(("arbitrary",)),
        name="scatter_rows",
    )(pos_flat, u2)


FFN_TM = 152
SUBLANES = 8
FFN_WIN = FFN_TM + SUBLANES
WEIGHT_SLOTS = 3


def _ffn_kernel(row_ref, tile_ref, exp_ref, valid_ref, new_ref, slot_ref, next_ref, next2_ref,
                x_ref, w1_hbm, w3_hbm, w2_hbm, o_ref,
                w1_buf, w3_buf, w2_buf, w1_bf, w3_bf, w2_bf, sem):
    it = pl.program_id(0)
    slot = slot_ref[it]

    def weight_copies(expert, s):
        half = D_EXPERT // 2
        return ((pltpu.make_async_copy(w1_hbm.at[expert], w1_buf.at[s], sem.at[s, 0]), 0),
                (pltpu.make_async_copy(w3_hbm.at[expert], w3_buf.at[s], sem.at[s, 1]), 1),
                (pltpu.make_async_copy(w2_hbm.at[expert, pl.ds(0, half)], w2_buf.at[s, pl.ds(0, half)],
                                       sem.at[s, 2]), 0),
                (pltpu.make_async_copy(w2_hbm.at[expert, pl.ds(half, half)], w2_buf.at[s, pl.ds(half, half)],
                                       sem.at[s, 3]), 1))

    def next_slot(s, ahead):
        s = s + ahead
        return jnp.where(s >= WEIGHT_SLOTS, s - WEIGHT_SLOTS, s)

    @pl.when(it == 0)
    def _():
        for cp, queue in weight_copies(exp_ref[0], slot):
            cp.start(priority=queue)

        @pl.when(next_ref[0] >= 0)
        def _():
            for cp, queue in weight_copies(next_ref[0], next_slot(slot, 1)):
                cp.start(priority=queue)

    @pl.when(new_ref[it] == 1)
    def _():
        @pl.when(next2_ref[it] >= 0)
        def _():
            for cp, queue in weight_copies(next2_ref[it], next_slot(slot, 2)):
                cp.start(priority=queue)

        for cp, _ in weight_copies(exp_ref[it], slot):
            cp.wait()

        w1_bf[...] = w1_buf[slot].astype(BF16)
        w3_bf[...] = w3_buf[slot].astype(BF16)
        w2_bf[...] = w2_buf[slot].astype(BF16)

    @pl.when(valid_ref[it] == 1)
    def _():
        x = _unpack_bf16_pairs(x_ref[...]).astype(BF16)
        a = jnp.dot(x, w1_bf[...], preferred_element_type=F32)
        b = jnp.dot(x, w3_bf[...], preferred_element_type=F32)
        mid = (a / (1.0 + jnp.exp(-a)) * b).astype(BF16)
        o_ref[...] = _pack_bf16_pairs(jnp.dot(mid, w2_bf[...], preferred_element_type=F32))

    @pl.when(valid_ref[it] == 0)
    def _():
        o_ref[...] = jnp.zeros_like(o_ref)


def _grouped_ffn(items, xs, w1, w3, w2):
    n_items = items[0].shape[0]

    def row_map(i, row_group, *_):
        return (row_group[i] * SUBLANES, 0)

    def tile_map(i, row, tile, *_):
        return (tile[i], 0)

    return pl.pallas_call(
        _ffn_kernel,
        grid_spec=pltpu.PrefetchScalarGridSpec(
            num_scalar_prefetch=len(items),
            grid=(n_items,),
            in_specs=[
                pl.BlockSpec((pl.Element(FFN_WIN), pl.Element(D_PACKED)), row_map),
                pl.BlockSpec(memory_space=pl.ANY),
                pl.BlockSpec(memory_space=pl.ANY),
                pl.BlockSpec(memory_space=pl.ANY),
            ],
            out_specs=pl.BlockSpec((FFN_WIN, D_PACKED), tile_map),
            scratch_shapes=[
                pltpu.VMEM((WEIGHT_SLOTS, D_MODEL, D_EXPERT), F32),
                pltpu.VMEM((WEIGHT_SLOTS, D_MODEL, D_EXPERT), F32),
                pltpu.VMEM((WEIGHT_SLOTS, D_EXPERT, D_MODEL), F32),
                pltpu.VMEM((D_MODEL, D_EXPERT), BF16),
                pltpu.VMEM((D_MODEL, D_EXPERT), BF16),
                pltpu.VMEM((D_EXPERT, D_MODEL), BF16),
                pltpu.SemaphoreType.DMA((WEIGHT_SLOTS, 4)),
            ],
        ),
        out_shape=jax.ShapeDtypeStruct((n_items * FFN_WIN, D_PACKED), jnp.uint32),
        compiler_params=_cparams(("arbitrary",)),
        name="grouped_ffn",
    )(*items, xs, w1, w3, w2)


def _combine_kernel(pos_ref, h_ref, w0_ref, w1_ref, ys_ref, o_ref, ybuf, sem):
    step = pl.program_id(0)
    n_steps = pl.num_programs(0)

    def fetch(s, slot):
        for t in range(MOVE_TM):
            for k in range(TOP_K):
                src = pos_ref[(s * MOVE_TM + t) * TOP_K + k]
                pltpu.make_async_copy(ys_ref.at[pl.ds(src, 1)], ybuf.at[slot, k, pl.ds(t, 1)],
                                      sem.at[slot]).start(priority=k)

    @pl.when(step == 0)
    def _():
        fetch(0, 0)

    slot = step % 2

    @pl.when(step + 1 < n_steps)
    def _():
        fetch(step + 1, 1 - slot)

    for k in range(TOP_K):
        pltpu.make_async_copy(ys_ref.at[pl.ds(0, MOVE_TM)], ybuf.at[slot, k], sem.at[slot]).wait()
    reps = D_MODEL // LANES
    w0 = jnp.concatenate([w0_ref[...]] * reps, axis=1)
    w1 = jnp.concatenate([w1_ref[...]] * reps, axis=1)
    o_ref[...] = (h_ref[...] + w0 * _unpack_bf16_pairs(ybuf[slot, 0])
                  + w1 * _unpack_bf16_pairs(ybuf[slot, 1]))


def _combine(pos_flat, h1, w0b, w1b, ys):
    rows = h1.shape[0]
    tm = MOVE_TM
    return pl.pallas_call(
        _combine_kernel,
        grid_spec=pltpu.PrefetchScalarGridSpec(
            num_scalar_prefetch=1,
            grid=(rows // tm,),
            in_specs=[
                pl.BlockSpec((tm, D_MODEL), lambda i, pos: (i, 0)),
                pl.BlockSpec((tm, LANES), lambda i, pos: (i, 0)),
                pl.BlockSpec((tm, LANES), lambda i, pos: (i, 0)),
                pl.BlockSpec(memory_space=pl.ANY),
            ],
            out_specs=pl.BlockSpec((tm, D_MODEL), lambda i, pos: (i, 0)),
            scratch_shapes=[pltpu.VMEM((2, TOP_K, tm, D_PACKED), jnp.uint32), pltpu.SemaphoreType.DMA((2,))],
        ),
        out_shape=jax.ShapeDtypeStruct((rows, D_MODEL), F32),
        compiler_params=_cparams(("arbitrary",)),
        name="combine",
    )(pos_flat, h1, w0b, w1b, ys)


def _work_items(counts, n_assign):
    counts = counts.astype(jnp.int32)
    starts = jnp.cumsum(counts) - counts
    chunks = (counts + FFN_TM - 1) // FFN_TM
    chunk_end = jnp.cumsum(chunks)
    chunk_start = chunk_end - chunks
    n_items = n_assign // FFN_TM + N_EXPERTS
    n_real = chunk_end[-1]
    idx = jnp.minimum(jnp.arange(n_items, dtype=jnp.int32), n_real - 1)
    item_valid = (jnp.arange(n_items, dtype=jnp.int32) < n_real).astype(jnp.int32)
    item_exp = jnp.sum((chunk_end[None, :] <= idx[:, None]).astype(jnp.int32), axis=1).astype(jnp.int32)
    onehot = item_exp[:, None] == jnp.arange(N_EXPERTS, dtype=jnp.int32)[None, :]
    exp_start = jnp.sum(jnp.where(onehot, starts[None, :], 0), axis=1)
    exp_chunk0 = jnp.sum(jnp.where(onehot, chunk_start[None, :], 0), axis=1)
    item_row = ((exp_start + (idx - exp_chunk0) * FFN_TM) // SUBLANES).astype(jnp.int32)
    item_new = jnp.concatenate([jnp.ones((1,), jnp.int32), (item_exp[1:] != item_exp[:-1]).astype(jnp.int32)])
    ordinal = jnp.cumsum(item_new) - 1
    run_exp = jnp.full((n_items + 2,), -1, jnp.int32).at[ordinal].set(item_exp)
    item_next = run_exp[ordinal + 1]
    item_next2 = run_exp[ordinal + 2]
    item_slot = (ordinal % WEIGHT_SLOTS).astype(jnp.int32)
    item_tile = jnp.arange(n_items, dtype=jnp.int32)
    items = (item_row, item_tile, item_exp, item_valid, item_new, item_slot, item_next, item_next2)
    return starts.astype(jnp.int32), chunk_start.astype(jnp.int32), items


def kernel(x, meta_tokens, norm1_g, w_in, b_f, q_gain, k_gain, fox_out_gain, sb_out_gain, w_out, norm2_g,
           w_coarse, b_coarse, w_fine, b_fine, w1, w3, w2):
    assert norm1_g.shape[0] == 1, "single-layer block"
    n_batch, seq, _ = x.shape
    n_tok = n_batch * seq
    n_assign = n_tok * TOP_K
    scale = HEAD_DIM ** -0.5
    x2d = x.reshape(n_tok, D_MODEL)

    w_in_t = w_in[0].T
    w_out_bf = _cast_bf16(w_out[0], 2 * D_GROUP)
    wft_bf = jnp.pad(w_in_t[D_PROJ:], ((0, LANES - N_HEADS), (0, 0))).astype(BF16)
    bf_pad = jnp.pad(b_f[0], (0, LANES - N_HEADS)).reshape(1, LANES)
    ones = jnp.ones((HEAD_DIM,), F32)
    zeros = jnp.zeros((HEAD_DIM,), F32)
    gains = jnp.stack([jnp.stack([q_gain[0] * scale, ones]), jnp.stack([k_gain[0], ones]),
                       jnp.stack([ones, zeros]), jnp.stack([ones, zeros]), jnp.stack([ones * scale, zeros]),
                       jnp.stack([ones, zeros]), jnp.stack([ones, zeros])])
    gains = jnp.repeat(gains, N_HEADS, axis=0)
    meta_pad = jnp.pad(meta_tokens.astype(F32), ((0, BLK - N_META), (0, 0)))
    n1 = norm1_g[0].reshape(1, D_MODEL)

    proj_m, lf_m, w_in_bf = _inproj(meta_pad, n1, w_in_t, wft_bf, bf_pad, gains, tm=BLK)
    proj, lf = _inproj(x2d, n1, w_in_bf, wft_bf, bf_pad, gains, tm=1024, tn=INPROJ_TN)
    qx, kx, kxm = _cumgate(lf_m, lf, n_batch, seq)
    qk_bound = 1.02 * HEAD_DIM * scale * jnp.max(jnp.abs(q_gain[0])) * jnp.max(jnp.abs(k_gain[0]))
    bound = (2.0 * qk_bound).reshape(1).astype(F32)
    oa = _fox_attention(bound, proj, proj_m, qx, kx, kxm, fox_out_gain[0].reshape(N_HEADS, 1, HEAD_DIM),
                        n_batch, seq)
    ob = _sb_attention(proj, proj_m, sb_out_gain[0].reshape(N_HEADS, 1, HEAD_DIM), n_batch, seq)

    wr = jnp.pad(jnp.concatenate([w_fine[0], w_coarse[0]], axis=1),
                 ((0, 0), (0, LANES - N_GROUPS - N_EXPERTS)))
    wr_hi = wr.astype(BF16)
    wr_hl = jnp.concatenate([wr_hi, (wr - wr_hi.astype(F32)).astype(BF16)], axis=1)
    br = jnp.pad(jnp.concatenate([b_fine[0], b_coarse[0]]), (0, LANES - N_GROUPS - N_EXPERTS)).reshape(1, LANES)
    h1, u2, route, w0b, w1b, cnt = _outproj_router(oa, ob, w_out_bf, x2d, norm2_g[0].reshape(1, D_MODEL),
                                                   wr_hl, br)
    starts, chunk_starts, items = _work_items(cnt[0, :N_EXPERTS], n_assign)
    eid = route[:, 0:TOP_K]
    rank = route[:, TOP_K:2 * TOP_K]
    is_exp = eid[..., None] == jnp.arange(N_EXPERTS, dtype=jnp.int32)
    start_of = jnp.sum(jnp.where(is_exp, starts, 0), axis=-1)
    chunk = rank // FFN_TM
    item_of = jnp.sum(jnp.where(is_exp, chunk_starts, 0), axis=-1) + chunk
    pos_in = start_of + rank
    window_row = (start_of + chunk * FFN_TM) // SUBLANES * SUBLANES
    pos_out = (item_of * FFN_WIN + (pos_in - window_row)).reshape(-1)
    pos_in = pos_in.reshape(-1)
    xs = _scatter_rows(pos_in, u2)
    ys = _grouped_ffn(items, xs, w1[0], w3[0], w2[0])
    out = _combine(pos_out, h1, w0b, w1b, ys)
    return out.reshape(n_batch, seq, D_MODEL)
```
